```python
import math
import jax
import jax.numpy as jnp
from jax import lax
import numpy as np

D_MODEL = 2048
BATCH = 4
SEQ = 2048
DEPTH = 2
DEC_BATCH = 8
DEC_SEQ = 8
PAST_LEN = 16384
PAGE_SIZE = 128

RW_W = D_MODEL // 4
RW_DH = 64
RW_H = RW_W // RW_DH
R_W = max(32, int(round(1.8 * D_MODEL ** 0.5 / 32)) * 32)
R_A = max(32, int(round(1.8 * D_MODEL ** 0.5 / 32)) * 32)
R_G = max(32, int(round(0.6 * D_MODEL ** 0.8 / 32)) * 32)
RW_SPLITS = (RW_W, RW_W, RW_W, R_W, R_A, R_G)
RW_COLS = sum(RW_SPLITS)
RW_LN_EPS = 64e-5
RW_DECAY_CLAMP = 0.5
ML_W = D_MODEL // 4
ML_H = 4
ML_DH = ML_W // ML_H
ML_QK = 2 * ML_W
ML_CONV = 4
ML_CHUNK = 64
ML_SPLITS = (ML_QK, ML_W, ML_H, ML_H, ML_W)
ML_COLS = sum(ML_SPLITS)
ML_GN_EPS = 1e-5
FX_W = D_MODEL // 2
FX_DH = 128
FX_H = FX_W // FX_DH
FX_BLOCK = 128
FX_SPLITS = (FX_W, FX_W, FX_W, FX_H)
FX_COLS = sum(FX_SPLITS)
N_BRANCH = 3
GATE_COLS = N_BRANCH * D_MODEL
IN_SPLITS = (RW_COLS, ML_COLS, FX_COLS, GATE_COLS)
IN_COLS = sum(IN_SPLITS)
D_FF = ((8 * D_MODEL // 3 + 127) // 128) * 128
FFN_CONV = 3
N_MOD = 6
NORM_EPS = 1e-6
STATE_NAMES = ("fox_k", "fox_v", "fox_logf", "rw_shift", "rw_wkv", "ml_conv", "ml_c", "ml_n", "ml_m", "ffn_conv")

kernel_name = "hybrid_rwkv7_mlstm_fox_decode_step"


def _split(x, sizes):
    idx, acc = [], 0
    for s in sizes[:-1]:
        acc += s
        idx.append(acc)
    return jnp.split(x, idx, axis=-1)


def _rmsnorm(x, g):
    x32 = x.astype(jnp.float32)
    y = x32 * lax.rsqrt(jnp.mean(x32 * x32, axis=-1, keepdims=True) + NORM_EPS)
    return (y * g.astype(jnp.float32)).astype(x.dtype)


def _head_layernorm(h, eps):
    mu = jnp.mean(h, axis=-1, keepdims=True)
    var = jnp.mean(jnp.square(h - mu), axis=-1, keepdims=True)
    return (h - mu) * lax.rsqrt(var + eps)


def _causal_dwconv(x, buf, w, b):
    T = x.shape[1]
    cat = jnp.concatenate([buf.astype(x.dtype), x], axis=1)
    y = b
    for j in range(w.shape[0]):
        y = y + cat[:, j:j + T] * w[j]
    return y, cat[:, T:]


def _rwkv7_mix(p, shift_prev, wkv0, mu, w0, w_up, a0, a_up, g_up, k_k, k_a, r_k, ln_w, ln_b):
    B, T, _ = p.shape
    f32 = jnp.float32
    p32 = p.astype(f32)
    p_prev = jnp.concatenate([shift_prev[:, None].astype(f32), p32[:, :-1]], axis=1)
    xs = p32 + (p_prev - p32) * mu
    r, k, v, dw, da, dg = _split(xs, RW_SPLITS)
    w_raw = -jax.nn.softplus(-(w0 + jnp.tanh(dw) @ w_up)) - RW_DECAY_CLAMP
    log_w = -jnp.exp(w_raw)
    a = jax.nn.sigmoid(a0 + da @ a_up)
    g = jax.nn.sigmoid(dg) @ g_up
    heads = lambda z: z.reshape(B, T, RW_H, RW_DH)
    kk = heads(k * k_k)
    kk = kk / jnp.maximum(jnp.linalg.norm(kk, axis=-1, keepdims=True), 1e-12)
    k = k * (1.0 + (a - 1.0) * k_a)
    r, k, v, a, log_w = (heads(z) for z in (r, k, v, a, log_w))

    def step(S, inp):
        r_t, k_t, v_t, kk_t, a_t, lw_t = inp
        sa = jnp.einsum('bhvk,bhk->bhv', S, -kk_t)
        S = (S * jnp.exp(lw_t)[:, :, None, :]
             + sa[..., None] * (kk_t * a_t)[:, :, None, :]
             + v_t[..., None] * k_t[:, :, None, :])
        return S, jnp.einsum('bhvk,bhk->bhv', S, r_t)

    seq = tuple(jnp.moveaxis(z, 1, 0) for z in (r, k, v, kk, a, log_w))
    wkv, y = lax.scan(step, wkv0.astype(f32), seq)
    y = jnp.moveaxis(y, 0, 1)
    y = _head_layernorm(y, RW_LN_EPS) * ln_w.reshape(RW_H, RW_DH) + ln_b.reshape(RW_H, RW_DH)
    y = y + jnp.sum(r * k * r_k, axis=-1, keepdims=True) * v
    out = y.reshape(B, T, RW_W) * g
    return out.astype(p.dtype), p[:, -1], wkv.astype(wkv0.dtype)


def _mlstm_chunk(carry, inp):
    C, n, m = carry
    q, k, v, li, lf = inp
    L = q.shape[2]
    b = jnp.cumsum(lf, axis=-1)
    causal = jnp.tril(jnp.ones((L, L), dtype=bool))
    d = jnp.where(causal, b[..., :, None] - b[..., None, :] + li[..., None, :], -jnp.inf)
    inter = b + m[..., None]
    m_t = jnp.maximum(inter, jnp.max(d, axis=-1))
    w_intra = jnp.exp(d - m_t[..., None])
    w_state = jnp.exp(inter - m_t)
    s = jnp.einsum('bhtd,bhsd->bhts', q, k) * w_intra
    num = w_state[..., None] * jnp.einsum('bhvk,bhtk->bhtv', C, q) + jnp.einsum('bhts,bhsv->bhtv', s, v)
    den = w_state * jnp.einsum('bhk,bhtk->bht', n, q) + jnp.sum(s, axis=-1)
    h = num / jnp.maximum(jnp.abs(den), jnp.exp(-m_t))[..., None]
    g_end = b[..., -1]
    lw_s = g_end[..., None] - b + li
    m_new = jnp.maximum(g_end + m, jnp.max(lw_s, axis=-1))
    w_s = jnp.exp(lw_s - m_new[..., None])
    decay = jnp.exp(g_end + m - m_new)
    C = decay[..., None, None] * C + jnp.einsum('bhs,bhsv,bhsk->bhvk', w_s, v, k)
    n = decay[..., None] * n + jnp.einsum('bhs,bhsk->bhk', w_s, k)
    return (C, n, m_new), h


def _mlstm_mix(p, conv_buf, C0, n0, m0, conv_w, conv_b, b_i, b_f, gn_w):
    B, T, _ = p.shape
    f32 = jnp.float32
    qk_pre, v, ig, fg, og = _split(p, ML_SPLITS)
    qk, conv_new = _causal_dwconv(qk_pre, conv_buf, conv_w, conv_b)
    q, k = jnp.split(jax.nn.silu(qk.astype(f32)), 2, axis=-1)
    heads = lambda z: z.astype(f32).reshape(B, T, ML_H, ML_DH).transpose(0, 2, 1, 3)
    q, k, v = heads(q), heads(k) * (ML_DH ** -0.5), heads(v)
    li = (ig.astype(f32) + b_i).transpose(0, 2, 1)
    lf = jax.nn.log_sigmoid(fg.astype(f32) + b_f).transpose(0, 2, 1)
    L = math.gcd(T, ML_CHUNK)
    nc = T // L
    chunks = lambda z: jnp.moveaxis(z.reshape(z.shape[:2] + (nc, L) + z.shape[3:]), 2, 0)
    (C, n, m), h = lax.scan(_mlstm_chunk, (C0.astype(f32), n0.astype(f32), m0.astype(f32)),
                            tuple(chunks(z) for z in (q, k, v, li, lf)))
    h = jnp.moveaxis(h, 0, 2).reshape(B, ML_H, T, ML_DH).transpose(0, 2, 1, 3)
    h = _head_layernorm(h, ML_GN_EPS) * gn_w.reshape(ML_H, ML_DH)
    out = jax.nn.sigmoid(og.astype(f32)) * h.reshape(B, T, ML_W)
    return out.astype(p.dtype), conv_new, C.astype(C0.dtype), n.astype(n0.dtype), m.astype(m0.dtype)


def _fox_attention(q, Fq, qpos, k, v, Fk, kpos):
    B, T, H, Dh = q.shape
    f32 = jnp.float32
    bs = math.gcd(T, FX_BLOCK)
    nb = T // bs
    q_blocks = jnp.moveaxis(q.astype(f32).reshape(B, nb, bs, H, Dh), 1, 0)
    Fq_blocks = jnp.moveaxis(Fq.reshape(B, nb, bs, H), 1, 0)
    pos_blocks = qpos.reshape(nb, bs)
    k32, v32 = k.astype(f32), v.astype(f32)
    Fk_bhk = jnp.moveaxis(Fk, 1, 2)
    scale = Dh ** -0.5

    def block(args):
        qb, Fqb, posb = args
        s = jnp.einsum('bqhd,bkhd->bhqk', qb, k32) * scale
        s = s + jnp.moveaxis(Fqb, 1, 2)[..., None] - Fk_bhk[:, :, None, :]
        s = jnp.where(kpos[None, :] <= posb[:, None], s, -jnp.inf)
        return jnp.einsum('bhqk,bkhd->bqhd', jax.nn.softmax(s, axis=-1), v32)

    out = lax.map(block, (q_blocks, Fq_blocks, pos_blocks))
    return jnp.moveaxis(out, 0, 1).reshape(B, T, H, Dh)


def _fox_mix(p, past, b_f):
    B, T, _ = p.shape
    f32 = jnp.float32
    q, k, v, fg = _split(p, FX_SPLITS)
    heads = lambda z: z.reshape(B, T, FX_H, FX_DH)
    q, k, v = heads(q), heads(k), heads(v)
    logf = jax.nn.log_sigmoid(fg.astype(f32) + b_f)
    F_new = jnp.cumsum(logf, axis=1)
    if past is None:
        k_all, v_all, Fk, q_start = k, v, F_new, 0
    else:
        k_past, v_past, lf_past = past
        lf32 = lf_past.astype(f32)
        suffix = lax.cumsum(lf32, axis=1, reverse=True) - lf32
        k_all = jnp.concatenate([k_past.astype(k.dtype), k], axis=1)
        v_all = jnp.concatenate([v_past.astype(v.dtype), v], axis=1)
        Fk = jnp.concatenate([-suffix, F_new], axis=1)
        q_start = k_past.shape[1]
    kpos = jnp.arange(k_all.shape[1])
    qpos = q_start + jnp.arange(T)
    out = _fox_attention(q, F_new, qpos, k_all, v_all, Fk, kpos)
    return out.reshape(B, T, FX_W).astype(p.dtype), k, v, logf.astype(p.dtype)


def _conv_ffn(z, buf, w_gate, w_val, conv_w, conv_b, w_down):
    a, new_buf = _causal_dwconv(z @ w_gate, buf, conv_w, conv_b)
    h = jax.nn.gelu(a, approximate=True) * (z @ w_val)
    return h @ w_down, new_buf


def _layer(x, c, st, past, P, l):
    B, T, _ = x.shape
    mod = jax.nn.silu(c) @ P['w_ada'][l] + P['b_ada'][l]
    shift1, scale1, gate1, shift2, scale2, gate2 = jnp.split(mod[:, None, :].astype(x.dtype), N_MOD, axis=-1)
    u = _rmsnorm(x, P['norm_pre_mix'][l]) * (1 + scale1) + shift1
    p_rw, p_ml, p_fx, p_gate = _split(u @ P['w_in'][l], IN_SPLITS)
    o_rw, rw_shift, rw_wkv = _rwkv7_mix(
        p_rw, st['rw_shift'][l], st['rw_wkv'][l], P['rw_mu'][l], P['rw_w0'][l], P['rw_w_up'][l],
        P['rw_a0'][l], P['rw_a_up'][l], P['rw_g_up'][l], P['rw_k_k'][l], P['rw_k_a'][l], P['rw_r_k'][l],
        P['rw_ln_w'][l], P['rw_ln_b'][l])
    o_ml, ml_conv, ml_c, ml_n, ml_m = _mlstm_mix(
        p_ml, st['ml_conv'][l], st['ml_c'][l], st['ml_n'][l], st['ml_m'][l], P['ml_conv_w'][l],
        P['ml_conv_b'][l], P['ml_b_i'][l], P['ml_b_f'][l], P['ml_gn_w'][l])
    o_fx, fox_k, fox_v, fox_logf = _fox_mix(p_fx, past, P['fx_b_f'][l])
    gates = jax.nn.sigmoid(p_gate.astype(jnp.float32)).reshape(B, T, N_BRANCH, D_MODEL).astype(x.dtype)
    merged = (gates[:, :, 0] * (o_rw @ P['w_br_rwkv'][l])
              + gates[:, :, 1] * (o_ml @ P['w_br_mlstm'][l])
              + gates[:, :, 2] * (o_fx @ P['w_br_fox'][l]))
    x = x + gate1 * _rmsnorm(merged @ P['w_out'][l], P['norm_post_mix'][l])
    z = _rmsnorm(x, P['norm_pre_ffn'][l]) * (1 + scale2) + shift2
    f, ffn_conv = _conv_ffn(z, st['ffn_conv'][l], P['ffn_w_gate'][l], P['ffn_w_val'][l],
                            P['ffn_conv_w'][l], P['ffn_conv_b'][l], P['ffn_w_down'][l])
    x = x + gate2 * _rmsnorm(f, P['norm_post_ffn'][l])
    new = dict(fox_k=fox_k, fox_v=fox_v, fox_logf=fox_logf, rw_shift=rw_shift, rw_wkv=rw_wkv,
               ml_conv=ml_conv, ml_c=ml_c, ml_n=ml_n, ml_m=ml_m, ffn_conv=ffn_conv)
    return x, new


def _trunk(x, c, st, cache_k, cache_v, cache_lf, page_table, P):
    new = {name: [] for name in STATE_NAMES}
    for l in range(DEPTH):
        if page_table is None:
            past = None
        else:
            n_seq, n_pages = page_table.shape
            gather = lambda cache: cache[l][page_table].reshape((n_seq, n_pages * cache.shape[2]) + cache.shape[3:])
            past = (gather(cache_k), gather(cache_v), gather(cache_lf))
        x, layer_new = _layer(x, c, st, past, P, l)
        for name in STATE_NAMES:
            new[name].append(layer_new[name])
    return x, {name: jnp.stack(vals) for name, vals in new.items()}


def setup_inputs(seed: int = 0) -> dict:
    key = jax.random.key(seed)
    ks = iter(jax.random.split(key, 64))
    f32 = jnp.float32

    def nrm(shape, scale):
        return scale * jax.random.normal(next(ks), shape, f32)

    def unif(shape, lo, hi):
        return jax.random.uniform(next(ks), shape, f32, lo, hi)

    def gain(shape):
        return 1.0 + nrm(shape, 0.02)

    n_pages = PAST_LEN // PAGE_SIZE
    n_pool = (DEC_BATCH * n_pages * 5) // 4
    L, D = DEPTH, D_MODEL
    inp = {}
    inp['x_prompt'] = nrm((BATCH, SEQ, D), 1.0)
    inp['x_sample'] = nrm((DEC_BATCH, DEC_SEQ, D), 1.0)
    inp['cache_fox_k'] = nrm((L, n_pool, PAGE_SIZE, FX_H, FX_DH), 1.0)
    inp['cache_fox_v'] = nrm((L, n_pool, PAGE_SIZE, FX_H, FX_DH), 1.0)
    inp['cache_fox_logf'] = jax.nn.log_sigmoid(nrm((L, n_pool, PAGE_SIZE, FX_H), 1.0) + unif((L, 1, 1, FX_H), 2.0, 4.0))
    inp['state_rwkv_shift'] = nrm((L, DEC_BATCH, RW_COLS), 1.0)
    inp['state_rwkv_wkv'] = nrm((L, DEC_BATCH, RW_H, RW_DH, RW_DH), 0.5)
    inp['state_mlstm_conv'] = nrm((L, DEC_BATCH, ML_CONV - 1, ML_QK), 1.0)
    inp['state_mlstm_c'] = nrm((L, DEC_BATCH, ML_H, ML_DH, ML_DH), 0.1)
    inp['state_mlstm_n'] = nrm((L, DEC_BATCH, ML_H, ML_DH), 0.1)
    inp['state_mlstm_m'] = nrm((L, DEC_BATCH, ML_H), 1.0)
    inp['state_ffn_conv'] = nrm((L, DEC_BATCH, FFN_CONV - 1, D_FF), 1.0)
    inp['page_table'] = jax.random.permutation(next(ks), n_pool)[: DEC_BATCH * n_pages].reshape(DEC_BATCH, n_pages).astype(jnp.int32)
    inp['c_prompt'] = nrm((BATCH, D), 1.0)
    inp['c_sample'] = nrm((DEC_BATCH, D), 1.0)
    inp['w_ada'] = nrm((L, D, N_MOD * D), 0.5 * D ** -0.5)
    inp['b_ada'] = nrm((L, N_MOD * D), 0.02)
    inp['norm_pre_mix'] = gain((L, D))
    inp['norm_post_mix'] = gain((L, D))
    inp['norm_pre_ffn'] = gain((L, D))
    inp['norm_post_ffn'] = gain((L, D))
    inp['w_in'] = nrm((L, D, IN_COLS), D ** -0.5)
    inp['rw_mu'] = unif((L, RW_COLS), 0.0, 1.0)
    inp['rw_w0'] = nrm((L, RW_W), 0.5)
    inp['rw_w_up'] = nrm((L, R_W, RW_W), R_W ** -0.5)
    inp['rw_a0'] = nrm((L, RW_W), 0.5)
    inp['rw_a_up'] = nrm((L, R_A, RW_W), R_A ** -0.5)
    inp['rw_g_up'] = nrm((L, R_G, RW_W), R_G ** -0.5)
    inp['rw_k_k'] = 0.85 + nrm((L, RW_W), 0.02)
    inp['rw_k_a'] = 1.0 + nrm((L, RW_W), 0.02)
    inp['rw_r_k'] = nrm((L, RW_H, RW_DH), 0.1)
    inp['rw_ln_w'] = gain((L, RW_W))
    inp['rw_ln_b'] = nrm((L, RW_W), 0.02)
    inp['ml_conv_w'] = nrm((L, ML_CONV, ML_QK), ML_CONV ** -0.5)
    inp['ml_conv_b'] = nrm((L, ML_QK), 0.02)
    inp['ml_b_i'] = nrm((L, ML_H), 0.1)
    inp['ml_b_f'] = unif((L, ML_H), 3.0, 6.0)
    inp['ml_gn_w'] = gain((L, ML_W))
    inp['fx_b_f'] = unif((L, FX_H), 2.0, 4.0)
    inp['w_br_rwkv'] = nrm((L, RW_W, D), RW_W ** -0.5)
    inp['w_br_mlstm'] = nrm((L, ML_W, D), ML_W ** -0.5)
    inp['w_br_fox'] = nrm((L, FX_W, D), FX_W ** -0.5)
    inp['w_out'] = nrm((L, D, D), D ** -0.5)
    inp['ffn_w_gate'] = nrm((L, D, D_FF), D ** -0.5)
    inp['ffn_w_val'] = nrm((L, D, D_FF), D ** -0.5)
    inp['ffn_conv_w'] = nrm((L, FFN_CONV, D_FF), FFN_CONV ** -0.5)
    inp['ffn_conv_b'] = nrm((L, D_FF), 0.02)
    inp['ffn_w_down'] = nrm((L, D_FF, D), D_FF ** -0.5)
    return inp


def reference(x_prompt, x_sample, cache_fox_k, cache_fox_v, cache_fox_logf, state_rwkv_shift, state_rwkv_wkv,
              state_mlstm_conv, state_mlstm_c, state_mlstm_n, state_mlstm_m, state_ffn_conv, page_table,
              c_prompt, c_sample, w_ada, b_ada, norm_pre_mix, norm_post_mix, norm_pre_ffn, norm_post_ffn, w_in,
              rw_mu, rw_w0, rw_w_up, rw_a0, rw_a_up, rw_g_up, rw_k_k, rw_k_a, rw_r_k, rw_ln_w, rw_ln_b,
              ml_conv_w, ml_conv_b, ml_b_i, ml_b_f, ml_gn_w, fx_b_f, w_br_rwkv, w_br_mlstm, w_br_fox, w_out,
              ffn_w_gate, ffn_w_val, ffn_conv_w, ffn_conv_b, ffn_w_down):
    P = dict(w_ada=w_ada, b_ada=b_ada, norm_pre_mix=norm_pre_mix, norm_post_mix=norm_post_mix,
             norm_pre_ffn=norm_pre_ffn, norm_post_ffn=norm_post_ffn, w_in=w_in, rw_mu=rw_mu, rw_w0=rw_w0,
             rw_w_up=rw_w_up, rw_a0=rw_a0, rw_a_up=rw_a_up, rw_g_up=rw_g_up, rw_k_k=rw_k_k, rw_k_a=rw_k_a,
             rw_r_k=rw_r_k, rw_ln_w=rw_ln_w, rw_ln_b=rw_ln_b, ml_conv_w=ml_conv_w, ml_conv_b=ml_conv_b,
             ml_b_i=ml_b_i, ml_b_f=ml_b_f, ml_gn_w=ml_gn_w, fx_b_f=fx_b_f, w_br_rwkv=w_br_rwkv,
             w_br_mlstm=w_br_mlstm, w_br_fox=w_br_fox, w_out=w_out, ffn_w_gate=ffn_w_gate, ffn_w_val=ffn_w_val,
             ffn_conv_w=ffn_conv_w, ffn_conv_b=ffn_conv_b, ffn_w_down=ffn_w_down)
    B = x_prompt.shape[0]
    dt = x_prompt.dtype
    prompt_state = dict(
        rw_shift=jnp.zeros((DEPTH, B, RW_COLS), dt),
        rw_wkv=jnp.zeros((DEPTH, B, RW_H, RW_DH, RW_DH), dt),
        ml_conv=jnp.zeros((DEPTH, B, ML_CONV - 1, ML_QK), dt),
        ml_c=jnp.zeros((DEPTH, B, ML_H, ML_DH, ML_DH), dt),
        ml_n=jnp.zeros((DEPTH, B, ML_H, ML_DH), dt),
        ml_m=jnp.zeros((DEPTH, B, ML_H), dt),
        ffn_conv=jnp.zeros((DEPTH, B, FFN_CONV - 1, D_FF), dt))
    sample_state = dict(rw_shift=state_rwkv_shift, rw_wkv=state_rwkv_wkv, ml_conv=state_mlstm_conv,
                        ml_c=state_mlstm_c, ml_n=state_mlstm_n, ml_m=state_mlstm_m, ffn_conv=state_ffn_conv)
    y_prompt, sp = _trunk(x_prompt, c_prompt, prompt_state, None, None, None, None, P)
    y_sample, ss = _trunk(x_sample, c_sample, sample_state, cache_fox_k, cache_fox_v, cache_fox_logf, page_table, P)
    return (y_prompt, y_sample,
            sp['fox_k'], ss['fox_k'], sp['fox_v'], ss['fox_v'], sp['fox_logf'], ss['fox_logf'],
            sp['rw_shift'], ss['rw_shift'], sp['rw_wkv'], ss['rw_wkv'],
            sp['ml_conv'], ss['ml_conv'], sp['ml_c'], ss['ml_c'], sp['ml_n'], ss['ml_n'], sp['ml_m'], ss['ml_m'],
            sp['ffn_conv'], ss['ffn_conv'])
```

```python
import functools
import math

import jax
import jax.numpy as jnp
from jax import lax
from jax.experimental import pallas as pl
from jax.experimental.pallas import tpu as pltpu

f32 = jnp.float32
bf16 = jnp.bfloat16

D_MODEL = 2048
DEPTH = 2
PAGE_SIZE = 128
RW_W = D_MODEL // 4
RW_DH = 64
RW_H = RW_W // RW_DH
R_W = max(32, int(round(1.8 * D_MODEL ** 0.5 / 32)) * 32)
R_A = R_W
R_G = max(32, int(round(0.6 * D_MODEL ** 0.8 / 32)) * 32)
RW_COLS = 3 * RW_W + R_W + R_A + R_G
RW_LN_EPS = 64e-5
RW_DECAY_CLAMP = 0.5
ML_W = D_MODEL // 4
ML_H = 4
ML_DH = ML_W // ML_H
ML_QK = 2 * ML_W
ML_CONV = 4
ML_COLS = ML_QK + ML_W + 2 * ML_H + ML_W
ML_GN_EPS = 1e-5
FX_W = D_MODEL // 2
FX_DH = 128
FX_H = FX_W // FX_DH
FX_COLS = 3 * FX_W + FX_H
N_BRANCH = 3
GATE_COLS = N_BRANCH * D_MODEL
D_FF = ((8 * D_MODEL // 3 + 127) // 128) * 128
FFN_CONV = 3
N_MOD = 6
NORM_EPS = 1e-6

LANES = 128
SUBLANES = 8
VMEM_LIMIT = 56 * 1024 * 1024
CHUNK = 128
NEG = -1e30

RW_LORA_PAD = LANES
RW_PCOLS = 3 * RW_W + 2 * RW_LORA_PAD + R_G
OFF_GATE = 0
OFF_RW = OFF_GATE + GATE_COLS
OFF_ML = OFF_RW + RW_PCOLS
ML_PCOLS = ML_QK + 2 * ML_W
OFF_FXQ = OFF_ML + ML_PCOLS
OFF_FXK = OFF_FXQ + FX_W
OFF_FXV = OFF_FXK + FX_W
OFF_SMALL = OFF_FXV + FX_W
P_COLS = OFF_SMALL + LANES
SM_LI, SM_LF, SM_FX = 0, ML_H, 2 * ML_H
D_FF_PAD = 5632
FF_TK = 1408


def _cparams(*sem):
    return pltpu.CompilerParams(dimension_semantics=sem, vmem_limit_bytes=VMEM_LIMIT)


def _split3(x):
    hi = x.astype(bf16)
    r = x - hi.astype(f32)
    mid = r.astype(bf16)
    lo = (r - mid.astype(f32)).astype(bf16)
    return hi, mid, lo


def _dot(a, b):
    return jnp.dot(a, b, preferred_element_type=f32)


def _dot_nt(a, b):
    return lax.dot_general(a, b, (((1,), (1,)), ((), ())), preferred_element_type=f32)


def _dot_r(x, m):
    hi, mid, lo = _split3(x)
    return _dot(hi, m) + _dot(mid, m) + _dot(lo, m)


def _dot_l(m, x):
    hi, mid, lo = _split3(x)
    return _dot(m, hi) + _dot(m, mid) + _dot(m, lo)


def _sigmoid(x):
    return 1.0 / (1.0 + jnp.exp(-x))


def _log_sigmoid(x):
    return jnp.minimum(x, 0.0) - jnp.log(1.0 + jnp.exp(-jnp.abs(x)))


def _softplus(x):
    return jnp.maximum(x, 0.0) + jnp.log(1.0 + jnp.exp(-jnp.abs(x)))


def _iota(shape, dim):
    return lax.broadcasted_iota(jnp.int32, shape, dim)


def _group_vec(v, T, tm):
    B, D = v.shape
    if tm <= T:
        per = T // tm
        return v[:, None, :], (None, 1, D), lambda i, *_: (i // per, 0, 0)
    rows = jnp.repeat(v, T, axis=0)
    return rows.reshape(-1, tm, D), (None, tm, D), lambda i, *_: (i, 0, 0)


def _row_tile(n_rows, T, want):
    return want if T >= want else n_rows


def _ada_kernel(c_ref, w_ref, b_ref, o_ref):
    c = c_ref[...]
    a = (c * _sigmoid(c)).astype(bf16)
    o_ref[...] = _dot(a, w_ref[...]) + b_ref[...]


def ada_mod(c, w, b):
    M, K = c.shape
    N = w.shape[1]
    tn = 1024
    return pl.pallas_call(
        _ada_kernel, grid=(N // tn,),
        in_specs=[pl.BlockSpec((M, K), lambda j: (0, 0)),
                  pl.BlockSpec((K, tn), lambda j: (0, j)),
                  pl.BlockSpec((1, tn), lambda j: (0, j))],
        out_specs=pl.BlockSpec((M, tn), lambda j: (0, j)),
        out_shape=jax.ShapeDtypeStruct((M, N), f32),
        compiler_params=_cparams("parallel"), name="ada_mod")(c, w, b)


def _norm_mod_kernel(x_ref, g_ref, sc_ref, sh_ref, o_ref):
    x = x_ref[...]
    ms = jnp.mean(x * x, axis=-1, keepdims=True)
    y = x * lax.rsqrt(ms + NORM_EPS) * g_ref[...]
    o_ref[...] = (y * (1.0 + sc_ref[...]) + sh_ref[...]).astype(o_ref.dtype)


def norm_mod(x, g, scale, shift, T):
    N, D = x.shape
    tm = _row_tile(N, T, 512)
    sc, sc_blk, sc_map = _group_vec(scale, T, tm)
    sh, _, _ = _group_vec(shift, T, tm)
    return pl.pallas_call(
        _norm_mod_kernel, grid=(N // tm,),
        in_specs=[pl.BlockSpec((tm, D), lambda i: (i, 0)),
                  pl.BlockSpec((1, D), lambda i: (0, 0)),
                  pl.BlockSpec(sc_blk, sc_map), pl.BlockSpec(sc_blk, sc_map)],
        out_specs=pl.BlockSpec((tm, D), lambda i: (i, 0)),
        out_shape=jax.ShapeDtypeStruct((N, D), bf16),
        compiler_params=_cparams("parallel"), name="norm_mod")(x, g.reshape(1, D), sc, sh)


def _mm_kernel(a_ref, w_ref, o_ref):
    o_ref[...] = _dot(a_ref[...], w_ref[...]).astype(o_ref.dtype)


def matmul(a, w, tn, out_dtype=f32):
    M, K = a.shape
    N = w.shape[1]
    tm = min(M, 1024)
    return pl.pallas_call(
        _mm_kernel, grid=(M // tm, N // tn),
        in_specs=[pl.BlockSpec((tm, K), lambda i, j: (i, 0)),
                  pl.BlockSpec((K, tn), lambda i, j: (0, j))],
        out_specs=pl.BlockSpec((tm, tn), lambda i, j: (i, j)),
        out_shape=jax.ShapeDtypeStruct((M, N), out_dtype),
        compiler_params=_cparams("parallel", "arbitrary"), name="matmul")(a, w)


def _mm_norm_res_kernel(a_ref, w_ref, x_ref, g_ref, gate_ref, o_ref, acc_ref, *, nk):
    k = pl.program_id(1)
    part = _dot(a_ref[...].astype(bf16), w_ref[...])

    @pl.when(k == 0)
    def _():
        acc_ref[...] = part

    @pl.when(k > 0)
    def _():
        acc_ref[...] += part

    @pl.when(k == nk - 1)
    def _():
        f = acc_ref[...]
        ms = jnp.mean(f * f, axis=-1, keepdims=True)
        y = f * lax.rsqrt(ms + NORM_EPS) * g_ref[...]
        o_ref[...] = x_ref[...] + gate_ref[...] * y


def mm_norm_res(a, w, x, g, gate, T, tk):
    M, K = a.shape
    D = w.shape[1]
    tm = _row_tile(M, T, 512)
    nk = K // tk
    gt, gt_blk, gt_map = _group_vec(gate, T, tm)
    return pl.pallas_call(
        functools.partial(_mm_norm_res_kernel, nk=nk), grid=(M // tm, nk),
        in_specs=[pl.BlockSpec((tm, tk), lambda i, k: (i, k)),
                  pl.BlockSpec((tk, D), lambda i, k: (k, 0)),
                  pl.BlockSpec((tm, D), lambda i, k: (i, 0)),
                  pl.BlockSpec((1, D), lambda i, k: (0, 0)),
                  pl.BlockSpec(gt_blk, gt_map)],
        out_specs=pl.BlockSpec((tm, D), lambda i, k: (i, 0)),
        out_shape=jax.ShapeDtypeStruct((M, D), f32),
        scratch_shapes=[pltpu.VMEM((tm, D), f32)],
        compiler_params=_cparams("parallel", "arbitrary"), name="mm_norm_res")(
            a, w, x, g.reshape(1, D), gt)


def _merge_kernel(orw_ref, oml_ref, ofx_ref, wrw_ref, wml_ref, wfx_ref, g0_ref, g1_ref, g2_ref, o_ref):
    m = _sigmoid(g0_ref[...]) * _dot(orw_ref[...].astype(bf16), wrw_ref[...])
    m += _sigmoid(g1_ref[...]) * _dot(oml_ref[...].astype(bf16), wml_ref[...])
    m += _sigmoid(g2_ref[...]) * _dot(ofx_ref[...].astype(bf16), wfx_ref[...])
    o_ref[...] = m.astype(o_ref.dtype)


def merge(o_rw, o_ml, o_fx, w_rw, w_ml, w_fx, P):
    N = o_rw.shape[0]
    D = D_MODEL
    tm = min(N, 512)
    tn = 512
    nb = D // tn
    gspec = lambda b: pl.BlockSpec((tm, tn), lambda i, j: (i, OFF_GATE // tn + b * nb + j))
    return pl.pallas_call(
        _merge_kernel, grid=(N // tm, nb),
        in_specs=[pl.BlockSpec((tm, RW_W), lambda i, j: (i, 0)),
                  pl.BlockSpec((tm, ML_W), lambda i, j: (i, 0)),
                  pl.BlockSpec((tm, FX_W), lambda i, j: (i, 0)),
                  pl.BlockSpec((RW_W, tn), lambda i, j: (0, j)),
                  pl.BlockSpec((ML_W, tn), lambda i, j: (0, j)),
                  pl.BlockSpec((FX_W, tn), lambda i, j: (0, j)),
                  gspec(0), gspec(1), gspec(2)],
        out_specs=pl.BlockSpec((tm, tn), lambda i, j: (i, j)),
        out_shape=jax.ShapeDtypeStruct((N, D), bf16),
        compiler_params=_cparams("parallel", "arbitrary"), name="merge")(
            o_rw, o_ml, o_fx, w_rw, w_ml, w_fx, P, P, P)


def _ffn_act_kernel(a_ref, halo_ref, buf_ref, val_ref, cw_ref, cb_ref, o_ref, scr_ref, *, per, tm):
    first = (pl.program_id(0) % per) == 0
    scr_ref[0:SUBLANES, :] = jnp.where(first, buf_ref[...], halo_ref[...])
    scr_ref[SUBLANES:SUBLANES + tm, :] = a_ref[...]
    y = cb_ref[...]
    for j in range(FFN_CONV):
        off = SUBLANES - (FFN_CONV - 1) + j
        y = y + scr_ref[off:off + tm, :] * cw_ref[j:j + 1, :]
    c0 = math.sqrt(2.0 / math.pi)
    gelu = 0.5 * y * (1.0 + jnp.tanh(c0 * (y + 0.044715 * (y * y * y))))
    o_ref[...] = (gelu * val_ref[...]).astype(o_ref.dtype)


def ffn_act(av, buf8, conv_w, conv_b, T):
    N = av.shape[0]
    tm = min(T, 512)
    per = T // tm
    tn = FF_TK
    nj = D_FF_PAD // tn
    hb = tm // SUBLANES
    out_dtype = bf16 if tm % 16 == 0 else f32
    return pl.pallas_call(
        functools.partial(_ffn_act_kernel, per=per, tm=tm), grid=(N // tm, nj),
        in_specs=[pl.BlockSpec((tm, tn), lambda i, j: (i, j)),
                  pl.BlockSpec((SUBLANES, tn), lambda i, j: (jnp.maximum(i * hb - 1, 0), j)),
                  pl.BlockSpec((None, SUBLANES, tn), lambda i, j: (i // per, 0, j)),
                  pl.BlockSpec((tm, tn), lambda i, j: (i, nj + j)),
                  pl.BlockSpec((SUBLANES, tn), lambda i, j: (0, j)),
                  pl.BlockSpec((1, tn), lambda i, j: (0, j))],
        out_specs=pl.BlockSpec((tm, tn), lambda i, j: (i, j)),
        out_shape=jax.ShapeDtypeStruct((N, D_FF_PAD), out_dtype),
        scratch_shapes=[pltpu.VMEM((tm + SUBLANES, tn), f32)],
        compiler_params=_cparams("parallel", "arbitrary"), name="ffn_act")(
            av, av, buf8, av, conv_w, conv_b)


def _head_sum(x, ones_blk):
    return _dot_r(x, ones_blk)


def _rwkv_prep_kernel(p_ref, halo_ref, sp_ref, mu_ref, w0_ref, a0_ref, kk_ref, ka_ref, rk_ref,
                      wup_ref, aup_ref, gup_ref, ones_ref,
                      nkk_o, wr_o, w_o, b_o, k_o, v_o, br_o, kr_o, rkr_o, g_o, scr_ref, *, per, tm):
    first = (pl.program_id(0) % per) == 0
    scr_ref[0:SUBLANES, :] = jnp.where(first, sp_ref[...], halo_ref[...])
    p = p_ref[...]
    scr_ref[SUBLANES:SUBLANES + tm, :] = p
    prev = scr_ref[SUBLANES - 1:SUBLANES - 1 + tm, :]
    xs = p + (prev - p) * mu_ref[...]
    W = RW_W
    r, k, v = xs[:, 0:W], xs[:, W:2 * W], xs[:, 2 * W:3 * W]
    o = 3 * W
    dw = xs[:, o:o + RW_LORA_PAD]
    da = xs[:, o + RW_LORA_PAD:o + 2 * RW_LORA_PAD]
    dg = xs[:, o + 2 * RW_LORA_PAD:o + 2 * RW_LORA_PAD + R_G]
    w_raw = -_softplus(-(w0_ref[...] + _dot(jnp.tanh(dw).astype(bf16), wup_ref[...]))) - RW_DECAY_CLAMP
    w = jnp.exp(-jnp.exp(w_raw))
    a = _sigmoid(a0_ref[...] + _dot(da.astype(bf16), aup_ref[...]))
    g = _dot(_sigmoid(dg).astype(bf16), gup_ref[...])
    ones_blk = ones_ref[...]
    kk = k * kk_ref[...]
    nrm = jnp.sqrt(_head_sum(kk * kk, ones_blk))
    kk = kk / jnp.maximum(nrm, 1e-12)
    k2 = k * (1.0 + (a - 1.0) * ka_ref[...])
    b = kk * a
    nkk_o[...] = -kk
    wr_o[...] = w * r
    w_o[...] = w
    b_o[...] = b
    k_o[...] = k2
    v_o[...] = v
    br_o[...] = _head_sum(b * r, ones_blk)
    kr_o[...] = _head_sum(k2 * r, ones_blk)
    rkr_o[...] = _head_sum(r * k2 * rk_ref[...], ones_blk)
    g_o[...] = g


def rwkv_prep(P, col_blk, sp8, mu, w0, a0, k_k, k_a, r_k, w_up, a_up, g_up, ones_blk, T):
    N = P.shape[0]
    tm = min(T, 256)
    per = T // tm
    hb = tm // SUBLANES
    C = RW_PCOLS
    vec = lambda n: pl.BlockSpec((1, n), lambda i: (0, 0))
    full = lambda a: pl.BlockSpec(a.shape, lambda i: (0, 0))
    out = jax.ShapeDtypeStruct((N, RW_W), f32)
    ospec = pl.BlockSpec((tm, RW_W), lambda i: (i, 0))
    return pl.pallas_call(
        functools.partial(_rwkv_prep_kernel, per=per, tm=tm), grid=(N // tm,),
        in_specs=[pl.BlockSpec((tm, C), lambda i: (i, col_blk)),
                  pl.BlockSpec((SUBLANES, C), lambda i: (jnp.maximum(i * hb - 1, 0), col_blk)),
                  pl.BlockSpec((None, SUBLANES, C), lambda i: (i // per, 0, 0)),
                  vec(C), vec(RW_W), vec(RW_W), vec(RW_W), vec(RW_W), vec(RW_W),
                  full(w_up), full(a_up), full(g_up), full(ones_blk)],
        out_specs=[ospec] * 10, out_shape=[out] * 10,
        scratch_shapes=[pltpu.VMEM((tm + SUBLANES, C), f32)],
        compiler_params=_cparams("parallel"), name="rwkv_prep")(
            P, P, sp8, mu, w0, a0, k_k, k_a, r_k, w_up, a_up, g_up, ones_blk)


N_PAIR = RW_H // 2


def _rwkv_scan_kernel(nkk_ref, wr_ref, w_ref, b_ref, k_ref, v_ref, s0_ref,
                      z_ref, sa_ref, so_ref, S_ref, vT_ref, zacc_ref, saacc_ref, *, n_sub, n_steps):
    c = pl.program_id(1)

    @pl.when(c == 0)
    def _():
        S_ref[...] = s0_ref[...]

    lo = _iota((1, LANES), 1) < RW_DH
    lane = _iota((1, LANES), 1)
    for sub in range(n_sub):
        r0 = sub * CHUNK
        for j in range(N_PAIR):
            vT_ref[j] = v_ref[r0:r0 + CHUNK, j * LANES:(j + 1) * LANES].T
        zacc_ref[...] = jnp.zeros_like(zacc_ref)
        saacc_ref[...] = jnp.zeros_like(saacc_ref)

        def group(g, carry):
            base = pl.multiple_of(r0 + g * SUBLANES, SUBLANES)
            for j in range(N_PAIR):
                cs = slice(j * LANES, (j + 1) * LANES)
                tiles = [ref[pl.ds(base, SUBLANES), cs] for ref in (nkk_ref, wr_ref, w_ref, b_ref, k_ref)]
                for i in range(SUBLANES):
                    tmask = lane == g * SUBLANES + i
                    nkk_r, wr_r, w_r, b_r, k_r = [tl[i:i + 1, :] for tl in tiles]
                    S = S_ref[j]
                    t1 = S * nkk_r
                    a1 = jnp.sum(jnp.where(lo, t1, 0.0), axis=-1, keepdims=True)
                    b1 = jnp.sum(jnp.where(lo, 0.0, t1), axis=-1, keepdims=True)
                    t2 = S * wr_r
                    a2 = jnp.sum(jnp.where(lo, t2, 0.0), axis=-1, keepdims=True)
                    b2 = jnp.sum(jnp.where(lo, 0.0, t2), axis=-1, keepdims=True)
                    va = jnp.sum(jnp.where(tmask, vT_ref[j, 0:RW_DH, :], 0.0), axis=-1, keepdims=True)
                    vb = jnp.sum(jnp.where(tmask, vT_ref[j, RW_DH:2 * RW_DH, :], 0.0), axis=-1, keepdims=True)
                    sa = jnp.where(lo, a1, b1)
                    vp = jnp.where(lo, va, vb)
                    S_ref[j] = S * w_r + sa * b_r + vp * k_r
                    zacc_ref[j, 0:RW_DH, :] = jnp.where(tmask, a2, zacc_ref[j, 0:RW_DH, :])
                    zacc_ref[j, RW_DH:2 * RW_DH, :] = jnp.where(tmask, b2, zacc_ref[j, RW_DH:2 * RW_DH, :])
                    saacc_ref[j, 0:RW_DH, :] = jnp.where(tmask, a1, saacc_ref[j, 0:RW_DH, :])
                    saacc_ref[j, RW_DH:2 * RW_DH, :] = jnp.where(tmask, b1, saacc_ref[j, RW_DH:2 * RW_DH, :])
            return carry

        lax.fori_loop(0, n_steps // SUBLANES, group, 0)
        for j in range(N_PAIR):
            z_ref[r0:r0 + CHUNK, j * LANES:(j + 1) * LANES] = zacc_ref[j].T
            sa_ref[r0:r0 + CHUNK, j * LANES:(j + 1) * LANES] = saacc_ref[j].T

    so_ref[...] = S_ref[...]


def rwkv_scan(nkk, wr, w, b, k, v, s0, n_steps):
    B, Tp, W = nkk.shape
    tc = min(Tp, 4 * CHUNK)
    n_sub = tc // CHUNK
    rows = pl.BlockSpec((None, tc, W), lambda bi, c: (bi, c, 0))
    st = pl.BlockSpec((None, N_PAIR, RW_DH, LANES), lambda bi, c: (bi, 0, 0, 0))
    tile = pltpu.VMEM((N_PAIR, LANES, LANES), f32)
    return pl.pallas_call(
        functools.partial(_rwkv_scan_kernel, n_sub=n_sub, n_steps=min(n_steps, CHUNK)),
        grid=(B, Tp // tc),
        in_specs=[rows] * 6 + [st],
        out_specs=[rows, rows, st],
        out_shape=[jax.ShapeDtypeStruct((B, Tp, W), f32)] * 2 + [jax.ShapeDtypeStruct(s0.shape, f32)],
        scratch_shapes=[pltpu.VMEM((N_PAIR, RW_DH, LANES), f32), tile, tile, tile],
        compiler_params=_cparams("parallel", "arbitrary"), name="rwkv_scan")(nkk, wr, w, b, k, v, s0)


def _rwkv_post_kernel(z_ref, sa_ref, v_ref, br_ref, kr_ref, rkr_ref, g_ref, lnw_ref, lnb_ref, ones_ref, o_ref):
    ones_blk = ones_ref[...]
    v = v_ref[...]
    y = z_ref[...] + sa_ref[...] * br_ref[...] + v * kr_ref[...]
    mu = _head_sum(y, ones_blk) * (1.0 / RW_DH)
    yc = y - mu
    var = _head_sum(yc * yc, ones_blk) * (1.0 / RW_DH)
    yn = yc * lax.rsqrt(var + RW_LN_EPS) * lnw_ref[...] + lnb_ref[...]
    o_ref[...] = ((yn + rkr_ref[...] * v) * g_ref[...]).astype(o_ref.dtype)


def rwkv_post(z, sa, v, br, kr, rkr, g, ln_w, ln_b, ones_blk):
    N, W = z.shape
    tm = min(N, 512)
    rows = pl.BlockSpec((tm, W), lambda i: (i, 0))
    vec = pl.BlockSpec((1, W), lambda i: (0, 0))
    return pl.pallas_call(
        _rwkv_post_kernel, grid=(N // tm,),
        in_specs=[rows] * 7 + [vec, vec, pl.BlockSpec(ones_blk.shape, lambda i: (0, 0))],
        out_specs=rows, out_shape=jax.ShapeDtypeStruct((N, W), bf16),
        compiler_params=_cparams("parallel"), name="rwkv_post")(z, sa, v, br, kr, rkr, g, ln_w, ln_b, ones_blk)


def _mlstm_kernel(main_ref, halo_ref, buf_ref, sm_ref, bias_ref, cw_ref, cb_ref, gn_ref, tri_ref,
                  c0_ref, n0_ref, m0_ref,
                  o_ref, lf_o, F_o, FT_o, c_o, n_o, m_o,
                  scr_ref, ct_ref, n_ref, m_ref, carry_ref, *, t_real):
    c = pl.program_id(1)
    L = CHUNK

    @pl.when(c == 0)
    def _():
        ct_ref[...] = c0_ref[...]
        n_ref[...] = n0_ref[...]
        m_ref[...] = m0_ref[...]
        carry_ref[...] = jnp.zeros_like(carry_ref)

    main = main_ref[...]
    scr_ref[0:SUBLANES, :] = jnp.where(c == 0, buf_ref[...], halo_ref[...])
    scr_ref[SUBLANES:SUBLANES + L, :] = main[:, 0:ML_QK]
    qk = cb_ref[...]
    for j in range(ML_CONV):
        off = SUBLANES - (ML_CONV - 1) + j
        qk = qk + scr_ref[off:off + L, :] * cw_ref[j:j + 1, :]
    qk = qk * _sigmoid(qk)
    q_all = qk[:, 0:ML_W]
    k_all = qk[:, ML_W:ML_QK] * (ML_DH ** -0.5)
    v_all = main[:, ML_QK:ML_QK + ML_W]
    og_all = main[:, ML_QK + ML_W:ML_QK + 2 * ML_W]

    valid = (c * L + _iota((L, 1), 0)) < t_real
    pre = sm_ref[...] + bias_ref[...]
    li_all = jnp.where(valid, pre, NEG)
    lf_all = jnp.where(valid, _log_sigmoid(pre), 0.0)
    cum = _dot_l(tri_ref[...], lf_all)
    F = cum + carry_ref[...]
    carry_ref[...] = F[L - 1:L, :]
    lf_o[...] = lf_all
    F_o[...] = F
    FT_o[...] = F.T
    liT = li_all.T
    cumT = cum.T

    row = _iota((L, L), 0)
    col = _iota((L, L), 1)
    causal = col <= row
    for h in range(ML_H):
        hs = slice(h * ML_DH, (h + 1) * ML_DH)
        b_col = cum[:, SM_LF + h:SM_LF + h + 1]
        b_row = cumT[SM_LF + h:SM_LF + h + 1, :]
        li_col = li_all[:, SM_LI + h:SM_LI + h + 1]
        li_row = liT[SM_LI + h:SM_LI + h + 1, :]
        m_prev = m_ref[h][:, 0:1]
        d = jnp.where(causal, b_col - b_row + li_row, NEG)
        inter = b_col + m_prev
        m_t = jnp.maximum(inter, jnp.max(d, axis=-1, keepdims=True))
        w_intra = jnp.exp(d - m_t)
        w_state = jnp.exp(inter - m_t)
        q = q_all[:, hs]
        k = k_all[:, hs]
        v = v_all[:, hs]
        qb, kb = q.astype(bf16), k.astype(bf16)
        s = _dot_nt(qb, kb) * w_intra
        ct = ct_ref[h]
        num = w_state * _dot(qb, ct.astype(bf16)) + _dot(s.astype(bf16), v.astype(bf16))
        n_row = n_ref[h]
        den = w_state * jnp.sum(q * n_row, axis=-1, keepdims=True) + jnp.sum(s, axis=-1, keepdims=True)
        hh = num / jnp.maximum(jnp.abs(den), jnp.exp(-m_t))
        g_end = b_col[L - 1:L, :]
        lw_s = g_end - b_col + li_col
        m_new = jnp.maximum(g_end + m_prev, jnp.max(lw_s, axis=0, keepdims=True))
        w_s = jnp.exp(lw_s - m_new)
        decay = jnp.exp(g_end + m_prev - m_new)
        ct_ref[h] = decay * ct + _dot(kb.T, (w_s * v).astype(bf16))
        n_ref[h] = decay * n_row + jnp.sum(w_s * k, axis=0, keepdims=True)
        m_ref[h] = jnp.broadcast_to(m_new, (1, LANES))
        mu = jnp.mean(hh, axis=-1, keepdims=True)
        hc = hh - mu
        var = jnp.mean(hc * hc, axis=-1, keepdims=True)
        hn = hc * lax.rsqrt(var + ML_GN_EPS) * gn_ref[:, hs]
        o_ref[:, hs] = (_sigmoid(og_all[:, hs]) * hn).astype(o_ref.dtype)

    c_o[...] = ct_ref[...]
    n_o[...] = n_ref[...]
    m_o[...] = m_ref[...]


def mlstm_fox_prep(main_arr, main_blk, small_arr, small_blk, buf8, bias_row, conv_w, conv_b, gn_w, tri,
                   c0t, n0, m0, B, Tp, t_real):
    L = CHUNK
    nc = Tp // L
    hb = L // SUBLANES
    st_c = pl.BlockSpec((None, ML_H, ML_DH, ML_DH), lambda b, c: (b, 0, 0, 0))
    st_n = pl.BlockSpec((None, ML_H, 1, ML_DH), lambda b, c: (b, 0, 0, 0))
    rows = lambda w: pl.BlockSpec((L, w), lambda b, c: (b * nc + c, 0))
    vec = lambda n: pl.BlockSpec((1, n), lambda b, c: (0, 0))
    N = B * Tp
    return pl.pallas_call(
        functools.partial(_mlstm_kernel, t_real=t_real), grid=(B, nc),
        in_specs=[pl.BlockSpec((L, ML_PCOLS), lambda b, c: (b * nc + c, main_blk)),
                  pl.BlockSpec((SUBLANES, ML_QK), lambda b, c: (jnp.maximum((b * nc + c) * hb - 1, 0), 2 * main_blk)),
                  pl.BlockSpec((None, SUBLANES, ML_QK), lambda b, c: (b, 0, 0)),
                  pl.BlockSpec((L, LANES), lambda b, c: (b * nc + c, small_blk)),
                  vec(LANES),
                  pl.BlockSpec((SUBLANES, ML_QK), lambda b, c: (0, 0)),
                  vec(ML_QK), vec(ML_W),
                  pl.BlockSpec((L, L), lambda b, c: (0, 0)),
                  st_c, st_n, st_n],
        out_specs=[rows(ML_W), rows(LANES), rows(LANES),
                   pl.BlockSpec((None, LANES, L), lambda b, c: (b, 0, c)),
                   st_c, st_n, st_n],
        out_shape=[jax.ShapeDtypeStruct((N, ML_W), bf16),
                   jax.ShapeDtypeStruct((N, LANES), f32),
                   jax.ShapeDtypeStruct((N, LANES), f32),
                   jax.ShapeDtypeStruct((B, LANES, Tp), f32),
                   jax.ShapeDtypeStruct((B, ML_H, ML_DH, ML_DH), f32),
                   jax.ShapeDtypeStruct((B, ML_H, 1, ML_DH), f32),
                   jax.ShapeDtypeStruct((B, ML_H, 1, ML_DH), f32)],
        scratch_shapes=[pltpu.VMEM((L + SUBLANES, ML_QK), f32),
                        pltpu.VMEM((ML_H, ML_DH, ML_DH), f32),
                        pltpu.VMEM((ML_H, 1, ML_DH), f32),
                        pltpu.VMEM((ML_H, 1, LANES), f32),
                        pltpu.VMEM((1, LANES), f32)],
        compiler_params=_cparams("parallel", "arbitrary"), name="mlstm")(
            main_arr, main_arr, buf8, small_arr, bias_row, conv_w, conv_b, gn_w, tri, c0t, n0, m0)


def _fox_attn_kernel(q_ref, k_ref, v_ref, fc_ref, ft_ref, o_ref, *, tq, tk):
    h = pl.program_id(1)
    qi = pl.program_id(2)
    scale = FX_DH ** -0.5
    qb = q_ref[...].astype(bf16)
    fq = jnp.sum(jnp.where(_iota((1, LANES), 1) == SM_FX + h, fc_ref[...], 0.0), axis=-1, keepdims=True)
    row = qi * tq + _iota((tq, tk), 0)
    col0 = _iota((tq, tk), 1)

    def body(j, carry):
        m, l, acc = carry
        k0 = pl.multiple_of(j * tk, tk)
        kb = k_ref[pl.ds(k0, tk), :].astype(bf16)
        vb = v_ref[pl.ds(k0, tk), :].astype(bf16)
        fk = ft_ref[pl.ds(h, 1), pl.ds(k0, tk)]
        s = _dot_nt(qb, kb) * scale + fq - fk
        s = jnp.where(col0 + k0 <= row, s, NEG)
        m_new = jnp.maximum(m, jnp.max(s, axis=-1, keepdims=True))
        alpha = jnp.exp(m - m_new)
        p = jnp.exp(s - m_new)
        l = alpha * l + jnp.sum(p, axis=-1, keepdims=True)
        acc = alpha * acc + _dot(p.astype(bf16), vb)
        return m_new, l, acc

    init = (jnp.full((tq, 1), NEG, f32), jnp.zeros((tq, 1), f32), jnp.zeros((tq, FX_DH), f32))
    n_kv = (qi * tq + tq + tk - 1) // tk
    m, l, acc = lax.fori_loop(0, n_kv, body, init)
    o_ref[...] = (acc / l).astype(o_ref.dtype)


def fox_attn(P, F, FT, B, T):
    tq = 256
    tk = 256
    nq = T // tq
    qb, kb, vb = OFF_FXQ // FX_DH, OFF_FXK // FX_DH, OFF_FXV // FX_DH
    return pl.pallas_call(
        functools.partial(_fox_attn_kernel, tq=tq, tk=tk), grid=(B, FX_H, nq),
        in_specs=[pl.BlockSpec((tq, FX_DH), lambda b, h, i: (b * nq + i, qb + h)),
                  pl.BlockSpec((T, FX_DH), lambda b, h, i: (b, kb + h)),
                  pl.BlockSpec((T, FX_DH), lambda b, h, i: (b, vb + h)),
                  pl.BlockSpec((tq, LANES), lambda b, h, i: (b * nq + i, 0)),
                  pl.BlockSpec((None, SUBLANES, T), lambda b, h, i: (b, SM_FX // SUBLANES, 0))],
        out_specs=pl.BlockSpec((tq, FX_DH), lambda b, h, i: (b * nq + i, h)),
        out_shape=jax.ShapeDtypeStruct((B * T, FX_W), bf16),
        compiler_params=_cparams("parallel", "parallel", "arbitrary"), name="fox_attn")(P, P, P, F, FT)


QPAD = LANES // FX_H


def _diag_blocks(pv):
    return jnp.concatenate([pv[h * QPAD:(h + 1) * QPAD, h * FX_DH:(h + 1) * FX_DH] for h in range(FX_H)], axis=0)


def _fox_decode_kernel(pt_ref, q_ref, kc_ref, vc_ref, lf_ref, kn_ref, vn_ref, fq_ref, fkn_ref, triu_ref,
                       o_ref, m_ref, l_ref, acc_ref, r_ref, *, n_pages, t_new):
    j = pl.program_id(1)
    scale = FX_DH ** -0.5

    @pl.when(j == 0)
    def _():
        m_ref[...] = jnp.full_like(m_ref, NEG)
        l_ref[...] = jnp.zeros_like(l_ref)
        acc_ref[...] = jnp.zeros_like(acc_ref)
        r_ref[...] = jnp.zeros_like(r_ref)

    qb = q_ref[...]

    def update(s, vb):
        m = m_ref[...]
        m_new = jnp.maximum(m, jnp.max(s, axis=-1, keepdims=True))
        alpha = jnp.exp(m - m_new)
        p = jnp.exp(s - m_new)
        l_ref[...] = alpha * l_ref[...] + jnp.sum(p, axis=-1, keepdims=True)
        acc_ref[...] = alpha * acc_ref[...] + _diag_blocks(_dot(p.astype(bf16), vb))
        m_ref[...] = m_new

    @pl.when(j < n_pages)
    def _():
        lft = lf_ref[...]
        lfx = jnp.concatenate([jnp.broadcast_to(lft[h:h + 1, :], (QPAD, PAGE_SIZE)) for h in range(FX_H)], axis=0)
        within = _dot_r(lfx, triu_ref[...])
        s = _dot_nt(qb, kc_ref[...].astype(bf16)) * scale + fq_ref[...] + r_ref[...] + within
        update(s, vc_ref[...].astype(bf16))
        r_ref[...] = r_ref[...] + jnp.sum(lfx, axis=-1, keepdims=True)

    @pl.when(j == n_pages)
    def _():
        s = _dot_nt(qb, kn_ref[...].astype(bf16)) * scale + fq_ref[...] - fkn_ref[...]
        key = _iota((LANES, LANES), 1)
        tok = _iota((LANES, LANES), 0) % QPAD
        s = jnp.where((key <= tok) & (key < t_new), s, NEG)
        update(s, vn_ref[...].astype(bf16))
        o_ref[...] = acc_ref[...] / l_ref[...]


def fox_decode(page_table, qbd, cache_k, cache_v, cache_lft, k_new, v_new, fq, fkn, triu, layer, t_new):
    B, n_pages = page_table.shape
    page = lambda b, j, pt: pt[b, n_pages - 1 - jnp.minimum(j, n_pages - 1)]
    kv_spec = pl.BlockSpec((None, None, PAGE_SIZE, FX_W), lambda b, j, pt: (layer, page(b, j, pt), 0, 0))
    per_b = lambda shape: pl.BlockSpec((None,) + shape, lambda b, j, pt: (b, 0, 0))
    grid_spec = pltpu.PrefetchScalarGridSpec(
        num_scalar_prefetch=1, grid=(B, n_pages + 1),
        in_specs=[per_b((LANES, FX_W)), kv_spec, kv_spec,
                  pl.BlockSpec((None, None, FX_H, PAGE_SIZE), lambda b, j, pt: (layer, page(b, j, pt), 0, 0)),
                  per_b((LANES, FX_W)), per_b((LANES, FX_W)), per_b((LANES, LANES)), per_b((LANES, LANES)),
                  pl.BlockSpec((LANES, LANES), lambda b, j, pt: (0, 0))],
        out_specs=per_b((LANES, FX_DH)),
        scratch_shapes=[pltpu.VMEM((LANES, 1), f32), pltpu.VMEM((LANES, 1), f32),
                        pltpu.VMEM((LANES, FX_DH), f32), pltpu.VMEM((LANES, 1), f32)])
    return pl.pallas_call(
        functools.partial(_fox_decode_kernel, n_pages=n_pages, t_new=t_new), grid_spec=grid_spec,
        out_shape=jax.ShapeDtypeStruct((B, LANES, FX_DH), f32),
        compiler_params=_cparams("parallel", "arbitrary"), name="fox_decode")(
            page_table, qbd, cache_k, cache_v, cache_lft, k_new, v_new, fq, fkn, triu)


def _pad_cols(w, n):
    return jnp.pad(w, ((0, 0), (0, n - w.shape[1])))


def _rw_cols(x):
    W = RW_W
    pad = [(0, 0)] * (x.ndim - 1) + [(0, RW_LORA_PAD - R_W)]
    return jnp.concatenate([x[..., 0:3 * W], jnp.pad(x[..., 3 * W:3 * W + R_W], pad),
                            jnp.pad(x[..., 3 * W + R_W:3 * W + R_W + R_A], pad),
                            x[..., 3 * W + R_W + R_A:]], axis=-1)


def _rw_cols_inv(x):
    W = RW_W
    o = 3 * W
    return jnp.concatenate([x[..., 0:o], x[..., o:o + R_W], x[..., o + RW_LORA_PAD:o + RW_LORA_PAD + R_A],
                            x[..., o + 2 * RW_LORA_PAD:]], axis=-1)


def _w_in_cat(w):
    o_ml = RW_COLS
    o_fx = o_ml + ML_COLS
    o_gt = o_fx + FX_COLS
    ml = w[:, o_ml:o_fx]
    fx = w[:, o_fx:o_gt]
    ml_main = jnp.concatenate([ml[:, 0:ML_QK + ML_W], ml[:, ML_QK + ML_W + 2 * ML_H:]], axis=1)
    small = _pad_cols(jnp.concatenate([ml[:, ML_QK + ML_W:ML_QK + ML_W + 2 * ML_H], fx[:, 3 * FX_W:]], axis=1), LANES)
    cat = jnp.concatenate([w[:, o_gt:], _rw_cols(w[:, 0:RW_COLS]), ml_main, fx[:, 0:3 * FX_W], small], axis=1)
    return cat.astype(bf16)


def _pad_rows(w, n):
    return jnp.pad(w, ((0, n - w.shape[0]), (0, 0)))


def _prev_rows8(buf, width):
    B, r, C = buf.shape
    return jnp.pad(buf, ((0, 0), (SUBLANES - r, 0), (0, width - C)))


def _layer(x, mod, st, past, W, page_table, l, B, T):
    N = B * T
    shift1, scale1, gate1, shift2, scale2, gate2 = jnp.split(mod, N_MOD, axis=-1)
    u = norm_mod(x, W['norm_pre_mix'], scale1, shift1, T)
    P = matmul(u, W['w_in'], tn=896 if N >= 1024 else 1920)

    padded = T % CHUNK != 0
    Tp = T if not padded else CHUNK

    def pad_t(a):
        if not padded:
            return a
        return jnp.pad(a.reshape(B, T, -1), ((0, 0), (0, Tp - T), (0, 0))).reshape(B * Tp, -1)

    def unpad_t(a):
        if not padded:
            return a
        return a.reshape(B, Tp, -1)[:, :T].reshape(N, -1)

    sp8 = _prev_rows8(_rw_cols(st['rw_shift'])[:, None, :], RW_PCOLS)
    prep = rwkv_prep(P, OFF_RW // RW_PCOLS, sp8, W['rw_mu'], W['rw_w0'], W['rw_a0'], W['rw_k_k'], W['rw_k_a'],
                     W['rw_r_k'], W['rw_w_up'], W['rw_a_up'], W['rw_g_up'], W['ones_blk'], T)
    nkk, wr, w_, b_, k2, v_rw, br, kr, rkr, g_rw = prep
    s0 = st['rw_wkv'].reshape(B, N_PAIR, 2, RW_DH, RW_DH).transpose(0, 1, 3, 2, 4).reshape(B, N_PAIR, RW_DH, LANES)
    seq = [pad_t(a).reshape(B, Tp, RW_W) for a in (nkk, wr, w_, b_, k2, v_rw)]
    z, sa, s_out = rwkv_scan(*seq, s0, T)
    z, sa = unpad_t(z.reshape(B * Tp, RW_W)), unpad_t(sa.reshape(B * Tp, RW_W))
    o_rw = rwkv_post(z, sa, v_rw, br, kr, rkr, g_rw, W['rw_ln_w'], W['rw_ln_b'], W['ones_blk'])
    rw_wkv = s_out.reshape(B, N_PAIR, RW_DH, 2, RW_DH).transpose(0, 1, 3, 2, 4).reshape(B, RW_H, RW_DH, RW_DH)
    rw_shift = _rw_cols_inv(P[:, OFF_RW:OFF_RW + RW_PCOLS].reshape(B, T, RW_PCOLS)[:, -1])

    buf8 = _prev_rows8(st['ml_conv'], ML_QK)
    c0t = jnp.swapaxes(st['ml_c'], -1, -2)
    n0 = st['ml_n'][:, :, None, :]
    m0 = jnp.broadcast_to(st['ml_m'][:, :, None, None], (B, ML_H, 1, LANES))
    if padded:
        main_arr, main_blk = pad_t(P[:, OFF_ML:OFF_ML + ML_PCOLS]), 0
        small_arr, small_blk = pad_t(P[:, OFF_SMALL:]), 0
    else:
        main_arr, main_blk, small_arr, small_blk = P, OFF_ML // ML_PCOLS, P, OFF_SMALL // LANES
    o_ml, lf_all, F, FT, ct, n_new, m_new = mlstm_fox_prep(
        main_arr, main_blk, small_arr, small_blk, buf8, W['small_bias'], W['ml_conv_w'], W['ml_conv_b'],
        W['ml_gn_w'], W['tri'], c0t, n0, m0, B, Tp, T)
    o_ml = unpad_t(o_ml)
    ml_c = jnp.swapaxes(ct, -1, -2)
    ml_n = n_new[:, :, 0, :]
    ml_m = m_new[:, :, 0, 0]
    ml_conv = P[:, OFF_ML:OFF_ML + ML_QK].reshape(B, T, ML_QK)[:, T - (ML_CONV - 1):]
    fox_logf = unpad_t(lf_all)[:, SM_FX:SM_FX + FX_H].reshape(B, T, FX_H)
    fox_k = P[:, OFF_FXK:OFF_FXK + FX_W].reshape(B, T, FX_H, FX_DH)
    fox_v = P[:, OFF_FXV:OFF_FXV + FX_W].reshape(B, T, FX_H, FX_DH)

    if past is None:
        o_fx = fox_attn(P, F, FT, B, T)
    else:
        cache_k, cache_v, cache_lft = past
        Fn = F.reshape(B, Tp, LANES)[:, :T, SM_FX:SM_FX + FX_H]
        q = P[:, OFF_FXQ:OFF_FXQ + FX_W].reshape(B, T, FX_H, FX_DH)
        eye = jnp.eye(FX_H, dtype=f32)
        qbd = jnp.einsum('bthd,hg->bhtgd', q, eye)
        qbd = jnp.pad(qbd, ((0, 0), (0, 0), (0, QPAD - T), (0, 0), (0, 0))).reshape(B, LANES, FX_W).astype(bf16)
        padk = lambda a: jnp.pad(a.reshape(B, T, FX_W), ((0, 0), (0, LANES - T), (0, 0)))
        fq = jnp.pad(Fn.transpose(0, 2, 1), ((0, 0), (0, 0), (0, QPAD - T))).reshape(B, LANES, 1)
        fq = jnp.broadcast_to(fq, (B, LANES, LANES))
        fkn = jnp.pad(Fn.transpose(0, 2, 1), ((0, 0), (0, 0), (0, LANES - T)))
        fkn = jnp.broadcast_to(fkn[:, :, None, :], (B, FX_H, QPAD, LANES)).reshape(B, LANES, LANES)
        o = fox_decode(page_table, qbd, cache_k, cache_v, cache_lft, padk(fox_k), padk(fox_v), fq, fkn,
                       W['triu'], l, T)
        o_fx = o.reshape(B, FX_H, QPAD, FX_DH)[:, :, :T].transpose(0, 2, 1, 3).reshape(N, FX_W)

    merged = merge(o_rw, o_ml, o_fx, W['w_br_rwkv'], W['w_br_mlstm'], W['w_br_fox'], P)
    x = mm_norm_res(merged, W['w_out'], x, W['norm_post_mix'], gate1, T, D_MODEL)

    zf = norm_mod(x, W['norm_pre_ffn'], scale2, shift2, T)
    av = matmul(zf, W['ffn_w_gv'], tn=1024)
    fbuf8 = _prev_rows8(st['ffn_conv'], D_FF_PAD)
    hmid = ffn_act(av, fbuf8, W['ffn_conv_w'], W['ffn_conv_b'], T)
    x = mm_norm_res(hmid, W['ffn_w_down'], x, W['norm_post_ffn'], gate2, T, FF_TK)
    ffn_conv = av[:, 0:D_FF].reshape(B, T, D_FF)[:, T - (FFN_CONV - 1):]

    new = dict(fox_k=fox_k, fox_v=fox_v, fox_logf=fox_logf, rw_shift=rw_shift, rw_wkv=rw_wkv,
               ml_conv=ml_conv, ml_c=ml_c, ml_n=ml_n, ml_m=ml_m, ffn_conv=ffn_conv)
    return x, new


STATE_NAMES = ("fox_k", "fox_v", "fox_logf", "rw_shift", "rw_wkv", "ml_conv", "ml_c", "ml_n", "ml_m", "ffn_conv")


def _layer_weights(Pm, l):
    row = lambda v: v.reshape(1, -1)
    W = {}
    for name in ('norm_pre_mix', 'norm_post_mix', 'norm_pre_ffn', 'norm_post_ffn'):
        W[name] = Pm[name][l]
    W['w_in'] = _w_in_cat(Pm['w_in'][l])
    W['rw_mu'] = row(_rw_cols(Pm['rw_mu'][l]))
    for name in ('rw_w0', 'rw_a0', 'rw_k_k', 'rw_k_a', 'rw_r_k', 'rw_ln_w', 'rw_ln_b'):
        W[name] = row(Pm[name][l])
    W['rw_w_up'] = _pad_rows(Pm['rw_w_up'][l], RW_LORA_PAD).astype(bf16)
    W['rw_a_up'] = _pad_rows(Pm['rw_a_up'][l], RW_LORA_PAD).astype(bf16)
    W['rw_g_up'] = Pm['rw_g_up'][l].astype(bf16)
    hid = jnp.arange(RW_W) // RW_DH
    W['ones_blk'] = (hid[:, None] == hid[None, :]).astype(bf16)
    W['small_bias'] = row(jnp.pad(jnp.concatenate([Pm['ml_b_i'][l], Pm['ml_b_f'][l], Pm['fx_b_f'][l]]),
                                  (0, LANES - 2 * ML_H - FX_H)))
    W['ml_conv_w'] = _pad_rows(Pm['ml_conv_w'][l], SUBLANES)
    W['ml_conv_b'] = row(Pm['ml_conv_b'][l])
    W['ml_gn_w'] = row(Pm['ml_gn_w'][l])
    idx = jnp.arange(CHUNK)
    W['tri'] = (idx[None, :] <= idx[:, None]).astype(bf16)
    W['triu'] = (idx[:, None] > idx[None, :]).astype(bf16)
    W['w_br_rwkv'] = Pm['w_br_rwkv'][l].astype(bf16)
    W['w_br_mlstm'] = Pm['w_br_mlstm'][l].astype(bf16)
    W['w_br_fox'] = Pm['w_br_fox'][l].astype(bf16)
    W['w_out'] = Pm['w_out'][l].astype(bf16)
    W['ffn_w_gv'] = jnp.concatenate([_pad_cols(Pm['ffn_w_gate'][l], D_FF_PAD),
                                     _pad_cols(Pm['ffn_w_val'][l], D_FF_PAD)], axis=1).astype(bf16)
    W['ffn_conv_w'] = _pad_rows(_pad_cols(Pm['ffn_conv_w'][l], D_FF_PAD), SUBLANES)
    W['ffn_conv_b'] = _pad_cols(row(Pm['ffn_conv_b'][l]), D_FF_PAD)
    W['ffn_w_down'] = _pad_rows(Pm['ffn_w_down'][l], D_FF_PAD).astype(bf16)
    return W


def kernel(x_prompt, x_sample, cache_fox_k, cache_fox_v, cache_fox_logf, state_rwkv_shift, state_rwkv_wkv,
           state_mlstm_conv, state_mlstm_c, state_mlstm_n, state_mlstm_m, state_ffn_conv, page_table,
           c_prompt, c_sample, w_ada, b_ada, norm_pre_mix, norm_post_mix, norm_pre_ffn, norm_post_ffn, w_in,
           rw_mu, rw_w0, rw_w_up, rw_a0, rw_a_up, rw_g_up, rw_k_k, rw_k_a, rw_r_k, rw_ln_w, rw_ln_b,
           ml_conv_w, ml_conv_b, ml_b_i, ml_b_f, ml_gn_w, fx_b_f, w_br_rwkv, w_br_mlstm, w_br_fox, w_out,
           ffn_w_gate, ffn_w_val, ffn_conv_w, ffn_conv_b, ffn_w_down):
    Pm = dict(norm_pre_mix=norm_pre_mix, norm_post_mix=norm_post_mix, norm_pre_ffn=norm_pre_ffn,
              norm_post_ffn=norm_post_ffn, w_in=w_in, rw_mu=rw_mu, rw_w0=rw_w0, rw_w_up=rw_w_up, rw_a0=rw_a0,
              rw_a_up=rw_a_up, rw_g_up=rw_g_up, rw_k_k=rw_k_k, rw_k_a=rw_k_a,
              rw_r_k=rw_r_k.reshape(DEPTH, RW_W), rw_ln_w=rw_ln_w, rw_ln_b=rw_ln_b, ml_conv_w=ml_conv_w,
              ml_conv_b=ml_conv_b, ml_b_i=ml_b_i, ml_b_f=ml_b_f, ml_gn_w=ml_gn_w, fx_b_f=fx_b_f,
              w_br_rwkv=w_br_rwkv, w_br_mlstm=w_br_mlstm, w_br_fox=w_br_fox, w_out=w_out, ffn_w_gate=ffn_w_gate,
              ffn_w_val=ffn_w_val, ffn_conv_w=ffn_conv_w, ffn_conv_b=ffn_conv_b, ffn_w_down=ffn_w_down)
    Bp, Tpr, D = x_prompt.shape
    Bs, Ts, _ = x_sample.shape
    n_pool = cache_fox_k.shape[1]
    cache_k = cache_fox_k.reshape(DEPTH, n_pool, PAGE_SIZE, FX_W)
    cache_v = cache_fox_v.reshape(DEPTH, n_pool, PAGE_SIZE, FX_W)
    cache_lft = jnp.swapaxes(cache_fox_logf, -1, -2)

    zeros = lambda *s: jnp.zeros(s, f32)
    xp = x_prompt.reshape(Bp * Tpr, D)
    xs = x_sample.reshape(Bs * Ts, D)
    c_all = jnp.pad(jnp.concatenate([c_prompt, c_sample], axis=0), ((0, 16 - Bp - Bs), (0, 0)))
    new_p = {n: [] for n in STATE_NAMES}
    new_s = {n: [] for n in STATE_NAMES}
    for l in range(DEPTH):
        W = _layer_weights(Pm, l)
        mod = ada_mod(c_all, w_ada[l].astype(bf16), b_ada[l].reshape(1, -1))
        st_p = dict(rw_shift=zeros(Bp, RW_COLS), rw_wkv=zeros(Bp, RW_H, RW_DH, RW_DH),
                    ml_conv=zeros(Bp, ML_CONV - 1, ML_QK), ml_c=zeros(Bp, ML_H, ML_DH, ML_DH),
                    ml_n=zeros(Bp, ML_H, ML_DH), ml_m=zeros(Bp, ML_H), ffn_conv=zeros(Bp, FFN_CONV - 1, D_FF))
        st_s = dict(rw_shift=state_rwkv_shift[l], rw_wkv=state_rwkv_wkv[l], ml_conv=state_mlstm_conv[l],
                    ml_c=state_mlstm_c[l], ml_n=state_mlstm_n[l], ml_m=state_mlstm_m[l],
                    ffn_conv=state_ffn_conv[l])
        xp, lp = _layer(xp, mod[:Bp], st_p, None, W, None, l, Bp, Tpr)
        xs, ls = _layer(xs, mod[Bp:Bp + Bs], st_s, (cache_k, cache_v, cache_lft), W, page_table, l, Bs, Ts)
        for n in STATE_NAMES:
            new_p[n].append(lp[n])
            new_s[n].append(ls[n])
    sp = {n: jnp.stack(v) for n, v in new_p.items()}
    ss = {n: jnp.stack(v) for n, v in new_s.items()}
    return (xp.reshape(Bp, Tpr, D), xs.reshape(Bs, Ts, D),
            sp['fox_k'], ss['fox_k'], sp['fox_v'], ss['fox_v'], sp['fox_logf'], ss['fox_logf'],
            sp['rw_shift'], ss['rw_shift'], sp['rw_wkv'], ss['rw_wkv'],
            sp['ml_conv'], ss['ml_conv'], sp['ml_c'], ss['ml_c'], sp['ml_n'], ss['ml_n'], sp['ml_m'], ss['ml_m'],
            sp['ffn_conv'], ss['ffn_conv'])
```

```python
import functools
import math

import jax
import jax.numpy as jnp
from jax import lax
from jax.experimental import pallas as pl
from jax.experimental.pallas import tpu as pltpu

f32 = jnp.float32
bf16 = jnp.bfloat16

D_MODEL = 2048
DEPTH = 2
PAGE_SIZE = 128
RW_W = D_MODEL // 4
RW_DH = 64
RW_H = RW_W // RW_DH
R_W = max(32, int(round(1.8 * D_MODEL ** 0.5 / 32)) * 32)
R_A = R_W
R_G = max(32, int(round(0.6 * D_MODEL ** 0.8 / 32)) * 32)
RW_COLS = 3 * RW_W + R_W + R_A + R_G
RW_LN_EPS = 64e-5
RW_DECAY_CLAMP = 0.5
ML_W = D_MODEL // 4
ML_H = 4
ML_DH = ML_W // ML_H
ML_QK = 2 * ML_W
ML_CONV = 4
ML_COLS = ML_QK + ML_W + 2 * ML_H + ML_W
ML_GN_EPS = 1e-5
FX_W = D_MODEL // 2
FX_DH = 128
FX_H = FX_W // FX_DH
FX_COLS = 3 * FX_W + FX_H
N_BRANCH = 3
GATE_COLS = N_BRANCH * D_MODEL
D_FF = ((8 * D_MODEL // 3 + 127) // 128) * 128
FFN_CONV = 3
N_MOD = 6
NORM_EPS = 1e-6

LANES = 128
SUBLANES = 8
VMEM_LIMIT = 56 * 1024 * 1024
CHUNK = 128
NEG = -1e30

RW_LORA_PAD = LANES
RW_PCOLS = 3 * RW_W + 2 * RW_LORA_PAD + R_G
OFF_GATE = 0
OFF_RW = OFF_GATE + GATE_COLS
OFF_ML = OFF_RW + RW_PCOLS
ML_PCOLS = ML_QK + 2 * ML_W
OFF_FXQ = OFF_ML + ML_PCOLS
OFF_FXK = OFF_FXQ + FX_W
OFF_FXV = OFF_FXK + FX_W
OFF_SMALL = OFF_FXV + FX_W
P_COLS = OFF_SMALL + LANES
SM_LI, SM_LF, SM_FX = 0, ML_H, 2 * ML_H
D_FF_PAD = 5632
FF_TK = 1408


def _cparams(*sem):
    return pltpu.CompilerParams(dimension_semantics=sem, vmem_limit_bytes=VMEM_LIMIT)


def _split3(x):
    hi = x.astype(bf16)
    r = x - hi.astype(f32)
    mid = r.astype(bf16)
    lo = (r - mid.astype(f32)).astype(bf16)
    return hi, mid, lo


def _dot(a, b):
    return jnp.dot(a, b, preferred_element_type=f32)


def _dot_nt(a, b):
    return lax.dot_general(a, b, (((1,), (1,)), ((), ())), preferred_element_type=f32)


def _dot_r(x, m):
    hi, mid, lo = _split3(x)
    return _dot(hi, m) + _dot(mid, m) + _dot(lo, m)


def _dot_l(m, x):
    hi, mid, lo = _split3(x)
    return _dot(m, hi) + _dot(m, mid) + _dot(m, lo)


def _sigmoid(x):
    return 1.0 / (1.0 + jnp.exp(-x))


def _log_sigmoid(x):
    return jnp.minimum(x, 0.0) - jnp.log(1.0 + jnp.exp(-jnp.abs(x)))


def _softplus(x):
    return jnp.maximum(x, 0.0) + jnp.log(1.0 + jnp.exp(-jnp.abs(x)))


def _iota(shape, dim):
    return lax.broadcasted_iota(jnp.int32, shape, dim)


def _lane_tiles(x):
    return [x[:, i:i + LANES] for i in range(0, x.shape[-1], LANES)]


def _row_max(x):
    return jnp.max(functools.reduce(jnp.maximum, _lane_tiles(x)), axis=-1, keepdims=True)


def _row_sum(x):
    return jnp.sum(functools.reduce(jnp.add, _lane_tiles(x)), axis=-1, keepdims=True)


def _group_vec(v, T, tm):
    B, D = v.shape
    if tm <= T:
        per = T // tm
        return v[:, None, :], (None, 1, D), lambda i, *_: (i // per, 0, 0)
    rows = jnp.repeat(v, T, axis=0)
    return rows.reshape(-1, tm, D), (None, tm, D), lambda i, *_: (i, 0, 0)


def _row_tile(n_rows, T, want):
    return want if T >= want else n_rows


def _ada_kernel(c_ref, w_ref, b_ref, o_ref):
    c = c_ref[...]
    a = (c * _sigmoid(c)).astype(bf16)
    o_ref[...] = _dot(a, w_ref[...]) + b_ref[...]


def ada_mod(c, w, b):
    M, K = c.shape
    N = w.shape[1]
    tn = 1024
    return pl.pallas_call(
        _ada_kernel, grid=(N // tn,),
        in_specs=[pl.BlockSpec((M, K), lambda j: (0, 0)),
                  pl.BlockSpec((K, tn), lambda j: (0, j)),
                  pl.BlockSpec((1, tn), lambda j: (0, j))],
        out_specs=pl.BlockSpec((M, tn), lambda j: (0, j)),
        out_shape=jax.ShapeDtypeStruct((M, N), f32),
        compiler_params=_cparams("parallel"), name="ada_mod")(c, w, b)


def _norm_mod_kernel(x_ref, g_ref, sc_ref, sh_ref, o_ref):
    x = x_ref[...]
    ms = jnp.mean(x * x, axis=-1, keepdims=True)
    y = x * lax.rsqrt(ms + NORM_EPS) * g_ref[...]
    o_ref[...] = (y * (1.0 + sc_ref[...]) + sh_ref[...]).astype(o_ref.dtype)


def norm_mod(x, g, scale, shift, T):
    N, D = x.shape
    tm = _row_tile(N, T, 512)
    sc, sc_blk, sc_map = _group_vec(scale, T, tm)
    sh, _, _ = _group_vec(shift, T, tm)
    return pl.pallas_call(
        _norm_mod_kernel, grid=(N // tm,),
        in_specs=[pl.BlockSpec((tm, D), lambda i: (i, 0)),
                  pl.BlockSpec((1, D), lambda i: (0, 0)),
                  pl.BlockSpec(sc_blk, sc_map), pl.BlockSpec(sc_blk, sc_map)],
        out_specs=pl.BlockSpec((tm, D), lambda i: (i, 0)),
        out_shape=jax.ShapeDtypeStruct((N, D), bf16),
        compiler_params=_cparams("parallel"), name="norm_mod")(x, g.reshape(1, D), sc, sh)


def _mm_kernel(a_ref, w_ref, o_ref):
    o_ref[...] = _dot(a_ref[...], w_ref[...]).astype(o_ref.dtype)


def matmul(a, w, tn, out_dtype=f32):
    M, K = a.shape
    N = w.shape[1]
    tm = min(M, 1024)
    return pl.pallas_call(
        _mm_kernel, grid=(M // tm, N // tn),
        in_specs=[pl.BlockSpec((tm, K), lambda i, j: (i, 0)),
                  pl.BlockSpec((K, tn), lambda i, j: (0, j))],
        out_specs=pl.BlockSpec((tm, tn), lambda i, j: (i, j)),
        out_shape=jax.ShapeDtypeStruct((M, N), out_dtype),
        compiler_params=_cparams("parallel", "arbitrary"), name="matmul")(a, w)


def _mm_norm_res_kernel(a_ref, w_ref, x_ref, g_ref, gate_ref, o_ref, acc_ref, *, nk):
    k = pl.program_id(1)
    part = _dot(a_ref[...].astype(bf16), w_ref[...])

    @pl.when(k == 0)
    def _():
        acc_ref[...] = part

    @pl.when(k > 0)
    def _():
        acc_ref[...] += part

    @pl.when(k == nk - 1)
    def _():
        f = acc_ref[...]
        ms = jnp.mean(f * f, axis=-1, keepdims=True)
        y = f * lax.rsqrt(ms + NORM_EPS) * g_ref[...]
        o_ref[...] = x_ref[...] + gate_ref[...] * y


def mm_norm_res(a, w, x, g, gate, T, tk):
    M, K = a.shape
    D = w.shape[1]
    tm = _row_tile(M, T, 512)
    nk = K // tk
    gt, gt_blk, gt_map = _group_vec(gate, T, tm)
    return pl.pallas_call(
        functools.partial(_mm_norm_res_kernel, nk=nk), grid=(M // tm, nk),
        in_specs=[pl.BlockSpec((tm, tk), lambda i, k: (i, k)),
                  pl.BlockSpec((tk, D), lambda i, k: (k, 0)),
                  pl.BlockSpec((tm, D), lambda i, k: (i, 0)),
                  pl.BlockSpec((1, D), lambda i, k: (0, 0)),
                  pl.BlockSpec(gt_blk, gt_map)],
        out_specs=pl.BlockSpec((tm, D), lambda i, k: (i, 0)),
        out_shape=jax.ShapeDtypeStruct((M, D), f32),
        scratch_shapes=[pltpu.VMEM((tm, D), f32)],
        compiler_params=_cparams("parallel", "arbitrary"), name="mm_norm_res")(
            a, w, x, g.reshape(1, D), gt)


def _merge_kernel(orw_ref, oml_ref, ofx_ref, wrw_ref, wml_ref, wfx_ref, g0_ref, g1_ref, g2_ref, o_ref):
    m = _sigmoid(g0_ref[...]) * _dot(orw_ref[...].astype(bf16), wrw_ref[...])
    m += _sigmoid(g1_ref[...]) * _dot(oml_ref[...].astype(bf16), wml_ref[...])
    m += _sigmoid(g2_ref[...]) * _dot(ofx_ref[...].astype(bf16), wfx_ref[...])
    o_ref[...] = m.astype(o_ref.dtype)


def merge(o_rw, o_ml, o_fx, w_rw, w_ml, w_fx, P):
    N = o_rw.shape[0]
    D = D_MODEL
    tm = min(N, 512)
    tn = 512
    nb = D // tn
    gspec = lambda b: pl.BlockSpec((tm, tn), lambda i, j: (i, OFF_GATE // tn + b * nb + j))
    return pl.pallas_call(
        _merge_kernel, grid=(N // tm, nb),
        in_specs=[pl.BlockSpec((tm, RW_W), lambda i, j: (i, 0)),
                  pl.BlockSpec((tm, ML_W), lambda i, j: (i, 0)),
                  pl.BlockSpec((tm, FX_W), lambda i, j: (i, 0)),
                  pl.BlockSpec((RW_W, tn), lambda i, j: (0, j)),
                  pl.BlockSpec((ML_W, tn), lambda i, j: (0, j)),
                  pl.BlockSpec((FX_W, tn), lambda i, j: (0, j)),
                  gspec(0), gspec(1), gspec(2)],
        out_specs=pl.BlockSpec((tm, tn), lambda i, j: (i, j)),
        out_shape=jax.ShapeDtypeStruct((N, D), bf16),
        compiler_params=_cparams("parallel", "arbitrary"), name="merge")(
            o_rw, o_ml, o_fx, w_rw, w_ml, w_fx, P, P, P)


def _ffn_act_kernel(a_ref, halo_ref, buf_ref, val_ref, cw_ref, cb_ref, o_ref, scr_ref, *, per, tm):
    first = (pl.program_id(0) % per) == 0
    scr_ref[0:SUBLANES, :] = jnp.where(first, buf_ref[...], halo_ref[...])
    scr_ref[SUBLANES:SUBLANES + tm, :] = a_ref[...]
    y = cb_ref[...]
    for j in range(FFN_CONV):
        off = SUBLANES - (FFN_CONV - 1) + j
        y = y + scr_ref[off:off + tm, :] * cw_ref[j:j + 1, :]
    c0 = math.sqrt(2.0 / math.pi)
    gelu = 0.5 * y * (1.0 + jnp.tanh(c0 * (y + 0.044715 * (y * y * y))))
    o_ref[...] = (gelu * val_ref[...]).astype(o_ref.dtype)


def ffn_act(av, buf8, conv_w, conv_b, T):
    N = av.shape[0]
    tm = min(T, 512)
    per = T // tm
    tn = FF_TK
    nj = D_FF_PAD // tn
    hb = tm // SUBLANES
    out_dtype = bf16 if tm % 16 == 0 else f32
    return pl.pallas_call(
        functools.partial(_ffn_act_kernel, per=per, tm=tm), grid=(N // tm, nj),
        in_specs=[pl.BlockSpec((tm, tn), lambda i, j: (i, j)),
                  pl.BlockSpec((SUBLANES, tn), lambda i, j: (jnp.maximum(i * hb - 1, 0), j)),
                  pl.BlockSpec((None, SUBLANES, tn), lambda i, j: (i // per, 0, j)),
                  pl.BlockSpec((tm, tn), lambda i, j: (i, nj + j)),
                  pl.BlockSpec((SUBLANES, tn), lambda i, j: (0, j)),
                  pl.BlockSpec((1, tn), lambda i, j: (0, j))],
        out_specs=pl.BlockSpec((tm, tn), lambda i, j: (i, j)),
        out_shape=jax.ShapeDtypeStruct((N, D_FF_PAD), out_dtype),
        scratch_shapes=[pltpu.VMEM((tm + SUBLANES, tn), f32)],
        compiler_params=_cparams("parallel", "arbitrary"), name="ffn_act")(
            av, av, buf8, av, conv_w, conv_b)


def _head_sum(x, ones_blk):
    return _dot_r(x, ones_blk)


def _rwkv_prep_kernel(p_ref, halo_ref, sp_ref, mu_ref, w0_ref, a0_ref, kk_ref, ka_ref, rk_ref,
                      wup_ref, aup_ref, gup_ref, ones_ref,
                      nkk_o, wr_o, w_o, b_o, k_o, v_o, br_o, kr_o, rkr_o, g_o, scr_ref, *, per, tm):
    first = (pl.program_id(0) % per) == 0
    scr_ref[0:SUBLANES, :] = jnp.where(first, sp_ref[...], halo_ref[...])
    p = p_ref[...]
    scr_ref[SUBLANES:SUBLANES + tm, :] = p
    prev = scr_ref[SUBLANES - 1:SUBLANES - 1 + tm, :]
    xs = p + (prev - p) * mu_ref[...]
    W = RW_W
    r, k, v = xs[:, 0:W], xs[:, W:2 * W], xs[:, 2 * W:3 * W]
    o = 3 * W
    dw = xs[:, o:o + RW_LORA_PAD]
    da = xs[:, o + RW_LORA_PAD:o + 2 * RW_LORA_PAD]
    dg = xs[:, o + 2 * RW_LORA_PAD:o + 2 * RW_LORA_PAD + R_G]
    w_raw = -_softplus(-(w0_ref[...] + _dot(jnp.tanh(dw).astype(bf16), wup_ref[...]))) - RW_DECAY_CLAMP
    w = jnp.exp(-jnp.exp(w_raw))
    a = _sigmoid(a0_ref[...] + _dot(da.astype(bf16), aup_ref[...]))
    g = _dot(_sigmoid(dg).astype(bf16), gup_ref[...])
    ones_blk = ones_ref[...]
    kk = k * kk_ref[...]
    nrm = jnp.sqrt(_head_sum(kk * kk, ones_blk))
    kk = kk / jnp.maximum(nrm, 1e-12)
    k2 = k * (1.0 + (a - 1.0) * ka_ref[...])
    b = kk * a
    nkk_o[...] = -kk
    wr_o[...] = w * r
    w_o[...] = w
    b_o[...] = b
    k_o[...] = k2
    v_o[...] = v
    br_o[...] = _head_sum(b * r, ones_blk)
    kr_o[...] = _head_sum(k2 * r, ones_blk)
    rkr_o[...] = _head_sum(r * k2 * rk_ref[...], ones_blk)
    g_o[...] = g


def rwkv_prep(P, col_blk, sp8, mu, w0, a0, k_k, k_a, r_k, w_up, a_up, g_up, ones_blk, T):
    N = P.shape[0]
    tm = min(T, 256)
    per = T // tm
    hb = tm // SUBLANES
    C = RW_PCOLS
    vec = lambda n: pl.BlockSpec((1, n), lambda i: (0, 0))
    full = lambda a: pl.BlockSpec(a.shape, lambda i: (0, 0))
    out = jax.ShapeDtypeStruct((N, RW_W), f32)
    ospec = pl.BlockSpec((tm, RW_W), lambda i: (i, 0))
    return pl.pallas_call(
        functools.partial(_rwkv_prep_kernel, per=per, tm=tm), grid=(N // tm,),
        in_specs=[pl.BlockSpec((tm, C), lambda i: (i, col_blk)),
                  pl.BlockSpec((SUBLANES, C), lambda i: (jnp.maximum(i * hb - 1, 0), col_blk)),
                  pl.BlockSpec((None, SUBLANES, C), lambda i: (i // per, 0, 0)),
                  vec(C), vec(RW_W), vec(RW_W), vec(RW_W), vec(RW_W), vec(RW_W),
                  full(w_up), full(a_up), full(g_up), full(ones_blk)],
        out_specs=[ospec] * 10, out_shape=[out] * 10,
        scratch_shapes=[pltpu.VMEM((tm + SUBLANES, C), f32)],
        compiler_params=_cparams("parallel"), name="rwkv_prep")(
            P, P, sp8, mu, w0, a0, k_k, k_a, r_k, w_up, a_up, g_up, ones_blk)


N_PAIR = RW_H // 2


def _rwkv_scan_kernel(nkk_ref, wr_ref, w_ref, b_ref, k_ref, v_ref, s0_ref,
                      z_ref, sa_ref, so_ref, S_ref, vT_ref, zacc_ref, saacc_ref, *, n_sub, n_steps):
    c = pl.program_id(1)

    @pl.when(c == 0)
    def _():
        S_ref[...] = s0_ref[...]

    lo = _iota((1, LANES), 1) < RW_DH
    lane = _iota((1, LANES), 1)
    for sub in range(n_sub):
        r0 = sub * CHUNK
        for j in range(N_PAIR):
            vT_ref[j] = v_ref[r0:r0 + CHUNK, j * LANES:(j + 1) * LANES].T
        zacc_ref[...] = jnp.zeros_like(zacc_ref)
        saacc_ref[...] = jnp.zeros_like(saacc_ref)

        def group(g, carry):
            base = pl.multiple_of(r0 + g * SUBLANES, SUBLANES)
            tiles = [[ref[pl.ds(base, SUBLANES), j * LANES:(j + 1) * LANES]
                      for ref in (nkk_ref, wr_ref, w_ref, b_ref, k_ref)] for j in range(N_PAIR)]
            for i in range(SUBLANES):
                tmask = lane == g * SUBLANES + i
                for j in range(N_PAIR):
                    nkk_r, wr_r, w_r, b_r, k_r = [tl[i:i + 1, :] for tl in tiles[j]]
                    S = S_ref[j]
                    t1 = S * nkk_r
                    a1 = jnp.sum(jnp.where(lo, t1, 0.0), axis=-1, keepdims=True)
                    b1 = jnp.sum(jnp.where(lo, 0.0, t1), axis=-1, keepdims=True)
                    t2 = S * wr_r
                    a2 = jnp.sum(jnp.where(lo, t2, 0.0), axis=-1, keepdims=True)
                    b2 = jnp.sum(jnp.where(lo, 0.0, t2), axis=-1, keepdims=True)
                    va = jnp.sum(jnp.where(tmask, vT_ref[j, 0:RW_DH, :], 0.0), axis=-1, keepdims=True)
                    vb = jnp.sum(jnp.where(tmask, vT_ref[j, RW_DH:2 * RW_DH, :], 0.0), axis=-1, keepdims=True)
                    sa = jnp.where(lo, a1, b1)
                    vp = jnp.where(lo, va, vb)
                    S_ref[j] = S * w_r + sa * b_r + vp * k_r
                    zacc_ref[j, 0:RW_DH, :] = jnp.where(tmask, a2, zacc_ref[j, 0:RW_DH, :])
                    zacc_ref[j, RW_DH:2 * RW_DH, :] = jnp.where(tmask, b2, zacc_ref[j, RW_DH:2 * RW_DH, :])
                    saacc_ref[j, 0:RW_DH, :] = jnp.where(tmask, a1, saacc_ref[j, 0:RW_DH, :])
                    saacc_ref[j, RW_DH:2 * RW_DH, :] = jnp.where(tmask, b1, saacc_ref[j, RW_DH:2 * RW_DH, :])
            return carry

        lax.fori_loop(0, n_steps // SUBLANES, group, 0)
        for j in range(N_PAIR):
            z_ref[r0:r0 + CHUNK, j * LANES:(j + 1) * LANES] = zacc_ref[j].T
            sa_ref[r0:r0 + CHUNK, j * LANES:(j + 1) * LANES] = saacc_ref[j].T

    so_ref[...] = S_ref[...]


def rwkv_scan(nkk, wr, w, b, k, v, s0, n_steps):
    B, Tp, W = nkk.shape
    tc = min(Tp, 4 * CHUNK)
    n_sub = tc // CHUNK
    rows = pl.BlockSpec((None, tc, W), lambda bi, c: (bi, c, 0))
    st = pl.BlockSpec((None, N_PAIR, RW_DH, LANES), lambda bi, c: (bi, 0, 0, 0))
    tile = pltpu.VMEM((N_PAIR, LANES, LANES), f32)
    return pl.pallas_call(
        functools.partial(_rwkv_scan_kernel, n_sub=n_sub, n_steps=min(n_steps, CHUNK)),
        grid=(B, Tp // tc),
        in_specs=[rows] * 6 + [st],
        out_specs=[rows, rows, st],
        out_shape=[jax.ShapeDtypeStruct((B, Tp, W), f32)] * 2 + [jax.ShapeDtypeStruct(s0.shape, f32)],
        scratch_shapes=[pltpu.VMEM((N_PAIR, RW_DH, LANES), f32), tile, tile, tile],
        compiler_params=_cparams("parallel", "arbitrary"), name="rwkv_scan")(nkk, wr, w, b, k, v, s0)


def _rwkv_post_kernel(z_ref, sa_ref, v_ref, br_ref, kr_ref, rkr_ref, g_ref, lnw_ref, lnb_ref, ones_ref, o_ref):
    ones_blk = ones_ref[...]
    v = v_ref[...]
    y = z_ref[...] + sa_ref[...] * br_ref[...] + v * kr_ref[...]
    mu = _head_sum(y, ones_blk) * (1.0 / RW_DH)
    yc = y - mu
    var = _head_sum(yc * yc, ones_blk) * (1.0 / RW_DH)
    yn = yc * lax.rsqrt(var + RW_LN_EPS) * lnw_ref[...] + lnb_ref[...]
    o_ref[...] = ((yn + rkr_ref[...] * v) * g_ref[...]).astype(o_ref.dtype)


def rwkv_post(z, sa, v, br, kr, rkr, g, ln_w, ln_b, ones_blk):
    N, W = z.shape
    tm = min(N, 512)
    rows = pl.BlockSpec((tm, W), lambda i: (i, 0))
    vec = pl.BlockSpec((1, W), lambda i: (0, 0))
    return pl.pallas_call(
        _rwkv_post_kernel, grid=(N // tm,),
        in_specs=[rows] * 7 + [vec, vec, pl.BlockSpec(ones_blk.shape, lambda i: (0, 0))],
        out_specs=rows, out_shape=jax.ShapeDtypeStruct((N, W), bf16),
        compiler_params=_cparams("parallel"), name="rwkv_post")(z, sa, v, br, kr, rkr, g, ln_w, ln_b, ones_blk)


def _mlstm_kernel(main_ref, halo_ref, buf_ref, sm_ref, bias_ref, cw_ref, cb_ref, gn_ref, tri_ref,
                  c0_ref, n0_ref, m0_ref,
                  o_ref, lf_o, F_o, FT_o, c_o, n_o, m_o,
                  scr_ref, ct_ref, n_ref, m_ref, carry_ref, *, t_real):
    c = pl.program_id(1)
    L = CHUNK

    @pl.when(c == 0)
    def _():
        ct_ref[...] = c0_ref[...]
        n_ref[...] = n0_ref[...]
        m_ref[...] = m0_ref[...]
        carry_ref[...] = jnp.zeros_like(carry_ref)

    main = main_ref[...]
    scr_ref[0:SUBLANES, :] = jnp.where(c == 0, buf_ref[...], halo_ref[...])
    scr_ref[SUBLANES:SUBLANES + L, :] = main[:, 0:ML_QK]
    qk = cb_ref[...]
    for j in range(ML_CONV):
        off = SUBLANES - (ML_CONV - 1) + j
        qk = qk + scr_ref[off:off + L, :] * cw_ref[j:j + 1, :]
    qk = qk * _sigmoid(qk)
    q_all = qk[:, 0:ML_W]
    k_all = qk[:, ML_W:ML_QK] * (ML_DH ** -0.5)
    v_all = main[:, ML_QK:ML_QK + ML_W]
    og_all = main[:, ML_QK + ML_W:ML_QK + 2 * ML_W]

    valid = (c * L + _iota((L, 1), 0)) < t_real
    pre = sm_ref[...] + bias_ref[...]
    li_all = jnp.where(valid, pre, NEG)
    lf_all = jnp.where(valid, _log_sigmoid(pre), 0.0)
    cum = _dot_l(tri_ref[...], lf_all)
    F = cum + carry_ref[...]
    carry_ref[...] = F[L - 1:L, :]
    lf_o[...] = lf_all
    F_o[...] = F
    FT_o[...] = F.T
    liT = li_all.T
    cumT = cum.T

    row = _iota((L, L), 0)
    col = _iota((L, L), 1)
    causal = col <= row
    for h in range(ML_H):
        hs = slice(h * ML_DH, (h + 1) * ML_DH)
        b_col = cum[:, SM_LF + h:SM_LF + h + 1]
        b_row = cumT[SM_LF + h:SM_LF + h + 1, :]
        li_col = li_all[:, SM_LI + h:SM_LI + h + 1]
        li_row = liT[SM_LI + h:SM_LI + h + 1, :]
        m_prev = m_ref[h][:, 0:1]
        d = jnp.where(causal, b_col - b_row + li_row, NEG)
        inter = b_col + m_prev
        m_t = jnp.maximum(inter, jnp.max(d, axis=-1, keepdims=True))
        w_intra = jnp.exp(d - m_t)
        w_state = jnp.exp(inter - m_t)
        q = q_all[:, hs]
        k = k_all[:, hs]
        v = v_all[:, hs]
        qb, kb = q.astype(bf16), k.astype(bf16)
        s = _dot_nt(qb, kb) * w_intra
        ct = ct_ref[h]
        num = w_state * _dot(qb, ct.astype(bf16)) + _dot(s.astype(bf16), v.astype(bf16))
        n_row = n_ref[h]
        den = w_state * jnp.sum(q * n_row, axis=-1, keepdims=True) + jnp.sum(s, axis=-1, keepdims=True)
        hh = num / jnp.maximum(jnp.abs(den), jnp.exp(-m_t))
        g_end = b_col[L - 1:L, :]
        lw_s = g_end - b_col + li_col
        m_new = jnp.maximum(g_end + m_prev, jnp.max(lw_s, axis=0, keepdims=True))
        w_s = jnp.exp(lw_s - m_new)
        decay = jnp.exp(g_end + m_prev - m_new)
        ct_ref[h] = decay * ct + _dot(kb.T, (w_s * v).astype(bf16))
        n_ref[h] = decay * n_row + jnp.sum(w_s * k, axis=0, keepdims=True)
        m_ref[h] = jnp.broadcast_to(m_new, (1, LANES))
        mu = jnp.mean(hh, axis=-1, keepdims=True)
        hc = hh - mu
        var = jnp.mean(hc * hc, axis=-1, keepdims=True)
        hn = hc * lax.rsqrt(var + ML_GN_EPS) * gn_ref[:, hs]
        o_ref[:, hs] = (_sigmoid(og_all[:, hs]) * hn).astype(o_ref.dtype)

    c_o[...] = ct_ref[...]
    n_o[...] = n_ref[...]
    m_o[...] = m_ref[...]


def mlstm_fox_prep(main_arr, main_blk, small_arr, small_blk, buf8, bias_row, conv_w, conv_b, gn_w, tri,
                   c0t, n0, m0, B, Tp, t_real):
    L = CHUNK
    nc = Tp // L
    hb = L // SUBLANES
    st_c = pl.BlockSpec((None, ML_H, ML_DH, ML_DH), lambda b, c: (b, 0, 0, 0))
    st_n = pl.BlockSpec((None, ML_H, 1, ML_DH), lambda b, c: (b, 0, 0, 0))
    rows = lambda w: pl.BlockSpec((L, w), lambda b, c: (b * nc + c, 0))
    vec = lambda n: pl.BlockSpec((1, n), lambda b, c: (0, 0))
    N = B * Tp
    return pl.pallas_call(
        functools.partial(_mlstm_kernel, t_real=t_real), grid=(B, nc),
        in_specs=[pl.BlockSpec((L, ML_PCOLS), lambda b, c: (b * nc + c, main_blk)),
                  pl.BlockSpec((SUBLANES, ML_QK), lambda b, c: (jnp.maximum((b * nc + c) * hb - 1, 0), 2 * main_blk)),
                  pl.BlockSpec((None, SUBLANES, ML_QK), lambda b, c: (b, 0, 0)),
                  pl.BlockSpec((L, LANES), lambda b, c: (b * nc + c, small_blk)),
                  vec(LANES),
                  pl.BlockSpec((SUBLANES, ML_QK), lambda b, c: (0, 0)),
                  vec(ML_QK), vec(ML_W),
                  pl.BlockSpec((L, L), lambda b, c: (0, 0)),
                  st_c, st_n, st_n],
        out_specs=[rows(ML_W), rows(LANES), rows(LANES),
                   pl.BlockSpec((None, LANES, L), lambda b, c: (b, 0, c)),
                   st_c, st_n, st_n],
        out_shape=[jax.ShapeDtypeStruct((N, ML_W), bf16),
                   jax.ShapeDtypeStruct((N, LANES), f32),
                   jax.ShapeDtypeStruct((N, LANES), f32),
                   jax.ShapeDtypeStruct((B, LANES, Tp), f32),
                   jax.ShapeDtypeStruct((B, ML_H, ML_DH, ML_DH), f32),
                   jax.ShapeDtypeStruct((B, ML_H, 1, ML_DH), f32),
                   jax.ShapeDtypeStruct((B, ML_H, 1, ML_DH), f32)],
        scratch_shapes=[pltpu.VMEM((L + SUBLANES, ML_QK), f32),
                        pltpu.VMEM((ML_H, ML_DH, ML_DH), f32),
                        pltpu.VMEM((ML_H, 1, ML_DH), f32),
                        pltpu.VMEM((ML_H, 1, LANES), f32),
                        pltpu.VMEM((1, LANES), f32)],
        compiler_params=_cparams("parallel", "arbitrary"), name="mlstm")(
            main_arr, main_arr, buf8, small_arr, bias_row, conv_w, conv_b, gn_w, tri, c0t, n0, m0)


FOX_HG = 2


def _fox_attn_kernel(q_ref, k_ref, v_ref, ft_ref, o_ref, m_ref, l_ref, acc_ref, *, tq, tk):
    h0 = pl.program_id(1) * FOX_HG
    qi = pl.program_id(2)
    m_ref[...] = jnp.full_like(m_ref, NEG)
    l_ref[...] = jnp.zeros_like(l_ref)
    acc_ref[...] = jnp.zeros_like(acc_ref)
    qbs = [(q_ref[:, u * FX_DH:(u + 1) * FX_DH] * (FX_DH ** -0.5)).astype(bf16) for u in range(FOX_HG)]

    def block(j, masked):
        k0 = pl.multiple_of(j * tk, tk)
        for u in range(FOX_HG):
            hs = slice(u * FX_DH, (u + 1) * FX_DH)
            kb = k_ref[pl.ds(k0, tk), hs].astype(bf16)
            vb = v_ref[pl.ds(k0, tk), hs].astype(bf16)
            s = _dot_nt(qbs[u], kb) - ft_ref[pl.ds(h0 + u, 1), pl.ds(k0, tk)]
            if masked:
                s = jnp.where(_iota((tq, tk), 1) + k0 <= qi * tq + _iota((tq, tk), 0), s, NEG)
            m = m_ref[u]
            m_new = jnp.maximum(m, _row_max(s))
            alpha = jnp.exp(m - m_new)
            p = jnp.exp(s - m_new)
            l_ref[u] = alpha * l_ref[u] + _row_sum(p)
            acc_ref[:, hs] = alpha * acc_ref[:, hs] + _dot(p.astype(bf16), vb)
            m_ref[u] = m_new

    n_full = (qi * tq) // tk

    def body(j, carry):
        block(j, False)
        return carry

    lax.fori_loop(0, n_full, body, 0)
    for d in range(max(tq // tk, 1)):
        block(n_full + d, True)
    for u in range(FOX_HG):
        hs = slice(u * FX_DH, (u + 1) * FX_DH)
        o_ref[:, hs] = (acc_ref[:, hs] / l_ref[u]).astype(o_ref.dtype)


def fox_attn(P, FT, B, T):
    tq = 256
    tk = 512
    nq = T // tq
    wb = FOX_HG * FX_DH
    qb, kb, vb = OFF_FXQ // wb, OFF_FXK // wb, OFF_FXV // wb
    return pl.pallas_call(
        functools.partial(_fox_attn_kernel, tq=tq, tk=tk), grid=(B, FX_H // FOX_HG, nq),
        in_specs=[pl.BlockSpec((tq, wb), lambda b, h, i: (b * nq + i, qb + h)),
                  pl.BlockSpec((T, wb), lambda b, h, i: (b, kb + h)),
                  pl.BlockSpec((T, wb), lambda b, h, i: (b, vb + h)),
                  pl.BlockSpec((None, SUBLANES, T), lambda b, h, i: (b, SM_FX // SUBLANES, 0))],
        out_specs=pl.BlockSpec((tq, wb), lambda b, h, i: (b * nq + i, h)),
        out_shape=jax.ShapeDtypeStruct((B * T, FX_W), bf16),
        scratch_shapes=[pltpu.VMEM((FOX_HG, tq, 1), f32), pltpu.VMEM((FOX_HG, tq, 1), f32),
                        pltpu.VMEM((tq, wb), f32)],
        compiler_params=_cparams("parallel", "parallel", "arbitrary"), name="fox_attn")(P, P, P, FT)


def _fox_suffix_kernel(pt_ref, lf_ref, triu_ref, later_ref, o_ref, x_ref, *, n_pages):
    b = pl.program_id(0)

    def gather(j, carry):
        x_ref[pl.ds(pl.multiple_of(j * FX_H, FX_H), FX_H), :] = lf_ref[pt_ref[b, j]]
        return carry

    lax.fori_loop(0, n_pages, gather, 0)
    x = x_ref[...]
    within = _dot_r(x, triu_ref[...])
    tot = jnp.broadcast_to(jnp.sum(x, axis=-1, keepdims=True), x.shape)
    o_ref[...] = within + _dot_l(later_ref[...], tot)


def fox_suffix(page_table, cache_lft, triu, later, layer):
    B, n_pages = page_table.shape
    n_pool = cache_lft.shape[1]
    rows = n_pages * FX_H
    grid_spec = pltpu.PrefetchScalarGridSpec(
        num_scalar_prefetch=1, grid=(B,),
        in_specs=[pl.BlockSpec((None, n_pool, FX_H, PAGE_SIZE), lambda b, pt: (layer, 0, 0, 0)),
                  pl.BlockSpec((PAGE_SIZE, PAGE_SIZE), lambda b, pt: (0, 0)),
                  pl.BlockSpec((rows, rows), lambda b, pt: (0, 0))],
        out_specs=pl.BlockSpec((None, rows, PAGE_SIZE), lambda b, pt: (b, 0, 0)),
        scratch_shapes=[pltpu.VMEM((rows, PAGE_SIZE), f32)])
    return pl.pallas_call(
        functools.partial(_fox_suffix_kernel, n_pages=n_pages), grid_spec=grid_spec,
        out_shape=jax.ShapeDtypeStruct((B, rows, PAGE_SIZE), f32),
        compiler_params=_cparams("parallel"), name="fox_suffix")(page_table, cache_lft, triu, later)


DEC_PP = 8


def _fox_decode_kernel(pt_ref, q_ref, *refs, n_steps, pp, t_new):
    kc = refs[0:pp]
    vc = refs[pp:2 * pp]
    suf_ref, hmask_ref, kn_ref, vn_ref, fkn_ref, o_ref, m_ref, l_ref, acc_ref = refs[2 * pp:]
    j = pl.program_id(1)
    nq = FX_H * t_new
    cols = PAGE_SIZE * FX_H

    @pl.when(j == 0)
    def _():
        m_ref[...] = jnp.full_like(m_ref, NEG)
        l_ref[...] = jnp.zeros_like(l_ref)
        acc_ref[...] = jnp.zeros_like(acc_ref)

    qb = (q_ref[...] * (FX_DH ** -0.5)).astype(bf16)

    def update(ss, vs):
        m = m_ref[...]
        m_new = m
        for s in ss:
            m_new = jnp.maximum(m_new, _row_max(s))
        alpha = jnp.exp(m - m_new)
        l = alpha * l_ref[...]
        acc = alpha * acc_ref[...]
        for s, vb in zip(ss, vs):
            p = jnp.exp(s - m_new)
            l = l + _row_sum(p)
            acc = acc + _dot(p.astype(bf16), vb)
        l_ref[...] = l
        acc_ref[...] = acc
        m_ref[...] = m_new

    @pl.when(j < n_steps)
    def _():
        hmask = hmask_ref[...]
        ss = [_dot_nt(qb, kc[i][...].reshape(cols, FX_DH).astype(bf16)) + hmask + suf_ref[i] for i in range(pp)]
        update(ss, [vc[i][...].reshape(cols, FX_DH).astype(bf16) for i in range(pp)])

    @pl.when(j == n_steps)
    def _():
        s = _dot_nt(qb, kn_ref[...].astype(bf16)) - fkn_ref[...]
        col = _iota((nq, LANES), 1)
        row = _iota((nq, LANES), 0)
        ok = (col % FX_H == row // t_new) & (col // FX_H <= row % t_new)
        update([jnp.where(ok, s, NEG)], [vn_ref[...].astype(bf16)])
        o_ref[...] = acc_ref[...] / l_ref[...]


def fox_decode(page_table, q2, cache_k, cache_v, suffix, hmask, k_new, v_new, fkn, layer, t_new):
    B, n_pages = page_table.shape
    pp = DEC_PP
    n_steps = n_pages // pp
    nq = FX_H * t_new
    cols = PAGE_SIZE * FX_H

    def page_spec(i):
        return pl.BlockSpec((None, None, PAGE_SIZE, FX_H, FX_DH),
                            lambda b, j, pt: (layer, pt[b, jnp.minimum(j, n_steps - 1) * pp + i], 0, 0, 0))

    per_b = lambda shape: pl.BlockSpec((None,) + shape, lambda b, j, pt: (b, 0, 0))
    grid_spec = pltpu.PrefetchScalarGridSpec(
        num_scalar_prefetch=1, grid=(B, n_steps + 1),
        in_specs=[per_b((nq, FX_DH))] + [page_spec(i) for i in range(pp)] * 2
        + [pl.BlockSpec((None, pp, 1, cols), lambda b, j, pt: (b, jnp.minimum(j, n_steps - 1), 0, 0)),
           pl.BlockSpec((nq, cols), lambda b, j, pt: (0, 0)),
           per_b((LANES, FX_DH)), per_b((LANES, FX_DH)), per_b((1, LANES))],
        out_specs=per_b((nq, FX_DH)),
        scratch_shapes=[pltpu.VMEM((nq, 1), f32), pltpu.VMEM((nq, 1), f32), pltpu.VMEM((nq, FX_DH), f32)])
    return pl.pallas_call(
        functools.partial(_fox_decode_kernel, n_steps=n_steps, pp=pp, t_new=t_new), grid_spec=grid_spec,
        out_shape=jax.ShapeDtypeStruct((B, nq, FX_DH), f32),
        compiler_params=_cparams("parallel", "arbitrary"), name="fox_decode")(
            page_table, q2, *([cache_k] * pp), *([cache_v] * pp), suffix, hmask, k_new, v_new, fkn)


def _pad_cols(w, n):
    return jnp.pad(w, ((0, 0), (0, n - w.shape[1])))


def _rw_cols(x):
    W = RW_W
    pad = [(0, 0)] * (x.ndim - 1) + [(0, RW_LORA_PAD - R_W)]
    return jnp.concatenate([x[..., 0:3 * W], jnp.pad(x[..., 3 * W:3 * W + R_W], pad),
                            jnp.pad(x[..., 3 * W + R_W:3 * W + R_W + R_A], pad),
                            x[..., 3 * W + R_W + R_A:]], axis=-1)


def _rw_cols_inv(x):
    W = RW_W
    o = 3 * W
    return jnp.concatenate([x[..., 0:o], x[..., o:o + R_W], x[..., o + RW_LORA_PAD:o + RW_LORA_PAD + R_A],
                            x[..., o + 2 * RW_LORA_PAD:]], axis=-1)


def _w_in_cat(w):
    o_ml = RW_COLS
    o_fx = o_ml + ML_COLS
    o_gt = o_fx + FX_COLS
    ml = w[:, o_ml:o_fx]
    fx = w[:, o_fx:o_gt]
    ml_main = jnp.concatenate([ml[:, 0:ML_QK + ML_W], ml[:, ML_QK + ML_W + 2 * ML_H:]], axis=1)
    small = _pad_cols(jnp.concatenate([ml[:, ML_QK + ML_W:ML_QK + ML_W + 2 * ML_H], fx[:, 3 * FX_W:]], axis=1), LANES)
    cat = jnp.concatenate([w[:, o_gt:], _rw_cols(w[:, 0:RW_COLS]), ml_main, fx[:, 0:3 * FX_W], small], axis=1)
    return cat.astype(bf16)


def _pad_rows(w, n):
    return jnp.pad(w, ((0, n - w.shape[0]), (0, 0)))


def _prev_rows8(buf, width):
    B, r, C = buf.shape
    return jnp.pad(buf, ((0, 0), (SUBLANES - r, 0), (0, width - C)))


def _layer(x, mod, st, past, W, page_table, l, B, T):
    N = B * T
    shift1, scale1, gate1, shift2, scale2, gate2 = jnp.split(mod, N_MOD, axis=-1)
    u = norm_mod(x, W['norm_pre_mix'], scale1, shift1, T)
    P = matmul(u, W['w_in'], tn=896 if N >= 1024 else 1920)

    padded = T % CHUNK != 0
    Tp = T if not padded else CHUNK

    def pad_t(a):
        if not padded:
            return a
        return jnp.pad(a.reshape(B, T, -1), ((0, 0), (0, Tp - T), (0, 0))).reshape(B * Tp, -1)

    def unpad_t(a):
        if not padded:
            return a
        return a.reshape(B, Tp, -1)[:, :T].reshape(N, -1)

    sp8 = _prev_rows8(_rw_cols(st['rw_shift'])[:, None, :], RW_PCOLS)
    prep = rwkv_prep(P, OFF_RW // RW_PCOLS, sp8, W['rw_mu'], W['rw_w0'], W['rw_a0'], W['rw_k_k'], W['rw_k_a'],
                     W['rw_r_k'], W['rw_w_up'], W['rw_a_up'], W['rw_g_up'], W['ones_blk'], T)
    nkk, wr, w_, b_, k2, v_rw, br, kr, rkr, g_rw = prep
    s0 = st['rw_wkv'].reshape(B, N_PAIR, 2, RW_DH, RW_DH).transpose(0, 1, 3, 2, 4).reshape(B, N_PAIR, RW_DH, LANES)
    seq = [pad_t(a).reshape(B, Tp, RW_W) for a in (nkk, wr, w_, b_, k2, v_rw)]
    z, sa, s_out = rwkv_scan(*seq, s0, T)
    z, sa = unpad_t(z.reshape(B * Tp, RW_W)), unpad_t(sa.reshape(B * Tp, RW_W))
    o_rw = rwkv_post(z, sa, v_rw, br, kr, rkr, g_rw, W['rw_ln_w'], W['rw_ln_b'], W['ones_blk'])
    rw_wkv = s_out.reshape(B, N_PAIR, RW_DH, 2, RW_DH).transpose(0, 1, 3, 2, 4).reshape(B, RW_H, RW_DH, RW_DH)
    rw_shift = _rw_cols_inv(P[:, OFF_RW:OFF_RW + RW_PCOLS].reshape(B, T, RW_PCOLS)[:, -1])

    buf8 = _prev_rows8(st['ml_conv'], ML_QK)
    c0t = jnp.swapaxes(st['ml_c'], -1, -2)
    n0 = st['ml_n'][:, :, None, :]
    m0 = jnp.broadcast_to(st['ml_m'][:, :, None, None], (B, ML_H, 1, LANES))
    if padded:
        main_arr, main_blk = pad_t(P[:, OFF_ML:OFF_ML + ML_PCOLS]), 0
        small_arr, small_blk = pad_t(P[:, OFF_SMALL:]), 0
    else:
        main_arr, main_blk, small_arr, small_blk = P, OFF_ML // ML_PCOLS, P, OFF_SMALL // LANES
    o_ml, lf_all, F, FT, ct, n_new, m_new = mlstm_fox_prep(
        main_arr, main_blk, small_arr, small_blk, buf8, W['small_bias'], W['ml_conv_w'], W['ml_conv_b'],
        W['ml_gn_w'], W['tri'], c0t, n0, m0, B, Tp, T)
    o_ml = unpad_t(o_ml)
    ml_c = jnp.swapaxes(ct, -1, -2)
    ml_n = n_new[:, :, 0, :]
    ml_m = m_new[:, :, 0, 0]
    ml_conv = P[:, OFF_ML:OFF_ML + ML_QK].reshape(B, T, ML_QK)[:, T - (ML_CONV - 1):]
    fox_logf = unpad_t(lf_all)[:, SM_FX:SM_FX + FX_H].reshape(B, T, FX_H)
    fox_k = P[:, OFF_FXK:OFF_FXK + FX_W].reshape(B, T, FX_H, FX_DH)
    fox_v = P[:, OFF_FXV:OFF_FXV + FX_W].reshape(B, T, FX_H, FX_DH)

    if past is None:
        o_fx = fox_attn(P, FT, B, T)
    else:
        cache_k, cache_v, cache_lft = past
        n_pages = page_table.shape[1]
        nq = FX_H * T
        assert nq <= LANES and n_pages % DEC_PP == 0
        suffix = fox_suffix(page_table, cache_lft, W['triu'], W['later'], l)
        suffix = suffix.reshape(B, n_pages, FX_H, PAGE_SIZE).transpose(0, 1, 3, 2).reshape(
            B, n_pages, 1, PAGE_SIZE * FX_H)
        q2 = P[:, OFF_FXQ:OFF_FXQ + FX_W].reshape(B, T, FX_H, FX_DH).transpose(0, 2, 1, 3).reshape(B, nq, FX_DH)
        padk = lambda a: jnp.pad(a.reshape(B, nq, FX_DH), ((0, 0), (0, LANES - nq), (0, 0)))
        fkn = F.reshape(B, Tp, LANES)[:, :T, SM_FX:SM_FX + FX_H].reshape(B, 1, nq)
        fkn = jnp.pad(fkn, ((0, 0), (0, 0), (0, LANES - nq)))
        hmask = jnp.where(jnp.arange(PAGE_SIZE * FX_H)[None, :] % FX_H == jnp.arange(nq)[:, None] // T, 0.0, NEG)
        o = fox_decode(page_table, q2, cache_k, cache_v, suffix, hmask.astype(f32), padk(fox_k), padk(fox_v),
                       fkn, l, T)
        o_fx = o.reshape(B, FX_H, T, FX_DH).transpose(0, 2, 1, 3).reshape(N, FX_W)

    merged = merge(o_rw, o_ml, o_fx, W['w_br_rwkv'], W['w_br_mlstm'], W['w_br_fox'], P)
    x = mm_norm_res(merged, W['w_out'], x, W['norm_post_mix'], gate1, T, D_MODEL)

    zf = norm_mod(x, W['norm_pre_ffn'], scale2, shift2, T)
    av = matmul(zf, W['ffn_w_gv'], tn=1024)
    fbuf8 = _prev_rows8(st['ffn_conv'], D_FF_PAD)
    hmid = ffn_act(av, fbuf8, W['ffn_conv_w'], W['ffn_conv_b'], T)
    x = mm_norm_res(hmid, W['ffn_w_down'], x, W['norm_post_ffn'], gate2, T, FF_TK)
    ffn_conv = av[:, 0:D_FF].reshape(B, T, D_FF)[:, T - (FFN_CONV - 1):]

    new = dict(fox_k=fox_k, fox_v=fox_v, fox_logf=fox_logf, rw_shift=rw_shift, rw_wkv=rw_wkv,
               ml_conv=ml_conv, ml_c=ml_c, ml_n=ml_n, ml_m=ml_m, ffn_conv=ffn_conv)
    return x, new


STATE_NAMES = ("fox_k", "fox_v", "fox_logf", "rw_shift", "rw_wkv", "ml_conv", "ml_c", "ml_n", "ml_m", "ffn_conv")


def _layer_weights(Pm, l, n_pages):
    row = lambda v: v.reshape(1, -1)
    W = {}
    for name in ('norm_pre_mix', 'norm_post_mix', 'norm_pre_ffn', 'norm_post_ffn'):
        W[name] = Pm[name][l]
    W['w_in'] = _w_in_cat(Pm['w_in'][l])
    W['rw_mu'] = row(_rw_cols(Pm['rw_mu'][l]))
    for name in ('rw_w0', 'rw_a0', 'rw_k_k', 'rw_k_a', 'rw_r_k', 'rw_ln_w', 'rw_ln_b'):
        W[name] = row(Pm[name][l])
    W['rw_w_up'] = _pad_rows(Pm['rw_w_up'][l], RW_LORA_PAD).astype(bf16)
    W['rw_a_up'] = _pad_rows(Pm['rw_a_up'][l], RW_LORA_PAD).astype(bf16)
    W['rw_g_up'] = Pm['rw_g_up'][l].astype(bf16)
    hid = jnp.arange(RW_W) // RW_DH
    W['ones_blk'] = (hid[:, None] == hid[None, :]).astype(bf16)
    W['small_bias'] = row(jnp.pad(jnp.concatenate([Pm['ml_b_i'][l], Pm['ml_b_f'][l], Pm['fx_b_f'][l]]),
                                  (0, LANES - 2 * ML_H - FX_H)))
    W['ml_conv_w'] = _pad_rows(Pm['ml_conv_w'][l], SUBLANES)
    W['ml_conv_b'] = row(Pm['ml_conv_b'][l])
    W['ml_gn_w'] = row(Pm['ml_gn_w'][l])
    idx = jnp.arange(CHUNK)
    W['tri'] = (idx[None, :] <= idx[:, None]).astype(bf16)
    W['triu'] = (idx[:, None] > idx[None, :]).astype(bf16)
    if n_pages:
        r = jnp.arange(n_pages * FX_H)
        W['later'] = ((r[:, None] % FX_H == r[None, :] % FX_H) & (r[None, :] // FX_H > r[:, None] // FX_H)).astype(bf16)
    W['w_br_rwkv'] = Pm['w_br_rwkv'][l].astype(bf16)
    W['w_br_mlstm'] = Pm['w_br_mlstm'][l].astype(bf16)
    W['w_br_fox'] = Pm['w_br_fox'][l].astype(bf16)
    W['w_out'] = Pm['w_out'][l].astype(bf16)
    W['ffn_w_gv'] = jnp.concatenate([_pad_cols(Pm['ffn_w_gate'][l], D_FF_PAD),
                                     _pad_cols(Pm['ffn_w_val'][l], D_FF_PAD)], axis=1).astype(bf16)
    W['ffn_conv_w'] = _pad_rows(_pad_cols(Pm['ffn_conv_w'][l], D_FF_PAD), SUBLANES)
    W['ffn_conv_b'] = _pad_cols(row(Pm['ffn_conv_b'][l]), D_FF_PAD)
    W['ffn_w_down'] = _pad_rows(Pm['ffn_w_down'][l], D_FF_PAD).astype(bf16)
    return W


def kernel(x_prompt, x_sample, cache_fox_k, cache_fox_v, cache_fox_logf, state_rwkv_shift, state_rwkv_wkv,
           state_mlstm_conv, state_mlstm_c, state_mlstm_n, state_mlstm_m, state_ffn_conv, page_table,
           c_prompt, c_sample, w_ada, b_ada, norm_pre_mix, norm_post_mix, norm_pre_ffn, norm_post_ffn, w_in,
           rw_mu, rw_w0, rw_w_up, rw_a0, rw_a_up, rw_g_up, rw_k_k, rw_k_a, rw_r_k, rw_ln_w, rw_ln_b,
           ml_conv_w, ml_conv_b, ml_b_i, ml_b_f, ml_gn_w, fx_b_f, w_br_rwkv, w_br_mlstm, w_br_fox, w_out,
           ffn_w_gate, ffn_w_val, ffn_conv_w, ffn_conv_b, ffn_w_down):
    Pm = dict(norm_pre_mix=norm_pre_mix, norm_post_mix=norm_post_mix, norm_pre_ffn=norm_pre_ffn,
              norm_post_ffn=norm_post_ffn, w_in=w_in, rw_mu=rw_mu, rw_w0=rw_w0, rw_w_up=rw_w_up, rw_a0=rw_a0,
              rw_a_up=rw_a_up, rw_g_up=rw_g_up, rw_k_k=rw_k_k, rw_k_a=rw_k_a,
              rw_r_k=rw_r_k.reshape(DEPTH, RW_W), rw_ln_w=rw_ln_w, rw_ln_b=rw_ln_b, ml_conv_w=ml_conv_w,
              ml_conv_b=ml_conv_b, ml_b_i=ml_b_i, ml_b_f=ml_b_f, ml_gn_w=ml_gn_w, fx_b_f=fx_b_f,
              w_br_rwkv=w_br_rwkv, w_br_mlstm=w_br_mlstm, w_br_fox=w_br_fox, w_out=w_out, ffn_w_gate=ffn_w_gate,
              ffn_w_val=ffn_w_val, ffn_conv_w=ffn_conv_w, ffn_conv_b=ffn_conv_b, ffn_w_down=ffn_w_down)
    Bp, Tpr, D = x_prompt.shape
    Bs, Ts, _ = x_sample.shape
    cache_k, cache_v = cache_fox_k, cache_fox_v
    cache_lft = jnp.swapaxes(cache_fox_logf, -1, -2)

    zeros = lambda *s: jnp.zeros(s, f32)
    xp = x_prompt.reshape(Bp * Tpr, D)
    xs = x_sample.reshape(Bs * Ts, D)
    c_all = jnp.pad(jnp.concatenate([c_prompt, c_sample], axis=0), ((0, 16 - Bp - Bs), (0, 0)))
    new_p = {n: [] for n in STATE_NAMES}
    new_s = {n: [] for n in STATE_NAMES}
    for l in range(DEPTH):
        W = _layer_weights(Pm, l, page_table.shape[1])
        mod = ada_mod(c_all, w_ada[l].astype(bf16), b_ada[l].reshape(1, -1))
        st_p = dict(rw_shift=zeros(Bp, RW_COLS), rw_wkv=zeros(Bp, RW_H, RW_DH, RW_DH),
                    ml_conv=zeros(Bp, ML_CONV - 1, ML_QK), ml_c=zeros(Bp, ML_H, ML_DH, ML_DH),
                    ml_n=zeros(Bp, ML_H, ML_DH), ml_m=zeros(Bp, ML_H), ffn_conv=zeros(Bp, FFN_CONV - 1, D_FF))
        st_s = dict(rw_shift=state_rwkv_shift[l], rw_wkv=state_rwkv_wkv[l], ml_conv=state_mlstm_conv[l],
                    ml_c=state_mlstm_c[l], ml_n=state_mlstm_n[l], ml_m=state_mlstm_m[l],
                    ffn_conv=state_ffn_conv[l])
        xp, lp = _layer(xp, mod[:Bp], st_p, None, W, None, l, Bp, Tpr)
        xs, ls = _layer(xs, mod[Bp:Bp + Bs], st_s, (cache_k, cache_v, cache_lft), W, page_table, l, Bs, Ts)
        for n in STATE_NAMES:
            new_p[n].append(lp[n])
            new_s[n].append(ls[n])
    sp = {n: jnp.stack(v) for n, v in new_p.items()}
    ss = {n: jnp.stack(v) for n, v in new_s.items()}
    return (xp.reshape(Bp, Tpr, D), xs.reshape(Bs, Ts, D),
            sp['fox_k'], ss['fox_k'], sp['fox_v'], ss['fox_v'], sp['fox_logf'], ss['fox_logf'],
            sp['rw_shift'], ss['rw_shift'], sp['rw_wkv'], ss['rw_wkv'],
            sp['ml_conv'], ss['ml_conv'], sp['ml_c'], ss['ml_c'], sp['ml_n'], ss['ml_n'], sp['ml_m'], ss['ml_m'],
            sp['ffn_conv'], ss['ffn_conv'])
```

```python
import functools
import math

import jax
import jax.numpy as jnp
from jax import lax
from jax.experimental import pallas as pl
from jax.experimental.pallas import tpu as pltpu

f32 = jnp.float32
bf16 = jnp.bfloat16

D_MODEL = 2048
DEPTH = 2
PAGE_SIZE = 128
RW_W = D_MODEL // 4
RW_DH = 64
RW_H = RW_W // RW_DH
R_W = max(32, int(round(1.8 * D_MODEL ** 0.5 / 32)) * 32)
R_A = R_W
R_G = max(32, int(round(0.6 * D_MODEL ** 0.8 / 32)) * 32)
RW_COLS = 3 * RW_W + R_W + R_A + R_G
RW_LN_EPS = 64e-5
RW_DECAY_CLAMP = 0.5
ML_W = D_MODEL // 4
ML_H = 4
ML_DH = ML_W // ML_H
ML_QK = 2 * ML_W
ML_CONV = 4
ML_COLS = ML_QK + ML_W + 2 * ML_H + ML_W
ML_GN_EPS = 1e-5
FX_W = D_MODEL // 2
FX_DH = 128
FX_H = FX_W // FX_DH
FX_COLS = 3 * FX_W + FX_H
N_BRANCH = 3
GATE_COLS = N_BRANCH * D_MODEL
D_FF = ((8 * D_MODEL // 3 + 127) // 128) * 128
FFN_CONV = 3
N_MOD = 6
NORM_EPS = 1e-6

LANES = 128
SUBLANES = 8
VMEM_LIMIT = 56 * 1024 * 1024
CHUNK = 128
NEG = -1e30

RW_LORA_PAD = LANES
RW_PCOLS = 3 * RW_W + 2 * RW_LORA_PAD + R_G
OFF_GATE = 0
OFF_RW = OFF_GATE + GATE_COLS
OFF_ML = OFF_RW + RW_PCOLS
ML_PCOLS = ML_QK + 2 * ML_W
OFF_FXQ = OFF_ML + ML_PCOLS
P_COLS = OFF_FXQ + FX_W
SM_LI, SM_LF, SM_FX = 0, ML_H, 2 * ML_H
D_FF_PAD = 5632
FF_TK = 1408


def _cparams(*sem):
    return pltpu.CompilerParams(dimension_semantics=sem, vmem_limit_bytes=VMEM_LIMIT)


def _split3(x):
    hi = x.astype(bf16)
    r = x - hi.astype(f32)
    mid = r.astype(bf16)
    lo = (r - mid.astype(f32)).astype(bf16)
    return hi, mid, lo


def _dot(a, b):
    return jnp.dot(a, b, preferred_element_type=f32)


def _dot_nt(a, b):
    return lax.dot_general(a, b, (((1,), (1,)), ((), ())), preferred_element_type=f32)


def _dot_r(x, m):
    hi, mid, lo = _split3(x)
    return _dot(hi, m) + _dot(mid, m) + _dot(lo, m)


def _dot_l(m, x):
    hi, mid, lo = _split3(x)
    return _dot(m, hi) + _dot(m, mid) + _dot(m, lo)


def _sigmoid(x):
    return 1.0 / (1.0 + jnp.exp(-x))


def _log_sigmoid(x):
    return jnp.minimum(x, 0.0) - jnp.log(1.0 + jnp.exp(-jnp.abs(x)))


def _softplus(x):
    return jnp.maximum(x, 0.0) + jnp.log(1.0 + jnp.exp(-jnp.abs(x)))


def _iota(shape, dim):
    return lax.broadcasted_iota(jnp.int32, shape, dim)


def _lane_tiles(x):
    return [x[:, i:i + LANES] for i in range(0, x.shape[-1], LANES)]


def _row_max(x):
    return jnp.max(functools.reduce(jnp.maximum, _lane_tiles(x)), axis=-1, keepdims=True)


def _row_sum(x):
    return jnp.sum(functools.reduce(jnp.add, _lane_tiles(x)), axis=-1, keepdims=True)


def _group_vec(v, T, tm):
    B, D = v.shape
    if tm <= T:
        per = T // tm
        return v[:, None, :], (None, 1, D), lambda i, *_: (i // per, 0, 0)
    rows = jnp.repeat(v, T, axis=0)
    return rows.reshape(-1, tm, D), (None, tm, D), lambda i, *_: (i, 0, 0)


def _row_tile(n_rows, T, want):
    return want if T >= want else n_rows


def _ada_kernel(c_ref, w_ref, b_ref, o_ref):
    c = c_ref[...]
    a = (c * _sigmoid(c)).astype(bf16)
    o_ref[...] = _dot(a, w_ref[...]) + b_ref[...]


def ada_mod(c, w, b):
    M, K = c.shape
    N = w.shape[1]
    tn = 1024
    return pl.pallas_call(
        _ada_kernel, grid=(N // tn,),
        in_specs=[pl.BlockSpec((M, K), lambda j: (0, 0)),
                  pl.BlockSpec((K, tn), lambda j: (0, j)),
                  pl.BlockSpec((1, tn), lambda j: (0, j))],
        out_specs=pl.BlockSpec((M, tn), lambda j: (0, j)),
        out_shape=jax.ShapeDtypeStruct((M, N), f32),
        compiler_params=_cparams("parallel"), name="ada_mod")(c, w, b)


def _norm_mod_kernel(x_ref, g_ref, sc_ref, sh_ref, o_ref):
    x = x_ref[...]
    ms = jnp.mean(x * x, axis=-1, keepdims=True)
    y = x * lax.rsqrt(ms + NORM_EPS) * g_ref[...]
    o_ref[...] = (y * (1.0 + sc_ref[...]) + sh_ref[...]).astype(o_ref.dtype)


def norm_mod(x, g, scale, shift, T):
    N, D = x.shape
    tm = _row_tile(N, T, 512)
    sc, sc_blk, sc_map = _group_vec(scale, T, tm)
    sh, _, _ = _group_vec(shift, T, tm)
    return pl.pallas_call(
        _norm_mod_kernel, grid=(N // tm,),
        in_specs=[pl.BlockSpec((tm, D), lambda i: (i, 0)),
                  pl.BlockSpec((1, D), lambda i: (0, 0)),
                  pl.BlockSpec(sc_blk, sc_map), pl.BlockSpec(sc_blk, sc_map)],
        out_specs=pl.BlockSpec((tm, D), lambda i: (i, 0)),
        out_shape=jax.ShapeDtypeStruct((N, D), bf16),
        compiler_params=_cparams("parallel"), name="norm_mod")(x, g.reshape(1, D), sc, sh)


def _mm_kernel(a_ref, w_ref, o_ref):
    o_ref[...] = _dot(a_ref[...], w_ref[...]).astype(o_ref.dtype)


def matmul(a, w, tn, out_dtype=f32):
    M, K = a.shape
    N = w.shape[1]
    tm = min(M, 1024)
    return pl.pallas_call(
        _mm_kernel, grid=(M // tm, N // tn),
        in_specs=[pl.BlockSpec((tm, K), lambda i, j: (i, 0)),
                  pl.BlockSpec((K, tn), lambda i, j: (0, j))],
        out_specs=pl.BlockSpec((tm, tn), lambda i, j: (i, j)),
        out_shape=jax.ShapeDtypeStruct((M, N), out_dtype),
        compiler_params=_cparams("parallel", "arbitrary"), name="matmul")(a, w)


def _mm_norm_res_kernel(a_ref, w_ref, x_ref, g_ref, gate_ref, o_ref, acc_ref, *, nk):
    k = pl.program_id(1)
    part = _dot(a_ref[...].astype(bf16), w_ref[...])

    @pl.when(k == 0)
    def _():
        acc_ref[...] = part

    @pl.when(k > 0)
    def _():
        acc_ref[...] += part

    @pl.when(k == nk - 1)
    def _():
        f = acc_ref[...]
        ms = jnp.mean(f * f, axis=-1, keepdims=True)
        y = f * lax.rsqrt(ms + NORM_EPS) * g_ref[...]
        o_ref[...] = x_ref[...] + gate_ref[...] * y


def mm_norm_res(a, w, x, g, gate, T, tk):
    M, K = a.shape
    D = w.shape[1]
    tm = _row_tile(M, T, 512)
    nk = K // tk
    gt, gt_blk, gt_map = _group_vec(gate, T, tm)
    return pl.pallas_call(
        functools.partial(_mm_norm_res_kernel, nk=nk), grid=(M // tm, nk),
        in_specs=[pl.BlockSpec((tm, tk), lambda i, k: (i, k)),
                  pl.BlockSpec((tk, D), lambda i, k: (k, 0)),
                  pl.BlockSpec((tm, D), lambda i, k: (i, 0)),
                  pl.BlockSpec((1, D), lambda i, k: (0, 0)),
                  pl.BlockSpec(gt_blk, gt_map)],
        out_specs=pl.BlockSpec((tm, D), lambda i, k: (i, 0)),
        out_shape=jax.ShapeDtypeStruct((M, D), f32),
        scratch_shapes=[pltpu.VMEM((tm, D), f32)],
        compiler_params=_cparams("parallel", "arbitrary"), name="mm_norm_res")(
            a, w, x, g.reshape(1, D), gt)


def _merge_kernel(orw_ref, oml_ref, ofx_ref, wrw_ref, wml_ref, wfx_ref, g0_ref, g1_ref, g2_ref, o_ref):
    m = _sigmoid(g0_ref[...]) * _dot(orw_ref[...].astype(bf16), wrw_ref[...])
    m += _sigmoid(g1_ref[...]) * _dot(oml_ref[...].astype(bf16), wml_ref[...])
    m += _sigmoid(g2_ref[...]) * _dot(ofx_ref[...].astype(bf16), wfx_ref[...])
    o_ref[...] = m.astype(o_ref.dtype)


def merge(o_rw, o_ml, o_fx, w_rw, w_ml, w_fx, P):
    N = o_rw.shape[0]
    D = D_MODEL
    tm = min(N, 512)
    tn = 512
    nb = D // tn
    gspec = lambda b: pl.BlockSpec((tm, tn), lambda i, j: (i, OFF_GATE // tn + b * nb + j))
    return pl.pallas_call(
        _merge_kernel, grid=(N // tm, nb),
        in_specs=[pl.BlockSpec((tm, RW_W), lambda i, j: (i, 0)),
                  pl.BlockSpec((tm, ML_W), lambda i, j: (i, 0)),
                  pl.BlockSpec((tm, FX_W), lambda i, j: (i, 0)),
                  pl.BlockSpec((RW_W, tn), lambda i, j: (0, j)),
                  pl.BlockSpec((ML_W, tn), lambda i, j: (0, j)),
                  pl.BlockSpec((FX_W, tn), lambda i, j: (0, j)),
                  gspec(0), gspec(1), gspec(2)],
        out_specs=pl.BlockSpec((tm, tn), lambda i, j: (i, j)),
        out_shape=jax.ShapeDtypeStruct((N, D), bf16),
        compiler_params=_cparams("parallel", "arbitrary"), name="merge")(
            o_rw, o_ml, o_fx, w_rw, w_ml, w_fx, P, P, P)


def _ffn_act_kernel(a_ref, halo_ref, buf_ref, val_ref, cw_ref, cb_ref, o_ref, scr_ref, *, per, tm):
    first = (pl.program_id(0) % per) == 0
    scr_ref[0:SUBLANES, :] = jnp.where(first, buf_ref[...], halo_ref[...])
    scr_ref[SUBLANES:SUBLANES + tm, :] = a_ref[...]
    y = cb_ref[...]
    for j in range(FFN_CONV):
        off = SUBLANES - (FFN_CONV - 1) + j
        y = y + scr_ref[off:off + tm, :] * cw_ref[j:j + 1, :]
    c0 = math.sqrt(2.0 / math.pi)
    gelu = 0.5 * y * (1.0 + jnp.tanh(c0 * (y + 0.044715 * (y * y * y))))
    o_ref[...] = (gelu * val_ref[...]).astype(o_ref.dtype)


def ffn_act(av, buf8, conv_w, conv_b, T):
    N = av.shape[0]
    tm = min(T, 512)
    per = T // tm
    tn = FF_TK
    nj = D_FF_PAD // tn
    hb = tm // SUBLANES
    out_dtype = bf16 if tm % 16 == 0 else f32
    return pl.pallas_call(
        functools.partial(_ffn_act_kernel, per=per, tm=tm), grid=(N // tm, nj),
        in_specs=[pl.BlockSpec((tm, tn), lambda i, j: (i, j)),
                  pl.BlockSpec((SUBLANES, tn), lambda i, j: (jnp.maximum(i * hb - 1, 0), j)),
                  pl.BlockSpec((None, SUBLANES, tn), lambda i, j: (i // per, 0, j)),
                  pl.BlockSpec((tm, tn), lambda i, j: (i, nj + j)),
                  pl.BlockSpec((SUBLANES, tn), lambda i, j: (0, j)),
                  pl.BlockSpec((1, tn), lambda i, j: (0, j))],
        out_specs=pl.BlockSpec((tm, tn), lambda i, j: (i, j)),
        out_shape=jax.ShapeDtypeStruct((N, D_FF_PAD), out_dtype),
        scratch_shapes=[pltpu.VMEM((tm + SUBLANES, tn), f32)],
        compiler_params=_cparams("parallel", "arbitrary"), name="ffn_act")(
            av, av, buf8, av, conv_w, conv_b)


BF16_ROWS = 16


def _ffn_up_kernel(z_ref, zh_ref, wg_ref, wv_ref, buf_ref, cw_ref, cb_ref, h_ref, tail_ref, scr_ref, *, per, tm):
    first = (pl.program_id(0) % per) == 0
    wg = wg_ref[...]
    a = _dot(z_ref[...], wg)
    val = _dot(z_ref[...], wv_ref[...])
    a_prev = _dot(zh_ref[...], wg)[BF16_ROWS - SUBLANES:, :]
    scr_ref[0:SUBLANES, :] = jnp.where(first, buf_ref[...], a_prev)
    scr_ref[SUBLANES:SUBLANES + tm, :] = a
    y = cb_ref[...]
    for j in range(FFN_CONV):
        off = SUBLANES - (FFN_CONV - 1) + j
        y = y + scr_ref[off:off + tm, :] * cw_ref[j:j + 1, :]
    c0 = math.sqrt(2.0 / math.pi)
    gelu = 0.5 * y * (1.0 + jnp.tanh(c0 * (y + 0.044715 * (y * y * y))))
    h_ref[...] = (gelu * val).astype(h_ref.dtype)
    tail_ref[...] = a[tm - SUBLANES:tm, :]


def ffn_up(z, w_gv, buf8, conv_w, conv_b, T):
    N, D = z.shape
    tm = min(T, 1024)
    per = T // tm
    tn = 512
    nj = D_FF_PAD // tn
    hb = tm // BF16_ROWS
    return pl.pallas_call(
        functools.partial(_ffn_up_kernel, per=per, tm=tm), grid=(N // tm, nj),
        in_specs=[pl.BlockSpec((tm, D), lambda i, j: (i, 0)),
                  pl.BlockSpec((BF16_ROWS, D), lambda i, j: (jnp.maximum(i * hb - 1, 0), 0)),
                  pl.BlockSpec((D, tn), lambda i, j: (0, j)),
                  pl.BlockSpec((D, tn), lambda i, j: (0, nj + j)),
                  pl.BlockSpec((None, SUBLANES, tn), lambda i, j: (i // per, 0, j)),
                  pl.BlockSpec((SUBLANES, tn), lambda i, j: (0, j)),
                  pl.BlockSpec((1, tn), lambda i, j: (0, j))],
        out_specs=[pl.BlockSpec((tm, tn), lambda i, j: (i, j)),
                   pl.BlockSpec((SUBLANES, tn), lambda i, j: (i, j))],
        out_shape=[jax.ShapeDtypeStruct((N, D_FF_PAD), bf16),
                   jax.ShapeDtypeStruct((N // tm * SUBLANES, D_FF_PAD), f32)],
        scratch_shapes=[pltpu.VMEM((tm + SUBLANES, tn), f32)],
        compiler_params=_cparams("parallel", "arbitrary"), name="ffn_up")(
            z, z, w_gv, w_gv, buf8, conv_w, conv_b)


def _head_sum(x, ones_blk):
    return _dot_r(x, ones_blk)


def _rwkv_prep_kernel(p_ref, halo_ref, sp_ref, mu_ref, w0_ref, a0_ref, kk_ref, ka_ref, rk_ref,
                      wup_ref, aup_ref, gup_ref, ones_ref,
                      nkk_o, wr_o, w_o, b_o, k_o, v_o, br_o, kr_o, rkr_o, g_o, scr_ref, *, per, tm):
    first = (pl.program_id(0) % per) == 0
    scr_ref[0:SUBLANES, :] = jnp.where(first, sp_ref[...], halo_ref[...])
    p = p_ref[...]
    scr_ref[SUBLANES:SUBLANES + tm, :] = p
    prev = scr_ref[SUBLANES - 1:SUBLANES - 1 + tm, :]
    xs = p + (prev - p) * mu_ref[...]
    W = RW_W
    r, k, v = xs[:, 0:W], xs[:, W:2 * W], xs[:, 2 * W:3 * W]
    o = 3 * W
    dw = xs[:, o:o + RW_LORA_PAD]
    da = xs[:, o + RW_LORA_PAD:o + 2 * RW_LORA_PAD]
    dg = xs[:, o + 2 * RW_LORA_PAD:o + 2 * RW_LORA_PAD + R_G]
    w_raw = -_softplus(-(w0_ref[...] + _dot(jnp.tanh(dw).astype(bf16), wup_ref[...]))) - RW_DECAY_CLAMP
    w = jnp.exp(-jnp.exp(w_raw))
    a = _sigmoid(a0_ref[...] + _dot(da.astype(bf16), aup_ref[...]))
    g = _dot(_sigmoid(dg).astype(bf16), gup_ref[...])
    ones_blk = ones_ref[...]
    kk = k * kk_ref[...]
    nrm = jnp.sqrt(_head_sum(kk * kk, ones_blk))
    kk = kk / jnp.maximum(nrm, 1e-12)
    k2 = k * (1.0 + (a - 1.0) * ka_ref[...])
    b = kk * a
    nkk_o[...] = -kk
    wr_o[...] = w * r
    w_o[...] = w
    b_o[...] = b
    k_o[...] = k2
    v_o[...] = v
    br_o[...] = _head_sum(b * r, ones_blk)
    kr_o[...] = _head_sum(k2 * r, ones_blk)
    rkr_o[...] = _head_sum(r * k2 * rk_ref[...], ones_blk)
    g_o[...] = g


def rwkv_prep(P, col_blk, sp8, mu, w0, a0, k_k, k_a, r_k, w_up, a_up, g_up, ones_blk, T):
    N = P.shape[0]
    tm = min(T, 256)
    per = T // tm
    hb = tm // SUBLANES
    C = RW_PCOLS
    vec = lambda n: pl.BlockSpec((1, n), lambda i: (0, 0))
    full = lambda a: pl.BlockSpec(a.shape, lambda i: (0, 0))
    out = jax.ShapeDtypeStruct((N, RW_W), f32)
    ospec = pl.BlockSpec((tm, RW_W), lambda i: (i, 0))
    return pl.pallas_call(
        functools.partial(_rwkv_prep_kernel, per=per, tm=tm), grid=(N // tm,),
        in_specs=[pl.BlockSpec((tm, C), lambda i: (i, col_blk)),
                  pl.BlockSpec((SUBLANES, C), lambda i: (jnp.maximum(i * hb - 1, 0), col_blk)),
                  pl.BlockSpec((None, SUBLANES, C), lambda i: (i // per, 0, 0)),
                  vec(C), vec(RW_W), vec(RW_W), vec(RW_W), vec(RW_W), vec(RW_W),
                  full(w_up), full(a_up), full(g_up), full(ones_blk)],
        out_specs=[ospec] * 10, out_shape=[out] * 10,
        scratch_shapes=[pltpu.VMEM((tm + SUBLANES, C), f32)],
        compiler_params=_cparams("parallel"), name="rwkv_prep")(
            P, P, sp8, mu, w0, a0, k_k, k_a, r_k, w_up, a_up, g_up, ones_blk)


N_PAIR = RW_H // 2


def _rwkv_scan_kernel(nkk_ref, wr_ref, w_ref, b_ref, k_ref, v_ref, s0_ref, ones_ref,
                      z_ref, sa_ref, so_ref, S_ref, vT_ref, *, n_sub, n_steps, nb):
    c = pl.program_id(1)
    chains = [(bb, j) for bb in range(nb) for j in range(N_PAIR)]

    @pl.when(c == 0)
    def _():
        S_ref[...] = s0_ref[...]

    if n_steps < CHUNK:
        z_ref[...] = jnp.zeros_like(z_ref)
        sa_ref[...] = jnp.zeros_like(sa_ref)

    lo = _iota((1, LANES), 1) < RW_DH
    lane = _iota((1, LANES), 1)
    diag = _iota((RW_DH, LANES), 1) % RW_DH == _iota((RW_DH, LANES), 0)
    pair_ones = ones_ref[...]

    def as_row(x):
        return jnp.sum(jnp.where(diag, x, 0.0), axis=0, keepdims=True)

    for sub in range(n_sub):
        r0 = sub * CHUNK
        for q, (bb, j) in enumerate(chains):
            vT_ref[q] = v_ref[bb, r0:r0 + CHUNK, j * LANES:(j + 1) * LANES].T

        def group(g, carry):
            base = pl.multiple_of(r0 + g * SUBLANES, SUBLANES)
            tiles = [[ref[bb, pl.ds(base, SUBLANES), j * LANES:(j + 1) * LANES]
                      for ref in (nkk_ref, wr_ref, w_ref, b_ref, k_ref)] for bb, j in chains]
            z_rows = [[] for _ in chains]
            sa_rows = [[] for _ in chains]
            for i in range(SUBLANES):
                tmask = lane == g * SUBLANES + i
                for q in range(len(chains)):
                    nkk_r, wr_r, w_r, b_r, k_r = [tl[i:i + 1, :] for tl in tiles[q]]
                    S = S_ref[q]
                    x = jnp.concatenate([S * nkk_r, S * wr_r], axis=0)
                    hi = x.astype(bf16)
                    low = (x - hi.astype(f32)).astype(bf16)
                    red = _dot(jnp.concatenate([hi, low], axis=0), pair_ones)
                    red = red[0:2 * RW_DH] + red[2 * RW_DH:4 * RW_DH]
                    sa = red[0:RW_DH]
                    zz = red[RW_DH:2 * RW_DH]
                    va = jnp.sum(jnp.where(tmask, vT_ref[q, 0:RW_DH, :], 0.0), axis=-1, keepdims=True)
                    vb = jnp.sum(jnp.where(tmask, vT_ref[q, RW_DH:2 * RW_DH, :], 0.0), axis=-1, keepdims=True)
                    vp = jnp.where(lo, va, vb)
                    S_ref[q] = S * w_r + sa * b_r + vp * k_r
                    z_rows[q].append(as_row(zz))
                    sa_rows[q].append(as_row(sa))
            for q, (bb, j) in enumerate(chains):
                cs = slice(j * LANES, (j + 1) * LANES)
                z_ref[bb, pl.ds(base, SUBLANES), cs] = jnp.concatenate(z_rows[q], axis=0)
                sa_ref[bb, pl.ds(base, SUBLANES), cs] = jnp.concatenate(sa_rows[q], axis=0)
            return carry

        lax.fori_loop(0, n_steps // SUBLANES, group, 0)

    so_ref[...] = S_ref[...]


def rwkv_scan(nkk, wr, w, b, k, v, s0, pair_ones, n_steps):
    B, Tp, W = nkk.shape
    nb = 2 if B % 2 == 0 else 1
    tc = min(Tp, 4 * CHUNK)
    n_sub = tc // CHUNK
    rows = pl.BlockSpec((None, nb, tc, W), lambda bi, c: (bi, 0, c, 0))
    st = pl.BlockSpec((None, nb * N_PAIR, RW_DH, LANES), lambda bi, c: (bi, 0, 0, 0))
    grp = lambda a: a.reshape(B // nb, nb, Tp, W)
    z, sa, so = pl.pallas_call(
        functools.partial(_rwkv_scan_kernel, n_sub=n_sub, n_steps=min(n_steps, CHUNK), nb=nb),
        grid=(B // nb, Tp // tc),
        in_specs=[rows] * 6 + [st, pl.BlockSpec((LANES, LANES), lambda bi, c: (0, 0))],
        out_specs=[rows, rows, st],
        out_shape=[jax.ShapeDtypeStruct((B // nb, nb, Tp, W), f32)] * 2
        + [jax.ShapeDtypeStruct((B // nb, nb * N_PAIR, RW_DH, LANES), f32)],
        scratch_shapes=[pltpu.VMEM((nb * N_PAIR, RW_DH, LANES), f32), pltpu.VMEM((nb * N_PAIR, LANES, LANES), f32)],
        compiler_params=_cparams("parallel", "arbitrary"), name="rwkv_scan")(
            grp(nkk), grp(wr), grp(w), grp(b), grp(k), grp(v),
            s0.reshape(B // nb, nb * N_PAIR, RW_DH, LANES), pair_ones)
    return z.reshape(B, Tp, W), sa.reshape(B, Tp, W), so.reshape(s0.shape)


def _rwkv_post_kernel(z_ref, sa_ref, v_ref, br_ref, kr_ref, rkr_ref, g_ref, lnw_ref, lnb_ref, ones_ref, o_ref):
    ones_blk = ones_ref[...]
    v = v_ref[...]
    y = z_ref[...] + sa_ref[...] * br_ref[...] + v * kr_ref[...]
    mu = _head_sum(y, ones_blk) * (1.0 / RW_DH)
    yc = y - mu
    var = _head_sum(yc * yc, ones_blk) * (1.0 / RW_DH)
    yn = yc * lax.rsqrt(var + RW_LN_EPS) * lnw_ref[...] + lnb_ref[...]
    o_ref[...] = ((yn + rkr_ref[...] * v) * g_ref[...]).astype(o_ref.dtype)


def rwkv_post(z, sa, v, br, kr, rkr, g, ln_w, ln_b, ones_blk):
    N, W = z.shape
    tm = min(N, 512)
    rows = pl.BlockSpec((tm, W), lambda i: (i, 0))
    vec = pl.BlockSpec((1, W), lambda i: (0, 0))
    return pl.pallas_call(
        _rwkv_post_kernel, grid=(N // tm,),
        in_specs=[rows] * 7 + [vec, vec, pl.BlockSpec(ones_blk.shape, lambda i: (0, 0))],
        out_specs=rows, out_shape=jax.ShapeDtypeStruct((N, W), bf16),
        compiler_params=_cparams("parallel"), name="rwkv_post")(z, sa, v, br, kr, rkr, g, ln_w, ln_b, ones_blk)


def _mlstm_kernel(main_ref, halo_ref, buf_ref, sm_ref, bias_ref, cw_ref, cb_ref, gn_ref, tri_ref,
                  c0_ref, n0_ref, m0_ref,
                  o_ref, lf_o, F_o, FT_o, c_o, n_o, m_o,
                  scr_ref, ct_ref, n_ref, m_ref, carry_ref, *, t_real):
    c = pl.program_id(1)
    L = CHUNK

    @pl.when(c == 0)
    def _():
        ct_ref[...] = c0_ref[...]
        n_ref[...] = n0_ref[...]
        m_ref[...] = m0_ref[...]
        carry_ref[...] = jnp.zeros_like(carry_ref)

    main = main_ref[...]
    scr_ref[0:SUBLANES, :] = jnp.where(c == 0, buf_ref[...], halo_ref[...])
    scr_ref[SUBLANES:SUBLANES + L, :] = main[:, 0:ML_QK]
    qk = cb_ref[...]
    for j in range(ML_CONV):
        off = SUBLANES - (ML_CONV - 1) + j
        qk = qk + scr_ref[off:off + L, :] * cw_ref[j:j + 1, :]
    qk = qk * _sigmoid(qk)
    q_all = qk[:, 0:ML_W]
    k_all = qk[:, ML_W:ML_QK] * (ML_DH ** -0.5)
    v_all = main[:, ML_QK:ML_QK + ML_W]
    og_all = main[:, ML_QK + ML_W:ML_QK + 2 * ML_W]

    valid = (c * L + _iota((L, 1), 0)) < t_real
    pre = sm_ref[...] + bias_ref[...]
    li_all = jnp.where(valid, pre, NEG)
    lf_all = jnp.where(valid, _log_sigmoid(pre), 0.0)
    cum = _dot_l(tri_ref[...], lf_all)
    F = cum + carry_ref[...]
    carry_ref[...] = F[L - 1:L, :]
    lf_o[...] = lf_all
    F_o[...] = F
    FT_o[...] = F.T
    liT = li_all.T
    cumT = cum.T

    row = _iota((L, L), 0)
    col = _iota((L, L), 1)
    causal = col <= row
    for h in range(ML_H):
        hs = slice(h * ML_DH, (h + 1) * ML_DH)
        b_col = cum[:, SM_LF + h:SM_LF + h + 1]
        b_row = cumT[SM_LF + h:SM_LF + h + 1, :]
        li_col = li_all[:, SM_LI + h:SM_LI + h + 1]
        li_row = liT[SM_LI + h:SM_LI + h + 1, :]
        m_prev = m_ref[h][:, 0:1]
        d = jnp.where(causal, b_col - b_row + li_row, NEG)
        inter = b_col + m_prev
        m_t = jnp.maximum(inter, jnp.max(d, axis=-1, keepdims=True))
        w_intra = jnp.exp(d - m_t)
        w_state = jnp.exp(inter - m_t)
        q = q_all[:, hs]
        k = k_all[:, hs]
        v = v_all[:, hs]
        qb, kb = q.astype(bf16), k.astype(bf16)
        s = _dot_nt(qb, kb) * w_intra
        ct = ct_ref[h]
        num = w_state * _dot(qb, ct.astype(bf16)) + _dot(s.astype(bf16), v.astype(bf16))
        n_row = n_ref[h]
        den = w_state * jnp.sum(q * n_row, axis=-1, keepdims=True) + jnp.sum(s, axis=-1, keepdims=True)
        hh = num / jnp.maximum(jnp.abs(den), jnp.exp(-m_t))
        g_end = b_col[L - 1:L, :]
        lw_s = g_end - b_col + li_col
        m_new = jnp.maximum(g_end + m_prev, jnp.max(lw_s, axis=0, keepdims=True))
        w_s = jnp.exp(lw_s - m_new)
        decay = jnp.exp(g_end + m_prev - m_new)
        ct_ref[h] = decay * ct + _dot(kb.T, (w_s * v).astype(bf16))
        n_ref[h] = decay * n_row + jnp.sum(w_s * k, axis=0, keepdims=True)
        m_ref[h] = jnp.broadcast_to(m_new, (1, LANES))
        mu = jnp.mean(hh, axis=-1, keepdims=True)
        hc = hh - mu
        var = jnp.mean(hc * hc, axis=-1, keepdims=True)
        hn = hc * lax.rsqrt(var + ML_GN_EPS) * gn_ref[:, hs]
        o_ref[:, hs] = (_sigmoid(og_all[:, hs]) * hn).astype(o_ref.dtype)

    c_o[...] = ct_ref[...]
    n_o[...] = n_ref[...]
    m_o[...] = m_ref[...]


def mlstm_fox_prep(main_arr, main_blk, small_arr, small_blk, buf8, bias_row, conv_w, conv_b, gn_w, tri,
                   c0t, n0, m0, B, Tp, t_real):
    L = CHUNK
    nc = Tp // L
    hb = L // SUBLANES
    st_c = pl.BlockSpec((None, ML_H, ML_DH, ML_DH), lambda b, c: (b, 0, 0, 0))
    st_n = pl.BlockSpec((None, ML_H, 1, ML_DH), lambda b, c: (b, 0, 0, 0))
    rows = lambda w: pl.BlockSpec((L, w), lambda b, c: (b * nc + c, 0))
    vec = lambda n: pl.BlockSpec((1, n), lambda b, c: (0, 0))
    N = B * Tp
    return pl.pallas_call(
        functools.partial(_mlstm_kernel, t_real=t_real), grid=(B, nc),
        in_specs=[pl.BlockSpec((L, ML_PCOLS), lambda b, c: (b * nc + c, main_blk)),
                  pl.BlockSpec((SUBLANES, ML_QK), lambda b, c: (jnp.maximum((b * nc + c) * hb - 1, 0), 2 * main_blk)),
                  pl.BlockSpec((None, SUBLANES, ML_QK), lambda b, c: (b, 0, 0)),
                  pl.BlockSpec((L, LANES), lambda b, c: (b * nc + c, small_blk)),
                  vec(LANES),
                  pl.BlockSpec((SUBLANES, ML_QK), lambda b, c: (0, 0)),
                  vec(ML_QK), vec(ML_W),
                  pl.BlockSpec((L, L), lambda b, c: (0, 0)),
                  st_c, st_n, st_n],
        out_specs=[rows(ML_W), rows(LANES), rows(LANES),
                   pl.BlockSpec((None, LANES, L), lambda b, c: (b, 0, c)),
                   st_c, st_n, st_n],
        out_shape=[jax.ShapeDtypeStruct((N, ML_W), bf16),
                   jax.ShapeDtypeStruct((N, LANES), f32),
                   jax.ShapeDtypeStruct((N, LANES), f32),
                   jax.ShapeDtypeStruct((B, LANES, Tp), f32),
                   jax.ShapeDtypeStruct((B, ML_H, ML_DH, ML_DH), f32),
                   jax.ShapeDtypeStruct((B, ML_H, 1, ML_DH), f32),
                   jax.ShapeDtypeStruct((B, ML_H, 1, ML_DH), f32)],
        scratch_shapes=[pltpu.VMEM((L + SUBLANES, ML_QK), f32),
                        pltpu.VMEM((ML_H, ML_DH, ML_DH), f32),
                        pltpu.VMEM((ML_H, 1, ML_DH), f32),
                        pltpu.VMEM((ML_H, 1, LANES), f32),
                        pltpu.VMEM((1, LANES), f32)],
        compiler_params=_cparams("parallel", "arbitrary"), name="mlstm")(
            main_arr, main_arr, buf8, small_arr, bias_row, conv_w, conv_b, gn_w, tri, c0t, n0, m0)


FOX_HG = 2


def _fox_attn_kernel(q_ref, k_ref, v_ref, ft_ref, o_ref, m_ref, l_ref, acc_ref, *, tq, tk):
    h0 = pl.program_id(1) * FOX_HG
    qi = pl.program_id(2)
    m_ref[...] = jnp.full_like(m_ref, NEG)
    l_ref[...] = jnp.zeros_like(l_ref)
    acc_ref[...] = jnp.zeros_like(acc_ref)
    qbs = [(q_ref[:, u * FX_DH:(u + 1) * FX_DH] * (FX_DH ** -0.5)).astype(bf16) for u in range(FOX_HG)]

    def block(j, masked):
        k0 = pl.multiple_of(j * tk, tk)
        for u in range(FOX_HG):
            hs = slice(u * FX_DH, (u + 1) * FX_DH)
            kb = k_ref[pl.ds(k0, tk), hs].astype(bf16)
            vb = v_ref[pl.ds(k0, tk), hs].astype(bf16)
            s = _dot_nt(qbs[u], kb) - ft_ref[pl.ds(h0 + u, 1), pl.ds(k0, tk)]
            if masked:
                s = jnp.where(_iota((tq, tk), 1) + k0 <= qi * tq + _iota((tq, tk), 0), s, NEG)
            m = m_ref[u]
            m_new = jnp.maximum(m, _row_max(s))
            alpha = jnp.exp(m - m_new)
            p = jnp.exp(s - m_new)
            l_ref[u] = alpha * l_ref[u] + _row_sum(p)
            acc_ref[:, hs] = alpha * acc_ref[:, hs] + _dot(p.astype(bf16), vb)
            m_ref[u] = m_new

    n_full = (qi * tq) // tk

    def body(j, carry):
        block(j, False)
        return carry

    lax.fori_loop(0, n_full, body, 0)
    for d in range(max(tq // tk, 1)):
        block(n_full + d, True)
    for u in range(FOX_HG):
        hs = slice(u * FX_DH, (u + 1) * FX_DH)
        o_ref[:, hs] = (acc_ref[:, hs] / l_ref[u]).astype(o_ref.dtype)


def fox_attn(P, Kx, Vx, FT, B, T):
    tq = 256
    tk = 512
    nq = T // tq
    wb = FOX_HG * FX_DH
    qb = OFF_FXQ // wb
    return pl.pallas_call(
        functools.partial(_fox_attn_kernel, tq=tq, tk=tk), grid=(B, FX_H // FOX_HG, nq),
        in_specs=[pl.BlockSpec((tq, wb), lambda b, h, i: (b * nq + i, qb + h)),
                  pl.BlockSpec((T, wb), lambda b, h, i: (b, h)),
                  pl.BlockSpec((T, wb), lambda b, h, i: (b, h)),
                  pl.BlockSpec((None, SUBLANES, T), lambda b, h, i: (b, SM_FX // SUBLANES, 0))],
        out_specs=pl.BlockSpec((tq, wb), lambda b, h, i: (b * nq + i, h)),
        out_shape=jax.ShapeDtypeStruct((B * T, FX_W), bf16),
        scratch_shapes=[pltpu.VMEM((FOX_HG, tq, 1), f32), pltpu.VMEM((FOX_HG, tq, 1), f32),
                        pltpu.VMEM((tq, wb), f32)],
        compiler_params=_cparams("parallel", "parallel", "arbitrary"), name="fox_attn")(P, Kx, Vx, FT)


def _fox_suffix_kernel(pt_ref, lf_ref, triu_ref, later_ref, o_ref, x_ref, *, n_pages):
    b = pl.program_id(0)

    def gather(j, carry):
        x_ref[pl.ds(pl.multiple_of(j * FX_H, FX_H), FX_H), :] = lf_ref[pt_ref[b, j]]
        return carry

    lax.fori_loop(0, n_pages, gather, 0)
    x = x_ref[...]
    within = _dot_r(x, triu_ref[...])
    tot = jnp.broadcast_to(jnp.sum(x, axis=-1, keepdims=True), x.shape)
    o_ref[...] = within + _dot_l(later_ref[...], tot)


def fox_suffix(page_table, cache_lft, triu, later, layer):
    B, n_pages = page_table.shape
    n_pool = cache_lft.shape[1]
    rows = n_pages * FX_H
    grid_spec = pltpu.PrefetchScalarGridSpec(
        num_scalar_prefetch=1, grid=(B,),
        in_specs=[pl.BlockSpec((None, n_pool, FX_H, PAGE_SIZE), lambda b, pt: (layer, 0, 0, 0)),
                  pl.BlockSpec((PAGE_SIZE, PAGE_SIZE), lambda b, pt: (0, 0)),
                  pl.BlockSpec((rows, rows), lambda b, pt: (0, 0))],
        out_specs=pl.BlockSpec((None, rows, PAGE_SIZE), lambda b, pt: (b, 0, 0)),
        scratch_shapes=[pltpu.VMEM((rows, PAGE_SIZE), f32)])
    return pl.pallas_call(
        functools.partial(_fox_suffix_kernel, n_pages=n_pages), grid_spec=grid_spec,
        out_shape=jax.ShapeDtypeStruct((B, rows, PAGE_SIZE), f32),
        compiler_params=_cparams("parallel"), name="fox_suffix")(page_table, cache_lft, triu, later)


DEC_PP = 8


def _fox_decode_kernel(pt_ref, q_ref, *refs, n_steps, pp, t_new):
    kc = refs[0:pp]
    vc = refs[pp:2 * pp]
    suf_ref, hmask_ref, kn_ref, vn_ref, fkn_ref, o_ref, m_ref, l_ref, acc_ref = refs[2 * pp:]
    j = pl.program_id(1)
    nq = FX_H * t_new
    cols = PAGE_SIZE * FX_H

    @pl.when(j == 0)
    def _():
        m_ref[...] = jnp.full_like(m_ref, NEG)
        l_ref[...] = jnp.zeros_like(l_ref)
        acc_ref[...] = jnp.zeros_like(acc_ref)

    qb = (q_ref[...] * (FX_DH ** -0.5)).astype(bf16)

    def update(ss, vs):
        m = m_ref[...]
        m_new = m
        for s in ss:
            m_new = jnp.maximum(m_new, _row_max(s))
        alpha = jnp.exp(m - m_new)
        l = alpha * l_ref[...]
        acc = alpha * acc_ref[...]
        for s, vb in zip(ss, vs):
            p = jnp.exp(s - m_new)
            l = l + _row_sum(p)
            acc = acc + _dot(p.astype(bf16), vb)
        l_ref[...] = l
        acc_ref[...] = acc
        m_ref[...] = m_new

    @pl.when(j < n_steps)
    def _():
        hmask = hmask_ref[...]
        ss = [_dot_nt(qb, kc[i][...].reshape(cols, FX_DH).astype(bf16)) + hmask + suf_ref[i] for i in range(pp)]
        update(ss, [vc[i][...].reshape(cols, FX_DH).astype(bf16) for i in range(pp)])

    @pl.when(j == n_steps)
    def _():
        s = _dot_nt(qb, kn_ref[...].astype(bf16)) - fkn_ref[...]
        col = _iota((nq, LANES), 1)
        row = _iota((nq, LANES), 0)
        ok = (col % FX_H == row // t_new) & (col // FX_H <= row % t_new)
        update([jnp.where(ok, s, NEG)], [vn_ref[...].astype(bf16)])
        o_ref[...] = acc_ref[...] / l_ref[...]


def fox_decode(page_table, q2, cache_k, cache_v, suffix, hmask, k_new, v_new, fkn, layer, t_new):
    B, n_pages = page_table.shape
    pp = DEC_PP
    n_steps = n_pages // pp
    nq = FX_H * t_new
    cols = PAGE_SIZE * FX_H

    def page_spec(i):
        return pl.BlockSpec((None, None, PAGE_SIZE, FX_H, FX_DH),
                            lambda b, j, pt: (layer, pt[b, jnp.minimum(j, n_steps - 1) * pp + i], 0, 0, 0))

    per_b = lambda shape: pl.BlockSpec((None,) + shape, lambda b, j, pt: (b, 0, 0))
    grid_spec = pltpu.PrefetchScalarGridSpec(
        num_scalar_prefetch=1, grid=(B, n_steps + 1),
        in_specs=[per_b((nq, FX_DH))] + [page_spec(i) for i in range(pp)] * 2
        + [pl.BlockSpec((None, pp, 1, cols), lambda b, j, pt: (b, jnp.minimum(j, n_steps - 1), 0, 0)),
           pl.BlockSpec((nq, cols), lambda b, j, pt: (0, 0)),
           per_b((LANES, FX_DH)), per_b((LANES, FX_DH)), per_b((1, LANES))],
        out_specs=per_b((nq, FX_DH)),
        scratch_shapes=[pltpu.VMEM((nq, 1), f32), pltpu.VMEM((nq, 1), f32), pltpu.VMEM((nq, FX_DH), f32)])
    return pl.pallas_call(
        functools.partial(_fox_decode_kernel, n_steps=n_steps, pp=pp, t_new=t_new), grid_spec=grid_spec,
        out_shape=jax.ShapeDtypeStruct((B, nq, FX_DH), f32),
        compiler_params=_cparams("parallel", "arbitrary"), name="fox_decode")(
            page_table, q2, *([cache_k] * pp), *([cache_v] * pp), suffix, hmask, k_new, v_new, fkn)


def _pad_cols(w, n):
    return jnp.pad(w, ((0, 0), (0, n - w.shape[1])))


def _rw_cols(x):
    W = RW_W
    pad = [(0, 0)] * (x.ndim - 1) + [(0, RW_LORA_PAD - R_W)]
    return jnp.concatenate([x[..., 0:3 * W], jnp.pad(x[..., 3 * W:3 * W + R_W], pad),
                            jnp.pad(x[..., 3 * W + R_W:3 * W + R_W + R_A], pad),
                            x[..., 3 * W + R_W + R_A:]], axis=-1)


def _rw_cols_inv(x):
    W = RW_W
    o = 3 * W
    return jnp.concatenate([x[..., 0:o], x[..., o:o + R_W], x[..., o + RW_LORA_PAD:o + RW_LORA_PAD + R_A],
                            x[..., o + 2 * RW_LORA_PAD:]], axis=-1)


def _w_in_split(w):
    o_ml = RW_COLS
    o_fx = o_ml + ML_COLS
    o_gt = o_fx + FX_COLS
    ml = w[:, o_ml:o_fx]
    fx = w[:, o_fx:o_gt]
    ml_main = jnp.concatenate([ml[:, 0:ML_QK + ML_W], ml[:, ML_QK + ML_W + 2 * ML_H:]], axis=1)
    small = _pad_cols(jnp.concatenate([ml[:, ML_QK + ML_W:ML_QK + ML_W + 2 * ML_H], fx[:, 3 * FX_W:]], axis=1), LANES)
    main = jnp.concatenate([w[:, o_gt:], _rw_cols(w[:, 0:RW_COLS]), ml_main, fx[:, 0:FX_W]], axis=1)
    return (main.astype(bf16), fx[:, FX_W:2 * FX_W].astype(bf16), fx[:, 2 * FX_W:3 * FX_W].astype(bf16),
            small.astype(bf16))


def _pad_rows(w, n):
    return jnp.pad(w, ((0, n - w.shape[0]), (0, 0)))


def _prev_rows8(buf, width):
    B, r, C = buf.shape
    return jnp.pad(buf, ((0, 0), (SUBLANES - r, 0), (0, width - C)))


def _layer(x, mod, st, past, W, page_table, l, B, T):
    N = B * T
    shift1, scale1, gate1, shift2, scale2, gate2 = jnp.split(mod, N_MOD, axis=-1)
    u = norm_mod(x, W['norm_pre_mix'], scale1, shift1, T)
    P = matmul(u, W['w_in'], tn=1024)
    Kx = matmul(u, W['w_in_k'], tn=FX_W)
    Vx = matmul(u, W['w_in_v'], tn=FX_W)
    SM = matmul(u, W['w_in_small'], tn=LANES)

    padded = T % CHUNK != 0
    Tp = T if not padded else CHUNK

    def pad_t(a):
        if not padded:
            return a
        return jnp.pad(a.reshape(B, T, -1), ((0, 0), (0, Tp - T), (0, 0))).reshape(B * Tp, -1)

    def unpad_t(a):
        if not padded:
            return a
        return a.reshape(B, Tp, -1)[:, :T].reshape(N, -1)

    sp8 = _prev_rows8(_rw_cols(st['rw_shift'])[:, None, :], RW_PCOLS)
    prep = rwkv_prep(P, OFF_RW // RW_PCOLS, sp8, W['rw_mu'], W['rw_w0'], W['rw_a0'], W['rw_k_k'], W['rw_k_a'],
                     W['rw_r_k'], W['rw_w_up'], W['rw_a_up'], W['rw_g_up'], W['ones_blk'], T)
    nkk, wr, w_, b_, k2, v_rw, br, kr, rkr, g_rw = prep
    s0 = st['rw_wkv'].reshape(B, N_PAIR, 2, RW_DH, RW_DH).transpose(0, 1, 3, 2, 4).reshape(B, N_PAIR, RW_DH, LANES)
    seq = [pad_t(a).reshape(B, Tp, RW_W) for a in (nkk, wr, w_, b_, k2, v_rw)]
    z, sa, s_out = rwkv_scan(*seq, s0, W['ones_blk'][0:LANES, 0:LANES], T)
    z, sa = unpad_t(z.reshape(B * Tp, RW_W)), unpad_t(sa.reshape(B * Tp, RW_W))
    o_rw = rwkv_post(z, sa, v_rw, br, kr, rkr, g_rw, W['rw_ln_w'], W['rw_ln_b'], W['ones_blk'])
    rw_wkv = s_out.reshape(B, N_PAIR, RW_DH, 2, RW_DH).transpose(0, 1, 3, 2, 4).reshape(B, RW_H, RW_DH, RW_DH)
    rw_shift = _rw_cols_inv(P[:, OFF_RW:OFF_RW + RW_PCOLS].reshape(B, T, RW_PCOLS)[:, -1])

    buf8 = _prev_rows8(st['ml_conv'], ML_QK)
    c0t = jnp.swapaxes(st['ml_c'], -1, -2)
    n0 = st['ml_n'][:, :, None, :]
    m0 = jnp.broadcast_to(st['ml_m'][:, :, None, None], (B, ML_H, 1, LANES))
    if padded:
        main_arr, main_blk = pad_t(P[:, OFF_ML:OFF_ML + ML_PCOLS]), 0
        small_arr, small_blk = pad_t(SM), 0
    else:
        main_arr, main_blk, small_arr, small_blk = P, OFF_ML // ML_PCOLS, SM, 0
    o_ml, lf_all, F, FT, ct, n_new, m_new = mlstm_fox_prep(
        main_arr, main_blk, small_arr, small_blk, buf8, W['small_bias'], W['ml_conv_w'], W['ml_conv_b'],
        W['ml_gn_w'], W['tri'], c0t, n0, m0, B, Tp, T)
    o_ml = unpad_t(o_ml)
    ml_c = jnp.swapaxes(ct, -1, -2)
    ml_n = n_new[:, :, 0, :]
    ml_m = m_new[:, :, 0, 0]
    ml_conv = P[:, OFF_ML:OFF_ML + ML_QK].reshape(B, T, ML_QK)[:, T - (ML_CONV - 1):]
    fox_logf = unpad_t(lf_all)[:, SM_FX:SM_FX + FX_H].reshape(B, T, FX_H)
    fox_k = Kx.reshape(B, T, FX_H, FX_DH)
    fox_v = Vx.reshape(B, T, FX_H, FX_DH)

    if past is None:
        o_fx = fox_attn(P, Kx, Vx, FT, B, T)
    else:
        cache_k, cache_v, cache_lft = past
        n_pages = page_table.shape[1]
        nq = FX_H * T
        assert nq <= LANES and n_pages % DEC_PP == 0
        suffix = fox_suffix(page_table, cache_lft, W['triu'], W['later'], l)
        suffix = suffix.reshape(B, n_pages, FX_H, PAGE_SIZE).transpose(0, 1, 3, 2).reshape(
            B, n_pages, 1, PAGE_SIZE * FX_H)
        q2 = P[:, OFF_FXQ:OFF_FXQ + FX_W].reshape(B, T, FX_H, FX_DH).transpose(0, 2, 1, 3).reshape(B, nq, FX_DH)
        padk = lambda a: jnp.pad(a.reshape(B, nq, FX_DH), ((0, 0), (0, LANES - nq), (0, 0)))
        fkn = F.reshape(B, Tp, LANES)[:, :T, SM_FX:SM_FX + FX_H].reshape(B, 1, nq)
        fkn = jnp.pad(fkn, ((0, 0), (0, 0), (0, LANES - nq)))
        hmask = jnp.where(jnp.arange(PAGE_SIZE * FX_H)[None, :] % FX_H == jnp.arange(nq)[:, None] // T, 0.0, NEG)
        o = fox_decode(page_table, q2, cache_k, cache_v, suffix, hmask.astype(f32), padk(fox_k), padk(fox_v),
                       fkn, l, T)
        o_fx = o.reshape(B, FX_H, T, FX_DH).transpose(0, 2, 1, 3).reshape(N, FX_W)

    merged = merge(o_rw, o_ml, o_fx, W['w_br_rwkv'], W['w_br_mlstm'], W['w_br_fox'], P)
    x = mm_norm_res(merged, W['w_out'], x, W['norm_post_mix'], gate1, T, D_MODEL)

    zf = norm_mod(x, W['norm_pre_ffn'], scale2, shift2, T)
    fbuf8 = _prev_rows8(st['ffn_conv'], D_FF_PAD)
    if T % BF16_ROWS == 0:
        hmid, tails = ffn_up(zf, W['ffn_w_gv'], fbuf8, W['ffn_conv_w'], W['ffn_conv_b'], T)
        per = N // B // min(T, 1024)
        ffn_conv = tails.reshape(B, per, SUBLANES, D_FF_PAD)[:, -1, SUBLANES - (FFN_CONV - 1):, 0:D_FF]
    else:
        av = matmul(zf, W['ffn_w_gv'], tn=1024)
        hmid = ffn_act(av, fbuf8, W['ffn_conv_w'], W['ffn_conv_b'], T)
        ffn_conv = av[:, 0:D_FF].reshape(B, T, D_FF)[:, T - (FFN_CONV - 1):]
    x = mm_norm_res(hmid, W['ffn_w_down'], x, W['norm_post_ffn'], gate2, T, FF_TK)

    new = dict(fox_k=fox_k, fox_v=fox_v, fox_logf=fox_logf, rw_shift=rw_shift, rw_wkv=rw_wkv,
               ml_conv=ml_conv, ml_c=ml_c, ml_n=ml_n, ml_m=ml_m, ffn_conv=ffn_conv)
    return x, new


STATE_NAMES = ("fox_k", "fox_v", "fox_logf", "rw_shift", "rw_wkv", "ml_conv", "ml_c", "ml_n", "ml_m", "ffn_conv")


def _layer_weights(Pm, l, n_pages):
    row = lambda v: v.reshape(1, -1)
    W = {}
    for name in ('norm_pre_mix', 'norm_post_mix', 'norm_pre_ffn', 'norm_post_ffn'):
        W[name] = Pm[name][l]
    W['w_in'], W['w_in_k'], W['w_in_v'], W['w_in_small'] = _w_in_split(Pm['w_in'][l])
    W['rw_mu'] = row(_rw_cols(Pm['rw_mu'][l]))
    for name in ('rw_w0', 'rw_a0', 'rw_k_k', 'rw_k_a', 'rw_r_k', 'rw_ln_w', 'rw_ln_b'):
        W[name] = row(Pm[name][l])
    W['rw_w_up'] = _pad_rows(Pm['rw_w_up'][l], RW_LORA_PAD).astype(bf16)
    W['rw_a_up'] = _pad_rows(Pm['rw_a_up'][l], RW_LORA_PAD).astype(bf16)
    W['rw_g_up'] = Pm['rw_g_up'][l].astype(bf16)
    hid = jnp.arange(RW_W) // RW_DH
    W['ones_blk'] = (hid[:, None] == hid[None, :]).astype(bf16)
    W['small_bias'] = row(jnp.pad(jnp.concatenate([Pm['ml_b_i'][l], Pm['ml_b_f'][l], Pm['fx_b_f'][l]]),
                                  (0, LANES - 2 * ML_H - FX_H)))
    W['ml_conv_w'] = _pad_rows(Pm['ml_conv_w'][l], SUBLANES)
    W['ml_conv_b'] = row(Pm['ml_conv_b'][l])
    W['ml_gn_w'] = row(Pm['ml_gn_w'][l])
    idx = jnp.arange(CHUNK)
    W['tri'] = (idx[None, :] <= idx[:, None]).astype(bf16)
    W['triu'] = (idx[:, None] > idx[None, :]).astype(bf16)
    if n_pages:
        r = jnp.arange(n_pages * FX_H)
        W['later'] = ((r[:, None] % FX_H == r[None, :] % FX_H) & (r[None, :] // FX_H > r[:, None] // FX_H)).astype(bf16)
    W['w_br_rwkv'] = Pm['w_br_rwkv'][l].astype(bf16)
    W['w_br_mlstm'] = Pm['w_br_mlstm'][l].astype(bf16)
    W['w_br_fox'] = Pm['w_br_fox'][l].astype(bf16)
    W['w_out'] = Pm['w_out'][l].astype(bf16)
    W['ffn_w_gv'] = jnp.concatenate([_pad_cols(Pm['ffn_w_gate'][l], D_FF_PAD),
                                     _pad_cols(Pm['ffn_w_val'][l], D_FF_PAD)], axis=1).astype(bf16)
    W['ffn_conv_w'] = _pad_rows(_pad_cols(Pm['ffn_conv_w'][l], D_FF_PAD), SUBLANES)
    W['ffn_conv_b'] = _pad_cols(row(Pm['ffn_conv_b'][l]), D_FF_PAD)
    W['ffn_w_down'] = _pad_rows(Pm['ffn_w_down'][l], D_FF_PAD).astype(bf16)
    return W


def kernel(x_prompt, x_sample, cache_fox_k, cache_fox_v, cache_fox_logf, state_rwkv_shift, state_rwkv_wkv,
           state_mlstm_conv, state_mlstm_c, state_mlstm_n, state_mlstm_m, state_ffn_conv, page_table,
           c_prompt, c_sample, w_ada, b_ada, norm_pre_mix, norm_post_mix, norm_pre_ffn, norm_post_ffn, w_in,
           rw_mu, rw_w0, rw_w_up, rw_a0, rw_a_up, rw_g_up, rw_k_k, rw_k_a, rw_r_k, rw_ln_w, rw_ln_b,
           ml_conv_w, ml_conv_b, ml_b_i, ml_b_f, ml_gn_w, fx_b_f, w_br_rwkv, w_br_mlstm, w_br_fox, w_out,
           ffn_w_gate, ffn_w_val, ffn_conv_w, ffn_conv_b, ffn_w_down):
    Pm = dict(norm_pre_mix=norm_pre_mix, norm_post_mix=norm_post_mix, norm_pre_ffn=norm_pre_ffn,
              norm_post_ffn=norm_post_ffn, w_in=w_in, rw_mu=rw_mu, rw_w0=rw_w0, rw_w_up=rw_w_up, rw_a0=rw_a0,
              rw_a_up=rw_a_up, rw_g_up=rw_g_up, rw_k_k=rw_k_k, rw_k_a=rw_k_a,
              rw_r_k=rw_r_k.reshape(DEPTH, RW_W), rw_ln_w=rw_ln_w, rw_ln_b=rw_ln_b, ml_conv_w=ml_conv_w,
              ml_conv_b=ml_conv_b, ml_b_i=ml_b_i, ml_b_f=ml_b_f, ml_gn_w=ml_gn_w, fx_b_f=fx_b_f,
              w_br_rwkv=w_br_rwkv, w_br_mlstm=w_br_mlstm, w_br_fox=w_br_fox, w_out=w_out, ffn_w_gate=ffn_w_gate,
              ffn_w_val=ffn_w_val, ffn_conv_w=ffn_conv_w, ffn_conv_b=ffn_conv_b, ffn_w_down=ffn_w_down)
    Bp, Tpr, D = x_prompt.shape
    Bs, Ts, _ = x_sample.shape
    cache_k, cache_v = cache_fox_k, cache_fox_v
    cache_lft = jnp.swapaxes(cache_fox_logf, -1, -2)

    zeros = lambda *s: jnp.zeros(s, f32)
    xp = x_prompt.reshape(Bp * Tpr, D)
    xs = x_sample.reshape(Bs * Ts, D)
    c_all = jnp.pad(jnp.concatenate([c_prompt, c_sample], axis=0), ((0, 16 - Bp - Bs), (0, 0)))
    new_p = {n: [] for n in STATE_NAMES}
    new_s = {n: [] for n in STATE_NAMES}
    for l in range(DEPTH):
        W = _layer_weights(Pm, l, page_table.shape[1])
        mod = ada_mod(c_all, w_ada[l].astype(bf16), b_ada[l].reshape(1, -1))
        st_p = dict(rw_shift=zeros(Bp, RW_COLS), rw_wkv=zeros(Bp, RW_H, RW_DH, RW_DH),
                    ml_conv=zeros(Bp, ML_CONV - 1, ML_QK), ml_c=zeros(Bp, ML_H, ML_DH, ML_DH),
                    ml_n=zeros(Bp, ML_H, ML_DH), ml_m=zeros(Bp, ML_H), ffn_conv=zeros(Bp, FFN_CONV - 1, D_FF))
        st_s = dict(rw_shift=state_rwkv_shift[l], rw_wkv=state_rwkv_wkv[l], ml_conv=state_mlstm_conv[l],
                    ml_c=state_mlstm_c[l], ml_n=state_mlstm_n[l], ml_m=state_mlstm_m[l],
                    ffn_conv=state_ffn_conv[l])
        xp, lp = _layer(xp, mod[:Bp], st_p, None, W, None, l, Bp, Tpr)
        xs, ls = _layer(xs, mod[Bp:Bp + Bs], st_s, (cache_k, cache_v, cache_lft), W, page_table, l, Bs, Ts)
        for n in STATE_NAMES:
            new_p[n].append(lp[n])
            new_s[n].append(ls[n])
    sp = {n: jnp.stack(v) for n, v in new_p.items()}
    ss = {n: jnp.stack(v) for n, v in new_s.items()}
    return (xp.reshape(Bp, Tpr, D), xs.reshape(Bs, Ts, D),
            sp['fox_k'], ss['fox_k'], sp['fox_v'], ss['fox_v'], sp['fox_logf'], ss['fox_logf'],
            sp['rw_shift'], ss['rw_shift'], sp['rw_wkv'], ss['rw_wkv'],
            sp['ml_conv'], ss['ml_conv'], sp['ml_c'], ss['ml_c'], sp['ml_n'], ss['ml_n'], sp['ml_m'], ss['ml_m'],
            sp['ffn_conv'], ss['ffn_conv'])
```

```python
import functools
import math

import jax
import jax.numpy as jnp
from jax import lax
from jax.experimental import pallas as pl
from jax.experimental.pallas import tpu as pltpu

f32 = jnp.float32
bf16 = jnp.bfloat16

D_MODEL = 2048
DEPTH = 2
PAGE_SIZE = 128
RW_W = D_MODEL // 4
RW_DH = 64
RW_H = RW_W // RW_DH
R_W = max(32, int(round(1.8 * D_MODEL ** 0.5 / 32)) * 32)
R_A = R_W
R_G = max(32, int(round(0.6 * D_MODEL ** 0.8 / 32)) * 32)
RW_COLS = 3 * RW_W + R_W + R_A + R_G
RW_LN_EPS = 64e-5
RW_DECAY_CLAMP = 0.5
ML_W = D_MODEL // 4
ML_H = 4
ML_DH = ML_W // ML_H
ML_QK = 2 * ML_W
ML_CONV = 4
ML_COLS = ML_QK + ML_W + 2 * ML_H + ML_W
ML_GN_EPS = 1e-5
FX_W = D_MODEL // 2
FX_DH = 128
FX_H = FX_W // FX_DH
FX_COLS = 3 * FX_W + FX_H
N_BRANCH = 3
GATE_COLS = N_BRANCH * D_MODEL
D_FF = ((8 * D_MODEL // 3 + 127) // 128) * 128
FFN_CONV = 3
N_MOD = 6
NORM_EPS = 1e-6

LANES = 128
SUBLANES = 8
VMEM_LIMIT = 56 * 1024 * 1024
CHUNK = 128
NEG = -1e30

RW_LORA_PAD = LANES
RW_PCOLS = 3 * RW_W + 2 * RW_LORA_PAD + R_G
OFF_GATE = 0
OFF_RW = OFF_GATE + GATE_COLS
OFF_ML = OFF_RW + RW_PCOLS
ML_PCOLS = ML_QK + 2 * ML_W
OFF_FXQ = OFF_ML + ML_PCOLS
P_COLS = OFF_FXQ + FX_W
SM_LI, SM_LF, SM_FX = 0, ML_H, 2 * ML_H
D_FF_PAD = 5632
FF_TK = 1408


def _cparams(*sem):
    return pltpu.CompilerParams(dimension_semantics=sem, vmem_limit_bytes=VMEM_LIMIT)


def _split3(x):
    hi = x.astype(bf16)
    r = x - hi.astype(f32)
    mid = r.astype(bf16)
    lo = (r - mid.astype(f32)).astype(bf16)
    return hi, mid, lo


def _dot(a, b):
    return jnp.dot(a, b, preferred_element_type=f32)


def _dot_nt(a, b):
    return lax.dot_general(a, b, (((1,), (1,)), ((), ())), preferred_element_type=f32)


def _dot_r(x, m):
    hi, mid, lo = _split3(x)
    return _dot(hi, m) + _dot(mid, m) + _dot(lo, m)


def _dot_r2(x, m):
    hi = x.astype(bf16)
    lo = (x - hi.astype(f32)).astype(bf16)
    return _dot(hi, m) + _dot(lo, m)


def _dot_l(m, x):
    hi, mid, lo = _split3(x)
    return _dot(m, hi) + _dot(m, mid) + _dot(m, lo)


def _sigmoid(x):
    return 0.5 * (jnp.tanh(0.5 * x) + 1.0)


def _log_sigmoid(x):
    return jnp.minimum(x, 0.0) - jnp.log(1.0 + jnp.exp(-jnp.abs(x)))


def _softplus(x):
    return jnp.maximum(x, 0.0) + jnp.log(1.0 + jnp.exp(-jnp.abs(x)))


def _iota(shape, dim):
    return lax.broadcasted_iota(jnp.int32, shape, dim)


def _lane_tiles(x):
    return [x[:, i:i + LANES] for i in range(0, x.shape[-1], LANES)]


def _row_max(x):
    return jnp.max(functools.reduce(jnp.maximum, _lane_tiles(x)), axis=-1, keepdims=True)


def _row_sum(x):
    return jnp.sum(functools.reduce(jnp.add, _lane_tiles(x)), axis=-1, keepdims=True)


def _softmax_step(scores, m_old, l_old):
    tiles = [_lane_tiles(s) for s in scores]
    flat = [t for ts in tiles for t in ts]
    m_new = jnp.maximum(m_old, jnp.max(functools.reduce(jnp.maximum, flat), axis=-1, keepdims=True))
    alpha = jnp.exp(m_old - m_new)
    p_tiles = [[jnp.exp(t - m_new) for t in ts] for ts in tiles]
    total = functools.reduce(jnp.add, [t for ts in p_tiles for t in ts])
    l_new = alpha * l_old + jnp.sum(total, axis=-1, keepdims=True)
    ps = [jnp.concatenate([t.astype(bf16) for t in ts], axis=1) for ts in p_tiles]
    return m_new, l_new, alpha, ps


def _group_vec(v, T, tm):
    B, D = v.shape
    if tm <= T:
        per = T // tm
        return v[:, None, :], (None, 1, D), lambda i, *_: (i // per, 0, 0)
    rows = jnp.repeat(v, T, axis=0)
    return rows.reshape(-1, tm, D), (None, tm, D), lambda i, *_: (i, 0, 0)


def _row_tile(n_rows, T, want):
    return want if T >= want else n_rows


def _ada_kernel(c_ref, w_ref, b_ref, o_ref):
    c = c_ref[...]
    a = (c * _sigmoid(c)).astype(bf16)
    o_ref[...] = _dot(a, w_ref[...]) + b_ref[...]


def ada_mod(c, w, b):
    M, K = c.shape
    N = w.shape[1]
    tn = 1024
    return pl.pallas_call(
        _ada_kernel, grid=(N // tn,),
        in_specs=[pl.BlockSpec((M, K), lambda j: (0, 0)),
                  pl.BlockSpec((K, tn), lambda j: (0, j)),
                  pl.BlockSpec((1, tn), lambda j: (0, j))],
        out_specs=pl.BlockSpec((M, tn), lambda j: (0, j)),
        out_shape=jax.ShapeDtypeStruct((M, N), f32),
        compiler_params=_cparams("parallel"), name="ada_mod")(c, w, b)


def _norm_mod_kernel(x_ref, g_ref, sc_ref, sh_ref, o_ref):
    x = x_ref[...]
    ms = jnp.mean(x * x, axis=-1, keepdims=True)
    y = x * lax.rsqrt(ms + NORM_EPS) * g_ref[...]
    o_ref[...] = (y * (1.0 + sc_ref[...]) + sh_ref[...]).astype(o_ref.dtype)


def norm_mod(x, g, scale, shift, T):
    N, D = x.shape
    tm = _row_tile(N, T, 512)
    sc, sc_blk, sc_map = _group_vec(scale, T, tm)
    sh, _, _ = _group_vec(shift, T, tm)
    return pl.pallas_call(
        _norm_mod_kernel, grid=(N // tm,),
        in_specs=[pl.BlockSpec((tm, D), lambda i: (i, 0)),
                  pl.BlockSpec((1, D), lambda i: (0, 0)),
                  pl.BlockSpec(sc_blk, sc_map), pl.BlockSpec(sc_blk, sc_map)],
        out_specs=pl.BlockSpec((tm, D), lambda i: (i, 0)),
        out_shape=jax.ShapeDtypeStruct((N, D), bf16),
        compiler_params=_cparams("parallel"), name="norm_mod")(x, g.reshape(1, D), sc, sh)


def _mm_kernel(a_ref, w_ref, o_ref):
    o_ref[...] = _dot(a_ref[...], w_ref[...]).astype(o_ref.dtype)


def matmul(a, w, tn, out_dtype=f32):
    M, K = a.shape
    N = w.shape[1]
    tm = min(M, 1024)
    return pl.pallas_call(
        _mm_kernel, grid=(M // tm, N // tn),
        in_specs=[pl.BlockSpec((tm, K), lambda i, j: (i, 0)),
                  pl.BlockSpec((K, tn), lambda i, j: (0, j))],
        out_specs=pl.BlockSpec((tm, tn), lambda i, j: (i, j)),
        out_shape=jax.ShapeDtypeStruct((M, N), out_dtype),
        compiler_params=_cparams("parallel", "arbitrary"), name="matmul")(a, w)


def _mm_norm_res_kernel(a_ref, w_ref, x_ref, g_ref, gate_ref, o_ref, acc_ref, *, nk):
    k = pl.program_id(1)
    part = _dot(a_ref[...].astype(bf16), w_ref[...])

    @pl.when(k == 0)
    def _():
        acc_ref[...] = part

    @pl.when(k > 0)
    def _():
        acc_ref[...] += part

    @pl.when(k == nk - 1)
    def _():
        f = acc_ref[...]
        ms = jnp.mean(f * f, axis=-1, keepdims=True)
        y = f * lax.rsqrt(ms + NORM_EPS) * g_ref[...]
        o_ref[...] = x_ref[...] + gate_ref[...] * y


def mm_norm_res(a, w, x, g, gate, T, tk):
    M, K = a.shape
    D = w.shape[1]
    tm = _row_tile(M, T, 512)
    nk = K // tk
    gt, gt_blk, gt_map = _group_vec(gate, T, tm)
    return pl.pallas_call(
        functools.partial(_mm_norm_res_kernel, nk=nk), grid=(M // tm, nk),
        in_specs=[pl.BlockSpec((tm, tk), lambda i, k: (i, k)),
                  pl.BlockSpec((tk, D), lambda i, k: (k, 0)),
                  pl.BlockSpec((tm, D), lambda i, k: (i, 0)),
                  pl.BlockSpec((1, D), lambda i, k: (0, 0)),
                  pl.BlockSpec(gt_blk, gt_map)],
        out_specs=pl.BlockSpec((tm, D), lambda i, k: (i, 0)),
        out_shape=jax.ShapeDtypeStruct((M, D), f32),
        scratch_shapes=[pltpu.VMEM((tm, D), f32)],
        compiler_params=_cparams("parallel", "arbitrary"), name="mm_norm_res")(
            a, w, x, g.reshape(1, D), gt)


def _merge_kernel(orw_ref, oml_ref, ofx_ref, wrw_ref, wml_ref, wfx_ref, g0_ref, g1_ref, g2_ref, o_ref):
    m = _sigmoid(g0_ref[...]) * _dot(orw_ref[...].astype(bf16), wrw_ref[...])
    m += _sigmoid(g1_ref[...]) * _dot(oml_ref[...].astype(bf16), wml_ref[...])
    m += _sigmoid(g2_ref[...]) * _dot(ofx_ref[...].astype(bf16), wfx_ref[...])
    o_ref[...] = m.astype(o_ref.dtype)


def merge(o_rw, o_ml, o_fx, w_rw, w_ml, w_fx, P):
    N = o_rw.shape[0]
    D = D_MODEL
    tm = min(N, 512)
    tn = 512
    nb = D // tn
    gspec = lambda b: pl.BlockSpec((tm, tn), lambda i, j: (i, OFF_GATE // tn + b * nb + j))
    return pl.pallas_call(
        _merge_kernel, grid=(N // tm, nb),
        in_specs=[pl.BlockSpec((tm, RW_W), lambda i, j: (i, 0)),
                  pl.BlockSpec((tm, ML_W), lambda i, j: (i, 0)),
                  pl.BlockSpec((tm, FX_W), lambda i, j: (i, 0)),
                  pl.BlockSpec((RW_W, tn), lambda i, j: (0, j)),
                  pl.BlockSpec((ML_W, tn), lambda i, j: (0, j)),
                  pl.BlockSpec((FX_W, tn), lambda i, j: (0, j)),
                  gspec(0), gspec(1), gspec(2)],
        out_specs=pl.BlockSpec((tm, tn), lambda i, j: (i, j)),
        out_shape=jax.ShapeDtypeStruct((N, D), bf16),
        compiler_params=_cparams("parallel", "arbitrary"), name="merge")(
            o_rw, o_ml, o_fx, w_rw, w_ml, w_fx, P, P, P)


def _ffn_act_kernel(a_ref, halo_ref, buf_ref, val_ref, cw_ref, cb_ref, o_ref, scr_ref, *, per, tm):
    first = (pl.program_id(0) % per) == 0
    scr_ref[0:SUBLANES, :] = jnp.where(first, buf_ref[...], halo_ref[...])
    scr_ref[SUBLANES:SUBLANES + tm, :] = a_ref[...]
    y = cb_ref[...]
    for j in range(FFN_CONV):
        off = SUBLANES - (FFN_CONV - 1) + j
        y = y + scr_ref[off:off + tm, :] * cw_ref[j:j + 1, :]
    c0 = math.sqrt(2.0 / math.pi)
    gelu = 0.5 * y * (1.0 + jnp.tanh(c0 * (y + 0.044715 * (y * y * y))))
    o_ref[...] = (gelu * val_ref[...]).astype(o_ref.dtype)


def ffn_act(av, buf8, conv_w, conv_b, T):
    N = av.shape[0]
    tm = min(T, 512)
    per = T // tm
    tn = FF_TK
    nj = D_FF_PAD // tn
    hb = tm // SUBLANES
    out_dtype = bf16 if tm % 16 == 0 else f32
    return pl.pallas_call(
        functools.partial(_ffn_act_kernel, per=per, tm=tm), grid=(N // tm, nj),
        in_specs=[pl.BlockSpec((tm, tn), lambda i, j: (i, j)),
                  pl.BlockSpec((SUBLANES, tn), lambda i, j: (jnp.maximum(i * hb - 1, 0), j)),
                  pl.BlockSpec((None, SUBLANES, tn), lambda i, j: (i // per, 0, j)),
                  pl.BlockSpec((tm, tn), lambda i, j: (i, nj + j)),
                  pl.BlockSpec((SUBLANES, tn), lambda i, j: (0, j)),
                  pl.BlockSpec((1, tn), lambda i, j: (0, j))],
        out_specs=pl.BlockSpec((tm, tn), lambda i, j: (i, j)),
        out_shape=jax.ShapeDtypeStruct((N, D_FF_PAD), out_dtype),
        scratch_shapes=[pltpu.VMEM((tm + SUBLANES, tn), f32)],
        compiler_params=_cparams("parallel", "arbitrary"), name="ffn_act")(
            av, av, buf8, av, conv_w, conv_b)


BF16_ROWS = 16


def _ffn_up_kernel(z_ref, zh_ref, wg_ref, wv_ref, buf_ref, cw_ref, cb_ref, h_ref, tail_ref, scr_ref, *, per, tm):
    first = (pl.program_id(0) % per) == 0
    wg = wg_ref[...]
    a = _dot(z_ref[...], wg)
    val = _dot(z_ref[...], wv_ref[...])
    a_prev = _dot(zh_ref[...], wg)[BF16_ROWS - SUBLANES:, :]
    scr_ref[0:SUBLANES, :] = jnp.where(first, buf_ref[...], a_prev)
    scr_ref[SUBLANES:SUBLANES + tm, :] = a
    y = cb_ref[...]
    for j in range(FFN_CONV):
        off = SUBLANES - (FFN_CONV - 1) + j
        y = y + scr_ref[off:off + tm, :] * cw_ref[j:j + 1, :]
    c0 = math.sqrt(2.0 / math.pi)
    gelu = 0.5 * y * (1.0 + jnp.tanh(c0 * (y + 0.044715 * (y * y * y))))
    h_ref[...] = (gelu * val).astype(h_ref.dtype)
    tail_ref[...] = a[tm - SUBLANES:tm, :]


def ffn_up(z, w_gv, buf8, conv_w, conv_b, T):
    N, D = z.shape
    tm = min(T, 1024)
    per = T // tm
    tn = 512
    nj = D_FF_PAD // tn
    hb = tm // BF16_ROWS
    return pl.pallas_call(
        functools.partial(_ffn_up_kernel, per=per, tm=tm), grid=(N // tm, nj),
        in_specs=[pl.BlockSpec((tm, D), lambda i, j: (i, 0)),
                  pl.BlockSpec((BF16_ROWS, D), lambda i, j: (jnp.maximum(i * hb - 1, 0), 0)),
                  pl.BlockSpec((D, tn), lambda i, j: (0, j)),
                  pl.BlockSpec((D, tn), lambda i, j: (0, nj + j)),
                  pl.BlockSpec((None, SUBLANES, tn), lambda i, j: (i // per, 0, j)),
                  pl.BlockSpec((SUBLANES, tn), lambda i, j: (0, j)),
                  pl.BlockSpec((1, tn), lambda i, j: (0, j))],
        out_specs=[pl.BlockSpec((tm, tn), lambda i, j: (i, j)),
                   pl.BlockSpec((SUBLANES, tn), lambda i, j: (i, j))],
        out_shape=[jax.ShapeDtypeStruct((N, D_FF_PAD), bf16),
                   jax.ShapeDtypeStruct((N // tm * SUBLANES, D_FF_PAD), f32)],
        scratch_shapes=[pltpu.VMEM((tm + SUBLANES, tn), f32)],
        compiler_params=_cparams("parallel", "arbitrary"), name="ffn_up")(
            z, z, w_gv, w_gv, buf8, conv_w, conv_b)


def _head_sum(x, ones_blk, terms=3):
    return _dot_r(x, ones_blk) if terms == 3 else _dot_r2(x, ones_blk)


def _rwkv_prep_kernel(p_ref, halo_ref, sp_ref, mu_ref, w0_ref, a0_ref, kk_ref, ka_ref, rk_ref,
                      wup_ref, aup_ref, gup_ref, ones_ref,
                      nkk_o, wr_o, w_o, b_o, k_o, v_o, br_o, kr_o, rkr_o, g_o, scr_ref, *, per, tm):
    first = (pl.program_id(0) % per) == 0
    scr_ref[0:SUBLANES, :] = jnp.where(first, sp_ref[...], halo_ref[...])
    p = p_ref[...]
    scr_ref[SUBLANES:SUBLANES + tm, :] = p
    prev = scr_ref[SUBLANES - 1:SUBLANES - 1 + tm, :]
    xs = p + (prev - p) * mu_ref[...]
    W = RW_W
    r, k, v = xs[:, 0:W], xs[:, W:2 * W], xs[:, 2 * W:3 * W]
    o = 3 * W
    dw = xs[:, o:o + RW_LORA_PAD]
    da = xs[:, o + RW_LORA_PAD:o + 2 * RW_LORA_PAD]
    dg = xs[:, o + 2 * RW_LORA_PAD:o + 2 * RW_LORA_PAD + R_G]
    w_raw = -_softplus(-(w0_ref[...] + _dot(jnp.tanh(dw).astype(bf16), wup_ref[...]))) - RW_DECAY_CLAMP
    w = jnp.exp(-jnp.exp(w_raw))
    a = _sigmoid(a0_ref[...] + _dot(da.astype(bf16), aup_ref[...]))
    g = _dot(_sigmoid(dg).astype(bf16), gup_ref[...])
    ones_blk = ones_ref[...]
    kk = k * kk_ref[...]
    nrm = jnp.sqrt(_head_sum(kk * kk, ones_blk))
    kk = kk / jnp.maximum(nrm, 1e-12)
    k2 = k * (1.0 + (a - 1.0) * ka_ref[...])
    b = kk * a
    nkk_o[...] = -kk
    wr_o[...] = w * r
    w_o[...] = w
    b_o[...] = b
    k_o[...] = k2
    v_o[...] = v
    br_o[...] = _head_sum(b * r, ones_blk, terms=2)
    kr_o[...] = _head_sum(k2 * r, ones_blk, terms=2)
    rkr_o[...] = _head_sum(r * k2 * rk_ref[...], ones_blk, terms=2)
    g_o[...] = g


def rwkv_prep(P, col_blk, sp8, mu, w0, a0, k_k, k_a, r_k, w_up, a_up, g_up, ones_blk, T):
    N = P.shape[0]
    tm = min(T, 256)
    per = T // tm
    hb = tm // SUBLANES
    C = RW_PCOLS
    vec = lambda n: pl.BlockSpec((1, n), lambda i: (0, 0))
    full = lambda a: pl.BlockSpec(a.shape, lambda i: (0, 0))
    out = jax.ShapeDtypeStruct((N, RW_W), f32)
    ospec = pl.BlockSpec((tm, RW_W), lambda i: (i, 0))
    return pl.pallas_call(
        functools.partial(_rwkv_prep_kernel, per=per, tm=tm), grid=(N // tm,),
        in_specs=[pl.BlockSpec((tm, C), lambda i: (i, col_blk)),
                  pl.BlockSpec((SUBLANES, C), lambda i: (jnp.maximum(i * hb - 1, 0), col_blk)),
                  pl.BlockSpec((None, SUBLANES, C), lambda i: (i // per, 0, 0)),
                  vec(C), vec(RW_W), vec(RW_W), vec(RW_W), vec(RW_W), vec(RW_W),
                  full(w_up), full(a_up), full(g_up), full(ones_blk)],
        out_specs=[ospec] * 10, out_shape=[out] * 10,
        scratch_shapes=[pltpu.VMEM((tm + SUBLANES, C), f32)],
        compiler_params=_cparams("parallel"), name="rwkv_prep")(
            P, P, sp8, mu, w0, a0, k_k, k_a, r_k, w_up, a_up, g_up, ones_blk)


N_PAIR = RW_H // 2


def _rwkv_scan_kernel(nkk_ref, wr_ref, w_ref, b_ref, k_ref, br_ref, v_ref, s0_ref, ones_ref,
                      z_ref, so_ref, S_ref, vT_ref, *, n_sub, n_steps, nb):
    c = pl.program_id(1)
    chains = [(bb, j) for bb in range(nb) for j in range(N_PAIR)]

    @pl.when(c == 0)
    def _():
        S_ref[...] = s0_ref[...]

    if n_steps < CHUNK:
        z_ref[...] = jnp.zeros_like(z_ref)

    lo = _iota((1, LANES), 1) < RW_DH
    lane = _iota((1, LANES), 1)
    diag = _iota((RW_DH, LANES), 1) % RW_DH == _iota((RW_DH, LANES), 0)
    pair_ones = ones_ref[...]

    def as_row(x):
        return jnp.sum(jnp.where(diag, x, 0.0), axis=0, keepdims=True)

    for sub in range(n_sub):
        r0 = sub * CHUNK
        for q, (bb, j) in enumerate(chains):
            vT_ref[q] = v_ref[bb, r0:r0 + CHUNK, j * LANES:(j + 1) * LANES].T

        def group(g, carry):
            base = pl.multiple_of(r0 + g * SUBLANES, SUBLANES)
            tiles = [[ref[bb, pl.ds(base, SUBLANES), j * LANES:(j + 1) * LANES]
                      for ref in (nkk_ref, wr_ref, w_ref, b_ref, k_ref, br_ref)] for bb, j in chains]
            z_rows = [[] for _ in chains]
            for i in range(SUBLANES):
                tmask = lane == g * SUBLANES + i
                for q in range(len(chains)):
                    nkk_r, wr_r, w_r, b_r, k_r, br_r = [tl[i:i + 1, :] for tl in tiles[q]]
                    S = S_ref[q]
                    xa = S * nkk_r
                    hi = jnp.concatenate([xa, S * wr_r], axis=0).astype(bf16)
                    low = (xa - hi[0:RW_DH].astype(f32)).astype(bf16)
                    red = _dot(jnp.concatenate([hi, low], axis=0), pair_ones)
                    sa = red[0:RW_DH] + red[2 * RW_DH:3 * RW_DH]
                    zz = red[RW_DH:2 * RW_DH] + sa * br_r
                    va = jnp.sum(jnp.where(tmask, vT_ref[q, 0:RW_DH, :], 0.0), axis=-1, keepdims=True)
                    vb = jnp.sum(jnp.where(tmask, vT_ref[q, RW_DH:2 * RW_DH, :], 0.0), axis=-1, keepdims=True)
                    vp = jnp.where(lo, va, vb)
                    S_ref[q] = S * w_r + sa * b_r + vp * k_r
                    z_rows[q].append(as_row(zz))
            for q, (bb, j) in enumerate(chains):
                cs = slice(j * LANES, (j + 1) * LANES)
                z_ref[bb, pl.ds(base, SUBLANES), cs] = jnp.concatenate(z_rows[q], axis=0)
            return carry

        lax.fori_loop(0, n_steps // SUBLANES, group, 0)

    so_ref[...] = S_ref[...]


def rwkv_scan(nkk, wr, w, b, k, br, v, s0, pair_ones, n_steps):
    B, Tp, W = nkk.shape
    nb = 2 if B % 2 == 0 else 1
    tc = min(Tp, 4 * CHUNK)
    n_sub = tc // CHUNK
    rows = pl.BlockSpec((None, nb, tc, W), lambda bi, c: (bi, 0, c, 0))
    st = pl.BlockSpec((None, nb * N_PAIR, RW_DH, LANES), lambda bi, c: (bi, 0, 0, 0))
    grp = lambda a: a.reshape(B // nb, nb, Tp, W)
    z, so = pl.pallas_call(
        functools.partial(_rwkv_scan_kernel, n_sub=n_sub, n_steps=min(n_steps, CHUNK), nb=nb),
        grid=(B // nb, Tp // tc),
        in_specs=[rows] * 7 + [st, pl.BlockSpec((LANES, LANES), lambda bi, c: (0, 0))],
        out_specs=[rows, st],
        out_shape=[jax.ShapeDtypeStruct((B // nb, nb, Tp, W), f32),
                   jax.ShapeDtypeStruct((B // nb, nb * N_PAIR, RW_DH, LANES), f32)],
        scratch_shapes=[pltpu.VMEM((nb * N_PAIR, RW_DH, LANES), f32), pltpu.VMEM((nb * N_PAIR, LANES, LANES), f32)],
        compiler_params=_cparams("parallel", "arbitrary"), name="rwkv_scan")(
            grp(nkk), grp(wr), grp(w), grp(b), grp(k), grp(br), grp(v),
            s0.reshape(B // nb, nb * N_PAIR, RW_DH, LANES), pair_ones)
    return z.reshape(B, Tp, W), so.reshape(s0.shape)


def _rwkv_post_kernel(z_ref, v_ref, kr_ref, rkr_ref, g_ref, lnw_ref, lnb_ref, ones_ref, o_ref):
    ones_blk = ones_ref[...]
    v = v_ref[...]
    y = z_ref[...] + v * kr_ref[...]
    mu = _head_sum(y, ones_blk, terms=2) * (1.0 / RW_DH)
    yc = y - mu
    var = _head_sum(yc * yc, ones_blk, terms=2) * (1.0 / RW_DH)
    yn = yc * lax.rsqrt(var + RW_LN_EPS) * lnw_ref[...] + lnb_ref[...]
    o_ref[...] = ((yn + rkr_ref[...] * v) * g_ref[...]).astype(o_ref.dtype)


def rwkv_post(z, v, kr, rkr, g, ln_w, ln_b, ones_blk):
    N, W = z.shape
    tm = min(N, 512)
    rows = pl.BlockSpec((tm, W), lambda i: (i, 0))
    vec = pl.BlockSpec((1, W), lambda i: (0, 0))
    return pl.pallas_call(
        _rwkv_post_kernel, grid=(N // tm,),
        in_specs=[rows] * 5 + [vec, vec, pl.BlockSpec(ones_blk.shape, lambda i: (0, 0))],
        out_specs=rows, out_shape=jax.ShapeDtypeStruct((N, W), bf16),
        compiler_params=_cparams("parallel"), name="rwkv_post")(z, v, kr, rkr, g, ln_w, ln_b, ones_blk)


def _mlstm_kernel(main_ref, halo_ref, buf_ref, sm_ref, bias_ref, cw_ref, cb_ref, gn_ref, tri_ref,
                  c0_ref, n0_ref, m0_ref,
                  o_ref, lf_o, F_o, FT_o, c_o, n_o, m_o,
                  scr_ref, ct_ref, n_ref, m_ref, carry_ref, *, t_real):
    c = pl.program_id(1)
    L = CHUNK

    @pl.when(c == 0)
    def _():
        ct_ref[...] = c0_ref[...]
        n_ref[...] = n0_ref[...]
        m_ref[...] = m0_ref[...]
        carry_ref[...] = jnp.zeros_like(carry_ref)

    main = main_ref[...]
    scr_ref[0:SUBLANES, :] = jnp.where(c == 0, buf_ref[...], halo_ref[...])
    scr_ref[SUBLANES:SUBLANES + L, :] = main[:, 0:ML_QK]
    qk = cb_ref[...]
    for j in range(ML_CONV):
        off = SUBLANES - (ML_CONV - 1) + j
        qk = qk + scr_ref[off:off + L, :] * cw_ref[j:j + 1, :]
    qk = qk * _sigmoid(qk)
    q_all = qk[:, 0:ML_W]
    k_all = qk[:, ML_W:ML_QK] * (ML_DH ** -0.5)
    v_all = main[:, ML_QK:ML_QK + ML_W]
    og_all = main[:, ML_QK + ML_W:ML_QK + 2 * ML_W]

    valid = (c * L + _iota((L, 1), 0)) < t_real
    pre = sm_ref[...] + bias_ref[...]
    li_all = jnp.where(valid, pre, NEG)
    lf_all = jnp.where(valid, _log_sigmoid(pre), 0.0)
    cum = _dot_l(tri_ref[...], lf_all)
    F = cum + carry_ref[...]
    carry_ref[...] = F[L - 1:L, :]
    lf_o[...] = lf_all
    F_o[...] = F
    FT_o[...] = F.T
    liT = li_all.T
    cumT = cum.T

    row = _iota((L, L), 0)
    col = _iota((L, L), 1)
    causal = col <= row
    for h in range(ML_H):
        hs = slice(h * ML_DH, (h + 1) * ML_DH)
        b_col = cum[:, SM_LF + h:SM_LF + h + 1]
        b_row = cumT[SM_LF + h:SM_LF + h + 1, :]
        li_col = li_all[:, SM_LI + h:SM_LI + h + 1]
        li_row = liT[SM_LI + h:SM_LI + h + 1, :]
        m_prev = m_ref[h][:, 0:1]
        d = jnp.where(causal, b_col - b_row + li_row, NEG)
        inter = b_col + m_prev
        m_t = jnp.maximum(inter, jnp.max(d, axis=-1, keepdims=True))
        w_intra = jnp.exp(d - m_t)
        w_state = jnp.exp(inter - m_t)
        q = q_all[:, hs]
        k = k_all[:, hs]
        v = v_all[:, hs]
        qb, kb = q.astype(bf16), k.astype(bf16)
        s = _dot_nt(qb, kb) * w_intra
        ct = ct_ref[h]
        num = w_state * _dot(qb, ct.astype(bf16)) + _dot(s.astype(bf16), v.astype(bf16))
        n_row = n_ref[h]
        den = w_state * jnp.sum(q * n_row, axis=-1, keepdims=True) + jnp.sum(s, axis=-1, keepdims=True)
        hh = num / jnp.maximum(jnp.abs(den), jnp.exp(-m_t))
        g_end = b_col[L - 1:L, :]
        lw_s = g_end - b_col + li_col
        m_new = jnp.maximum(g_end + m_prev, jnp.max(lw_s, axis=0, keepdims=True))
        w_s = jnp.exp(lw_s - m_new)
        decay = jnp.exp(g_end + m_prev - m_new)
        ct_ref[h] = decay * ct + _dot(kb.T, (w_s * v).astype(bf16))
        n_ref[h] = decay * n_row + jnp.sum(w_s * k, axis=0, keepdims=True)
        m_ref[h] = jnp.broadcast_to(m_new, (1, LANES))
        mu = jnp.mean(hh, axis=-1, keepdims=True)
        hc = hh - mu
        var = jnp.mean(hc * hc, axis=-1, keepdims=True)
        hn = hc * lax.rsqrt(var + ML_GN_EPS) * gn_ref[:, hs]
        o_ref[:, hs] = (_sigmoid(og_all[:, hs]) * hn).astype(o_ref.dtype)

    c_o[...] = ct_ref[...]
    n_o[...] = n_ref[...]
    m_o[...] = m_ref[...]


def mlstm_fox_prep(main_arr, main_blk, small_arr, small_blk, buf8, bias_row, conv_w, conv_b, gn_w, tri,
                   c0t, n0, m0, B, Tp, t_real):
    L = CHUNK
    nc = Tp // L
    hb = L // SUBLANES
    st_c = pl.BlockSpec((None, ML_H, ML_DH, ML_DH), lambda b, c: (b, 0, 0, 0))
    st_n = pl.BlockSpec((None, ML_H, 1, ML_DH), lambda b, c: (b, 0, 0, 0))
    rows = lambda w: pl.BlockSpec((L, w), lambda b, c: (b * nc + c, 0))
    vec = lambda n: pl.BlockSpec((1, n), lambda b, c: (0, 0))
    N = B * Tp
    return pl.pallas_call(
        functools.partial(_mlstm_kernel, t_real=t_real), grid=(B, nc),
        in_specs=[pl.BlockSpec((L, ML_PCOLS), lambda b, c: (b * nc + c, main_blk)),
                  pl.BlockSpec((SUBLANES, ML_QK), lambda b, c: (jnp.maximum((b * nc + c) * hb - 1, 0), 2 * main_blk)),
                  pl.BlockSpec((None, SUBLANES, ML_QK), lambda b, c: (b, 0, 0)),
                  pl.BlockSpec((L, LANES), lambda b, c: (b * nc + c, small_blk)),
                  vec(LANES),
                  pl.BlockSpec((SUBLANES, ML_QK), lambda b, c: (0, 0)),
                  vec(ML_QK), vec(ML_W),
                  pl.BlockSpec((L, L), lambda b, c: (0, 0)),
                  st_c, st_n, st_n],
        out_specs=[rows(ML_W), rows(LANES), rows(LANES),
                   pl.BlockSpec((None, LANES, L), lambda b, c: (b, 0, c)),
                   st_c, st_n, st_n],
        out_shape=[jax.ShapeDtypeStruct((N, ML_W), bf16),
                   jax.ShapeDtypeStruct((N, LANES), f32),
                   jax.ShapeDtypeStruct((N, LANES), f32),
                   jax.ShapeDtypeStruct((B, LANES, Tp), f32),
                   jax.ShapeDtypeStruct((B, ML_H, ML_DH, ML_DH), f32),
                   jax.ShapeDtypeStruct((B, ML_H, 1, ML_DH), f32),
                   jax.ShapeDtypeStruct((B, ML_H, 1, ML_DH), f32)],
        scratch_shapes=[pltpu.VMEM((L + SUBLANES, ML_QK), f32),
                        pltpu.VMEM((ML_H, ML_DH, ML_DH), f32),
                        pltpu.VMEM((ML_H, 1, ML_DH), f32),
                        pltpu.VMEM((ML_H, 1, LANES), f32),
                        pltpu.VMEM((1, LANES), f32)],
        compiler_params=_cparams("parallel", "arbitrary"), name="mlstm")(
            main_arr, main_arr, buf8, small_arr, bias_row, conv_w, conv_b, gn_w, tri, c0t, n0, m0)


FOX_HG = 4


def _fox_attn_kernel(q_ref, k_ref, v_ref, ft_ref, o_ref, m_ref, l_ref, acc_ref, *, tq, tk):
    h0 = pl.program_id(1) * FOX_HG
    qi = pl.program_id(2)
    m_ref[...] = jnp.full_like(m_ref, NEG)
    l_ref[...] = jnp.zeros_like(l_ref)
    acc_ref[...] = jnp.zeros_like(acc_ref)
    qbs = [(q_ref[:, u * FX_DH:(u + 1) * FX_DH] * (FX_DH ** -0.5)).astype(bf16) for u in range(FOX_HG)]

    def block(j, masked):
        k0 = pl.multiple_of(j * tk, tk)
        for u in range(FOX_HG):
            hs = slice(u * FX_DH, (u + 1) * FX_DH)
            kb = k_ref[pl.ds(k0, tk), hs].astype(bf16)
            vb = v_ref[pl.ds(k0, tk), hs].astype(bf16)
            s = _dot_nt(qbs[u], kb) - ft_ref[pl.ds(h0 + u, 1), pl.ds(k0, tk)]
            if masked:
                s = jnp.where(_iota((tq, tk), 1) + k0 <= qi * tq + _iota((tq, tk), 0), s, NEG)
            m_new, l_new, alpha, (p,) = _softmax_step([s], m_ref[u], l_ref[u])
            acc_ref[:, hs] = alpha * acc_ref[:, hs] + _dot(p, vb)
            m_ref[u] = m_new
            l_ref[u] = l_new

    n_full = (qi * tq) // tk

    def body(j, carry):
        block(j, False)
        return carry

    lax.fori_loop(0, n_full, body, 0)
    for d in range(max(tq // tk, 1)):
        block(n_full + d, True)
    for u in range(FOX_HG):
        hs = slice(u * FX_DH, (u + 1) * FX_DH)
        o_ref[:, hs] = (acc_ref[:, hs] / l_ref[u]).astype(o_ref.dtype)


def fox_attn(P, Kx, Vx, FT, B, T):
    tq = 256
    tk = 512
    nq = T // tq
    wb = FOX_HG * FX_DH
    qb = OFF_FXQ // wb
    return pl.pallas_call(
        functools.partial(_fox_attn_kernel, tq=tq, tk=tk), grid=(B, FX_H // FOX_HG, nq),
        in_specs=[pl.BlockSpec((tq, wb), lambda b, h, i: (b * nq + i, qb + h)),
                  pl.BlockSpec((T, wb), lambda b, h, i: (b, h)),
                  pl.BlockSpec((T, wb), lambda b, h, i: (b, h)),
                  pl.BlockSpec((None, SUBLANES, T), lambda b, h, i: (b, SM_FX // SUBLANES, 0))],
        out_specs=pl.BlockSpec((tq, wb), lambda b, h, i: (b * nq + i, h)),
        out_shape=jax.ShapeDtypeStruct((B * T, FX_W), bf16),
        scratch_shapes=[pltpu.VMEM((FOX_HG, tq, LANES), f32), pltpu.VMEM((FOX_HG, tq, LANES), f32),
                        pltpu.VMEM((tq, wb), f32)],
        compiler_params=_cparams("parallel", "parallel", "arbitrary"), name="fox_attn")(P, Kx, Vx, FT)


def _fox_suffix_kernel(pt_ref, lf_ref, triu_ref, later_ref, o_ref, x_ref, *, n_pages):
    b = pl.program_id(0)

    def gather(j, carry):
        x_ref[pl.ds(pl.multiple_of(j * FX_H, FX_H), FX_H), :] = lf_ref[pt_ref[b, j]]
        return carry

    lax.fori_loop(0, n_pages, gather, 0)
    x = x_ref[...]
    within = _dot_r(x, triu_ref[...])
    tot = jnp.broadcast_to(jnp.sum(x, axis=-1, keepdims=True), x.shape)
    o_ref[...] = within + _dot_l(later_ref[...], tot)


def fox_suffix(page_table, cache_lft, triu, later, layer):
    B, n_pages = page_table.shape
    n_pool = cache_lft.shape[1]
    rows = n_pages * FX_H
    grid_spec = pltpu.PrefetchScalarGridSpec(
        num_scalar_prefetch=1, grid=(B,),
        in_specs=[pl.BlockSpec((None, n_pool, FX_H, PAGE_SIZE), lambda b, pt: (layer, 0, 0, 0)),
                  pl.BlockSpec((PAGE_SIZE, PAGE_SIZE), lambda b, pt: (0, 0)),
                  pl.BlockSpec((rows, rows), lambda b, pt: (0, 0))],
        out_specs=pl.BlockSpec((None, rows, PAGE_SIZE), lambda b, pt: (b, 0, 0)),
        scratch_shapes=[pltpu.VMEM((rows, PAGE_SIZE), f32)])
    return pl.pallas_call(
        functools.partial(_fox_suffix_kernel, n_pages=n_pages), grid_spec=grid_spec,
        out_shape=jax.ShapeDtypeStruct((B, rows, PAGE_SIZE), f32),
        compiler_params=_cparams("parallel"), name="fox_suffix")(page_table, cache_lft, triu, later)


DEC_PP = 8


def _fox_decode_kernel(pt_ref, q_ref, *refs, n_steps, pp, t_new):
    kc = refs[0:pp]
    vc = refs[pp:2 * pp]
    suf_ref, hmask_ref, kn_ref, vn_ref, fkn_ref, o_ref, m_ref, l_ref, acc_ref = refs[2 * pp:]
    j = pl.program_id(1)
    nq = FX_H * t_new
    cols = PAGE_SIZE * FX_H

    @pl.when(j == 0)
    def _():
        m_ref[...] = jnp.full_like(m_ref, NEG)
        l_ref[...] = jnp.zeros_like(l_ref)
        acc_ref[...] = jnp.zeros_like(acc_ref)

    qb = (q_ref[...] * (FX_DH ** -0.5)).astype(bf16)

    def update(ss, vs):
        m_new, l_new, alpha, ps = _softmax_step(ss, m_ref[...], l_ref[...])
        acc = alpha * acc_ref[...]
        for p, vb in zip(ps, vs):
            acc = acc + _dot(p, vb)
        l_ref[...] = l_new
        acc_ref[...] = acc
        m_ref[...] = m_new

    @pl.when(j < n_steps)
    def _():
        hmask = hmask_ref[...]
        ss = [_dot_nt(qb, kc[i][...].reshape(cols, FX_DH).astype(bf16)) + hmask + suf_ref[i] for i in range(pp)]
        update(ss, [vc[i][...].reshape(cols, FX_DH).astype(bf16) for i in range(pp)])

    @pl.when(j == n_steps)
    def _():
        s = _dot_nt(qb, kn_ref[...].astype(bf16)) - fkn_ref[...]
        col = _iota((nq, LANES), 1)
        row = _iota((nq, LANES), 0)
        ok = (col % FX_H == row // t_new) & (col // FX_H <= row % t_new)
        update([jnp.where(ok, s, NEG)], [vn_ref[...].astype(bf16)])
        o_ref[...] = acc_ref[...] / l_ref[...]


def fox_decode(page_table, q2, cache_k, cache_v, suffix, hmask, k_new, v_new, fkn, layer, t_new):
    B, n_pages = page_table.shape
    pp = DEC_PP
    n_steps = n_pages // pp
    nq = FX_H * t_new
    cols = PAGE_SIZE * FX_H

    def page_spec(i):
        return pl.BlockSpec((None, None, PAGE_SIZE, FX_H, FX_DH),
                            lambda b, j, pt: (layer, pt[b, jnp.minimum(j, n_steps - 1) * pp + i], 0, 0, 0))

    per_b = lambda shape: pl.BlockSpec((None,) + shape, lambda b, j, pt: (b, 0, 0))
    grid_spec = pltpu.PrefetchScalarGridSpec(
        num_scalar_prefetch=1, grid=(B, n_steps + 1),
        in_specs=[per_b((nq, FX_DH))] + [page_spec(i) for i in range(pp)] * 2
        + [pl.BlockSpec((None, pp, 1, cols), lambda b, j, pt: (b, jnp.minimum(j, n_steps - 1), 0, 0)),
           pl.BlockSpec((nq, cols), lambda b, j, pt: (0, 0)),
           per_b((LANES, FX_DH)), per_b((LANES, FX_DH)), per_b((1, LANES))],
        out_specs=per_b((nq, FX_DH)),
        scratch_shapes=[pltpu.VMEM((nq, LANES), f32), pltpu.VMEM((nq, LANES), f32), pltpu.VMEM((nq, FX_DH), f32)])
    return pl.pallas_call(
        functools.partial(_fox_decode_kernel, n_steps=n_steps, pp=pp, t_new=t_new), grid_spec=grid_spec,
        out_shape=jax.ShapeDtypeStruct((B, nq, FX_DH), f32),
        compiler_params=_cparams("parallel", "arbitrary"), name="fox_decode")(
            page_table, q2, *([cache_k] * pp), *([cache_v] * pp), suffix, hmask, k_new, v_new, fkn)


def _pad_cols(w, n):
    return jnp.pad(w, ((0, 0), (0, n - w.shape[1])))


def _rw_cols(x):
    W = RW_W
    pad = [(0, 0)] * (x.ndim - 1) + [(0, RW_LORA_PAD - R_W)]
    return jnp.concatenate([x[..., 0:3 * W], jnp.pad(x[..., 3 * W:3 * W + R_W], pad),
                            jnp.pad(x[..., 3 * W + R_W:3 * W + R_W + R_A], pad),
                            x[..., 3 * W + R_W + R_A:]], axis=-1)


def _rw_cols_inv(x):
    W = RW_W
    o = 3 * W
    return jnp.concatenate([x[..., 0:o], x[..., o:o + R_W], x[..., o + RW_LORA_PAD:o + RW_LORA_PAD + R_A],
                            x[..., o + 2 * RW_LORA_PAD:]], axis=-1)


def _w_in_split(w):
    o_ml = RW_COLS
    o_fx = o_ml + ML_COLS
    o_gt = o_fx + FX_COLS
    ml = w[:, o_ml:o_fx]
    fx = w[:, o_fx:o_gt]
    ml_main = jnp.concatenate([ml[:, 0:ML_QK + ML_W], ml[:, ML_QK + ML_W + 2 * ML_H:]], axis=1)
    small = _pad_cols(jnp.concatenate([ml[:, ML_QK + ML_W:ML_QK + ML_W + 2 * ML_H], fx[:, 3 * FX_W:]], axis=1), LANES)
    main = jnp.concatenate([w[:, o_gt:], _rw_cols(w[:, 0:RW_COLS]), ml_main, fx[:, 0:FX_W]], axis=1)
    return (main.astype(bf16), fx[:, FX_W:2 * FX_W].astype(bf16), fx[:, 2 * FX_W:3 * FX_W].astype(bf16),
            small.astype(bf16))


def _pad_rows(w, n):
    return jnp.pad(w, ((0, n - w.shape[0]), (0, 0)))


def _prev_rows8(buf, width):
    B, r, C = buf.shape
    return jnp.pad(buf, ((0, 0), (SUBLANES - r, 0), (0, width - C)))


def _layer(x, mod, st, past, W, page_table, l, B, T):
    N = B * T
    shift1, scale1, gate1, shift2, scale2, gate2 = jnp.split(mod, N_MOD, axis=-1)
    u = norm_mod(x, W['norm_pre_mix'], scale1, shift1, T)
    P = matmul(u, W['w_in'], tn=1024)
    Kx = matmul(u, W['w_in_k'], tn=FX_W)
    Vx = matmul(u, W['w_in_v'], tn=FX_W)
    SM = matmul(u, W['w_in_small'], tn=LANES)

    padded = T % CHUNK != 0
    Tp = T if not padded else CHUNK

    def pad_t(a):
        if not padded:
            return a
        return jnp.pad(a.reshape(B, T, -1), ((0, 0), (0, Tp - T), (0, 0))).reshape(B * Tp, -1)

    def unpad_t(a):
        if not padded:
            return a
        return a.reshape(B, Tp, -1)[:, :T].reshape(N, -1)

    sp8 = _prev_rows8(_rw_cols(st['rw_shift'])[:, None, :], RW_PCOLS)
    prep = rwkv_prep(P, OFF_RW // RW_PCOLS, sp8, W['rw_mu'], W['rw_w0'], W['rw_a0'], W['rw_k_k'], W['rw_k_a'],
                     W['rw_r_k'], W['rw_w_up'], W['rw_a_up'], W['rw_g_up'], W['ones_blk'], T)
    nkk, wr, w_, b_, k2, v_rw, br, kr, rkr, g_rw = prep
    s0 = st['rw_wkv'].reshape(B, N_PAIR, 2, RW_DH, RW_DH).transpose(0, 1, 3, 2, 4).reshape(B, N_PAIR, RW_DH, LANES)
    seq = [pad_t(a).reshape(B, Tp, RW_W) for a in (nkk, wr, w_, b_, k2, br, v_rw)]
    z, s_out = rwkv_scan(*seq, s0, W['ones_blk'][0:LANES, 0:LANES], T)
    z = unpad_t(z.reshape(B * Tp, RW_W))
    o_rw = rwkv_post(z, v_rw, kr, rkr, g_rw, W['rw_ln_w'], W['rw_ln_b'], W['ones_blk'])
    rw_wkv = s_out.reshape(B, N_PAIR, RW_DH, 2, RW_DH).transpose(0, 1, 3, 2, 4).reshape(B, RW_H, RW_DH, RW_DH)
    rw_shift = _rw_cols_inv(P[:, OFF_RW:OFF_RW + RW_PCOLS].reshape(B, T, RW_PCOLS)[:, -1])

    buf8 = _prev_rows8(st['ml_conv'], ML_QK)
    c0t = jnp.swapaxes(st['ml_c'], -1, -2)
    n0 = st['ml_n'][:, :, None, :]
    m0 = jnp.broadcast_to(st['ml_m'][:, :, None, None], (B, ML_H, 1, LANES))
    if padded:
        main_arr, main_blk = pad_t(P[:, OFF_ML:OFF_ML + ML_PCOLS]), 0
        small_arr, small_blk = pad_t(SM), 0
    else:
        main_arr, main_blk, small_arr, small_blk = P, OFF_ML // ML_PCOLS, SM, 0
    o_ml, lf_all, F, FT, ct, n_new, m_new = mlstm_fox_prep(
        main_arr, main_blk, small_arr, small_blk, buf8, W['small_bias'], W['ml_conv_w'], W['ml_conv_b'],
        W['ml_gn_w'], W['tri'], c0t, n0, m0, B, Tp, T)
    o_ml = unpad_t(o_ml)
    ml_c = jnp.swapaxes(ct, -1, -2)
    ml_n = n_new[:, :, 0, :]
    ml_m = m_new[:, :, 0, 0]
    ml_conv = P[:, OFF_ML:OFF_ML + ML_QK].reshape(B, T, ML_QK)[:, T - (ML_CONV - 1):]
    fox_logf = unpad_t(lf_all)[:, SM_FX:SM_FX + FX_H].reshape(B, T, FX_H)
    fox_k = Kx.reshape(B, T, FX_H, FX_DH)
    fox_v = Vx.reshape(B, T, FX_H, FX_DH)

    if past is None:
        o_fx = fox_attn(P, Kx, Vx, FT, B, T)
    else:
        cache_k, cache_v, cache_lft = past
        n_pages = page_table.shape[1]
        nq = FX_H * T
        assert nq <= LANES and n_pages % DEC_PP == 0
        suffix = fox_suffix(page_table, cache_lft, W['triu'], W['later'], l)
        suffix = suffix.reshape(B, n_pages, FX_H, PAGE_SIZE).transpose(0, 1, 3, 2).reshape(
            B, n_pages, 1, PAGE_SIZE * FX_H)
        q2 = P[:, OFF_FXQ:OFF_FXQ + FX_W].reshape(B, T, FX_H, FX_DH).transpose(0, 2, 1, 3).reshape(B, nq, FX_DH)
        padk = lambda a: jnp.pad(a.reshape(B, nq, FX_DH), ((0, 0), (0, LANES - nq), (0, 0)))
        fkn = F.reshape(B, Tp, LANES)[:, :T, SM_FX:SM_FX + FX_H].reshape(B, 1, nq)
        fkn = jnp.pad(fkn, ((0, 0), (0, 0), (0, LANES - nq)))
        hmask = jnp.where(jnp.arange(PAGE_SIZE * FX_H)[None, :] % FX_H == jnp.arange(nq)[:, None] // T, 0.0, NEG)
        o = fox_decode(page_table, q2, cache_k, cache_v, suffix, hmask.astype(f32), padk(fox_k), padk(fox_v),
                       fkn, l, T)
        o_fx = o.reshape(B, FX_H, T, FX_DH).transpose(0, 2, 1, 3).reshape(N, FX_W)

    merged = merge(o_rw, o_ml, o_fx, W['w_br_rwkv'], W['w_br_mlstm'], W['w_br_fox'], P)
    x = mm_norm_res(merged, W['w_out'], x, W['norm_post_mix'], gate1, T, D_MODEL)

    zf = norm_mod(x, W['norm_pre_ffn'], scale2, shift2, T)
    fbuf8 = _prev_rows8(st['ffn_conv'], D_FF_PAD)
    if T % BF16_ROWS == 0:
        hmid, tails = ffn_up(zf, W['ffn_w_gv'], fbuf8, W['ffn_conv_w'], W['ffn_conv_b'], T)
        per = N // B // min(T, 1024)
        ffn_conv = tails.reshape(B, per, SUBLANES, D_FF_PAD)[:, -1, SUBLANES - (FFN_CONV - 1):, 0:D_FF]
    else:
        av = matmul(zf, W['ffn_w_gv'], tn=1024)
        hmid = ffn_act(av, fbuf8, W['ffn_conv_w'], W['ffn_conv_b'], T)
        ffn_conv = av[:, 0:D_FF].reshape(B, T, D_FF)[:, T - (FFN_CONV - 1):]
    x = mm_norm_res(hmid, W['ffn_w_down'], x, W['norm_post_ffn'], gate2, T, FF_TK)

    new = dict(fox_k=fox_k, fox_v=fox_v, fox_logf=fox_logf, rw_shift=rw_shift, rw_wkv=rw_wkv,
               ml_conv=ml_conv, ml_c=ml_c, ml_n=ml_n, ml_m=ml_m, ffn_conv=ffn_conv)
    return x, new


STATE_NAMES = ("fox_k", "fox_v", "fox_logf", "rw_shift", "rw_wkv", "ml_conv", "ml_c", "ml_n", "ml_m", "ffn_conv")


def _layer_weights(Pm, l, n_pages):
    row = lambda v: v.reshape(1, -1)
    W = {}
    for name in ('norm_pre_mix', 'norm_post_mix', 'norm_pre_ffn', 'norm_post_ffn'):
        W[name] = Pm[name][l]
    W['w_in'], W['w_in_k'], W['w_in_v'], W['w_in_small'] = _w_in_split(Pm['w_in'][l])
    W['rw_mu'] = row(_rw_cols(Pm['rw_mu'][l]))
    for name in ('rw_w0', 'rw_a0', 'rw_k_k', 'rw_k_a', 'rw_r_k', 'rw_ln_w', 'rw_ln_b'):
        W[name] = row(Pm[name][l])
    W['rw_w_up'] = _pad_rows(Pm['rw_w_up'][l], RW_LORA_PAD).astype(bf16)
    W['rw_a_up'] = _pad_rows(Pm['rw_a_up'][l], RW_LORA_PAD).astype(bf16)
    W['rw_g_up'] = Pm['rw_g_up'][l].astype(bf16)
    hid = jnp.arange(RW_W) // RW_DH
    W['ones_blk'] = (hid[:, None] == hid[None, :]).astype(bf16)
    W['small_bias'] = row(jnp.pad(jnp.concatenate([Pm['ml_b_i'][l], Pm['ml_b_f'][l], Pm['fx_b_f'][l]]),
                                  (0, LANES - 2 * ML_H - FX_H)))
    W['ml_conv_w'] = _pad_rows(Pm['ml_conv_w'][l], SUBLANES)
    W['ml_conv_b'] = row(Pm['ml_conv_b'][l])
    W['ml_gn_w'] = row(Pm['ml_gn_w'][l])
    idx = jnp.arange(CHUNK)
    W['tri'] = (idx[None, :] <= idx[:, None]).astype(bf16)
    W['triu'] = (idx[:, None] > idx[None, :]).astype(bf16)
    if n_pages:
        r = jnp.arange(n_pages * FX_H)
        W['later'] = ((r[:, None] % FX_H == r[None, :] % FX_H) & (r[None, :] // FX_H > r[:, None] // FX_H)).astype(bf16)
    W['w_br_rwkv'] = Pm['w_br_rwkv'][l].astype(bf16)
    W['w_br_mlstm'] = Pm['w_br_mlstm'][l].astype(bf16)
    W['w_br_fox'] = Pm['w_br_fox'][l].astype(bf16)
    W['w_out'] = Pm['w_out'][l].astype(bf16)
    W['ffn_w_gv'] = jnp.concatenate([_pad_cols(Pm['ffn_w_gate'][l], D_FF_PAD),
                                     _pad_cols(Pm['ffn_w_val'][l], D_FF_PAD)], axis=1).astype(bf16)
    W['ffn_conv_w'] = _pad_rows(_pad_cols(Pm['ffn_conv_w'][l], D_FF_PAD), SUBLANES)
    W['ffn_conv_b'] = _pad_cols(row(Pm['ffn_conv_b'][l]), D_FF_PAD)
    W['ffn_w_down'] = _pad_rows(Pm['ffn_w_down'][l], D_FF_PAD).astype(bf16)
    return W


def kernel(x_prompt, x_sample, cache_fox_k, cache_fox_v, cache_fox_logf, state_rwkv_shift, state_rwkv_wkv,
           state_mlstm_conv, state_mlstm_c, state_mlstm_n, state_mlstm_m, state_ffn_conv, page_table,
           c_prompt, c_sample, w_ada, b_ada, norm_pre_mix, norm_post_mix, norm_pre_ffn, norm_post_ffn, w_in,
           rw_mu, rw_w0, rw_w_up, rw_a0, rw_a_up, rw_g_up, rw_k_k, rw_k_a, rw_r_k, rw_ln_w, rw_ln_b,
           ml_conv_w, ml_conv_b, ml_b_i, ml_b_f, ml_gn_w, fx_b_f, w_br_rwkv, w_br_mlstm, w_br_fox, w_out,
           ffn_w_gate, ffn_w_val, ffn_conv_w, ffn_conv_b, ffn_w_down):
    Pm = dict(norm_pre_mix=norm_pre_mix, norm_post_mix=norm_post_mix, norm_pre_ffn=norm_pre_ffn,
              norm_post_ffn=norm_post_ffn, w_in=w_in, rw_mu=rw_mu, rw_w0=rw_w0, rw_w_up=rw_w_up, rw_a0=rw_a0,
              rw_a_up=rw_a_up, rw_g_up=rw_g_up, rw_k_k=rw_k_k, rw_k_a=rw_k_a,
              rw_r_k=rw_r_k.reshape(DEPTH, RW_W), rw_ln_w=rw_ln_w, rw_ln_b=rw_ln_b, ml_conv_w=ml_conv_w,
              ml_conv_b=ml_conv_b, ml_b_i=ml_b_i, ml_b_f=ml_b_f, ml_gn_w=ml_gn_w, fx_b_f=fx_b_f,
              w_br_rwkv=w_br_rwkv, w_br_mlstm=w_br_mlstm, w_br_fox=w_br_fox, w_out=w_out, ffn_w_gate=ffn_w_gate,
              ffn_w_val=ffn_w_val, ffn_conv_w=ffn_conv_w, ffn_conv_b=ffn_conv_b, ffn_w_down=ffn_w_down)
    Bp, Tpr, D = x_prompt.shape
    Bs, Ts, _ = x_sample.shape
    cache_k, cache_v = cache_fox_k, cache_fox_v
    cache_lft = jnp.swapaxes(cache_fox_logf, -1, -2)

    zeros = lambda *s: jnp.zeros(s, f32)
    xp = x_prompt.reshape(Bp * Tpr, D)
    xs = x_sample.reshape(Bs * Ts, D)
    c_all = jnp.pad(jnp.concatenate([c_prompt, c_sample], axis=0), ((0, 16 - Bp - Bs), (0, 0)))
    new_p = {n: [] for n in STATE_NAMES}
    new_s = {n: [] for n in STATE_NAMES}
    for l in range(DEPTH):
        W = _layer_weights(Pm, l, page_table.shape[1])
        mod = ada_mod(c_all, w_ada[l].astype(bf16), b_ada[l].reshape(1, -1))
        st_p = dict(rw_shift=zeros(Bp, RW_COLS), rw_wkv=zeros(Bp, RW_H, RW_DH, RW_DH),
                    ml_conv=zeros(Bp, ML_CONV - 1, ML_QK), ml_c=zeros(Bp, ML_H, ML_DH, ML_DH),
                    ml_n=zeros(Bp, ML_H, ML_DH), ml_m=zeros(Bp, ML_H), ffn_conv=zeros(Bp, FFN_CONV - 1, D_FF))
        st_s = dict(rw_shift=state_rwkv_shift[l], rw_wkv=state_rwkv_wkv[l], ml_conv=state_mlstm_conv[l],
                    ml_c=state_mlstm_c[l], ml_n=state_mlstm_n[l], ml_m=state_mlstm_m[l],
                    ffn_conv=state_ffn_conv[l])
        xp, lp = _layer(xp, mod[:Bp], st_p, None, W, None, l, Bp, Tpr)
        xs, ls = _layer(xs, mod[Bp:Bp + Bs], st_s, (cache_k, cache_v, cache_lft), W, page_table, l, Bs, Ts)
        for n in STATE_NAMES:
            new_p[n].append(lp[n])
            new_s[n].append(ls[n])
    sp = {n: jnp.stack(v) for n, v in new_p.items()}
    ss = {n: jnp.stack(v) for n, v in new_s.items()}
    return (xp.reshape(Bp, Tpr, D), xs.reshape(Bs, Ts, D),
            sp['fox_k'], ss['fox_k'], sp['fox_v'], ss['fox_v'], sp['fox_logf'], ss['fox_logf'],
            sp['rw_shift'], ss['rw_shift'], sp['rw_wkv'], ss['rw_wkv'],
            sp['ml_conv'], ss['ml_conv'], sp['ml_c'], ss['ml_c'], sp['ml_n'], ss['ml_n'], sp['ml_m'], ss['ml_m'],
            sp['ffn_conv'], ss['ffn_conv'])
```

```python
import functools
import math

import jax
import jax.numpy as jnp
from jax import lax
from jax.experimental import pallas as pl
from jax.experimental.pallas import tpu as pltpu

f32 = jnp.float32
bf16 = jnp.bfloat16

D_MODEL = 2048
DEPTH = 2
PAGE_SIZE = 128
RW_W = D_MODEL // 4
RW_DH = 64
RW_H = RW_W // RW_DH
R_W = max(32, int(round(1.8 * D_MODEL ** 0.5 / 32)) * 32)
R_A = R_W
R_G = max(32, int(round(0.6 * D_MODEL ** 0.8 / 32)) * 32)
RW_COLS = 3 * RW_W + R_W + R_A + R_G
RW_LN_EPS = 64e-5
RW_DECAY_CLAMP = 0.5
ML_W = D_MODEL // 4
ML_H = 4
ML_DH = ML_W // ML_H
ML_QK = 2 * ML_W
ML_CONV = 4
ML_COLS = ML_QK + ML_W + 2 * ML_H + ML_W
ML_GN_EPS = 1e-5
FX_W = D_MODEL // 2
FX_DH = 128
FX_H = FX_W // FX_DH
FX_COLS = 3 * FX_W + FX_H
N_BRANCH = 3
GATE_COLS = N_BRANCH * D_MODEL
D_FF = ((8 * D_MODEL // 3 + 127) // 128) * 128
FFN_CONV = 3
N_MOD = 6
NORM_EPS = 1e-6

LANES = 128
SUBLANES = 8
VMEM_LIMIT = 56 * 1024 * 1024
CHUNK = 128
NEG = -1e30

RW_LORA_PAD = LANES
RW_PCOLS = 3 * RW_W + 2 * RW_LORA_PAD + R_G
OFF_RW = 0
OFF_ML = OFF_RW + RW_PCOLS
ML_PCOLS = ML_QK + 2 * ML_W
OFF_FXQ = OFF_ML + ML_PCOLS
P_COLS = OFF_FXQ + FX_W
SM_LI, SM_LF, SM_FX = 0, ML_H, 2 * ML_H
D_FF_PAD = 5632
FF_TK = 1408


def _cparams(*sem):
    return pltpu.CompilerParams(dimension_semantics=sem, vmem_limit_bytes=VMEM_LIMIT)


def _split3(x):
    hi = x.astype(bf16)
    r = x - hi.astype(f32)
    mid = r.astype(bf16)
    lo = (r - mid.astype(f32)).astype(bf16)
    return hi, mid, lo


def _dot(a, b):
    return jnp.dot(a, b, preferred_element_type=f32)


def _dot_nt(a, b):
    return lax.dot_general(a, b, (((1,), (1,)), ((), ())), preferred_element_type=f32)


def _dot_r(x, m):
    hi, mid, lo = _split3(x)
    return _dot(hi, m) + _dot(mid, m) + _dot(lo, m)


def _dot_r2(x, m):
    hi = x.astype(bf16)
    lo = (x - hi.astype(f32)).astype(bf16)
    return _dot(hi, m) + _dot(lo, m)


def _dot_l(m, x):
    hi, mid, lo = _split3(x)
    return _dot(m, hi) + _dot(m, mid) + _dot(m, lo)


def _sigmoid(x):
    return 0.5 * (jnp.tanh(0.5 * x) + 1.0)


def _log_sigmoid(x):
    return jnp.minimum(x, 0.0) - jnp.log(1.0 + jnp.exp(-jnp.abs(x)))


def _softplus(x):
    return jnp.maximum(x, 0.0) + jnp.log(1.0 + jnp.exp(-jnp.abs(x)))


def _iota(shape, dim):
    return lax.broadcasted_iota(jnp.int32, shape, dim)


def _lane_tiles(x):
    return [x[:, i:i + LANES] for i in range(0, x.shape[-1], LANES)]


def _row_max(x):
    return jnp.max(functools.reduce(jnp.maximum, _lane_tiles(x)), axis=-1, keepdims=True)


def _row_sum(x):
    return jnp.sum(functools.reduce(jnp.add, _lane_tiles(x)), axis=-1, keepdims=True)


def _softmax_step(scores, m_old, l_old):
    tiles = [_lane_tiles(s) for s in scores]
    flat = [t for ts in tiles for t in ts]
    m_new = jnp.maximum(m_old, jnp.max(functools.reduce(jnp.maximum, flat), axis=-1, keepdims=True))
    alpha = jnp.exp(m_old - m_new)
    p_tiles = [[jnp.exp(t - m_new) for t in ts] for ts in tiles]
    total = functools.reduce(jnp.add, [t for ts in p_tiles for t in ts])
    l_new = alpha * l_old + jnp.sum(total, axis=-1, keepdims=True)
    ps = [jnp.concatenate([t.astype(bf16) for t in ts], axis=1) for ts in p_tiles]
    return m_new, l_new, alpha, ps


def _group_vec(v, T, tm):
    B, D = v.shape
    if tm <= T:
        per = T // tm
        return v[:, None, :], (None, 1, D), lambda i, *_: (i // per, 0, 0)
    rows = jnp.repeat(v, T, axis=0)
    return rows.reshape(-1, tm, D), (None, tm, D), lambda i, *_: (i, 0, 0)


def _row_tile(n_rows, T, want):
    return want if T >= want else n_rows


def _ada_kernel(c_ref, w_ref, b_ref, o_ref):
    c = c_ref[...]
    a = (c * _sigmoid(c)).astype(bf16)
    o_ref[...] = _dot(a, w_ref[...].astype(bf16)) + b_ref[...]


def ada_mod(c, w_all, b_all, layer):
    M, K = c.shape
    N = w_all.shape[2]
    tn = 1024
    return pl.pallas_call(
        _ada_kernel, grid=(N // tn,),
        in_specs=[pl.BlockSpec((M, K), lambda j: (0, 0)),
                  pl.BlockSpec((None, K, tn), lambda j: (layer, 0, j)),
                  pl.BlockSpec((None, 1, tn), lambda j: (layer, 0, j))],
        out_specs=pl.BlockSpec((M, tn), lambda j: (0, j)),
        out_shape=jax.ShapeDtypeStruct((M, N), f32),
        compiler_params=_cparams("parallel"), name="ada_mod")(c, w_all, b_all)


def _norm_mod_kernel(x_ref, g_ref, sc_ref, sh_ref, o_ref):
    x = x_ref[...]
    ms = jnp.mean(x * x, axis=-1, keepdims=True)
    y = x * lax.rsqrt(ms + NORM_EPS) * g_ref[...]
    o_ref[...] = (y * (1.0 + sc_ref[...]) + sh_ref[...]).astype(o_ref.dtype)


def norm_mod(x, g, scale, shift, T):
    N, D = x.shape
    tm = _row_tile(N, T, 512)
    sc, sc_blk, sc_map = _group_vec(scale, T, tm)
    sh, _, _ = _group_vec(shift, T, tm)
    return pl.pallas_call(
        _norm_mod_kernel, grid=(N // tm,),
        in_specs=[pl.BlockSpec((tm, D), lambda i: (i, 0)),
                  pl.BlockSpec((1, D), lambda i: (0, 0)),
                  pl.BlockSpec(sc_blk, sc_map), pl.BlockSpec(sc_blk, sc_map)],
        out_specs=pl.BlockSpec((tm, D), lambda i: (i, 0)),
        out_shape=jax.ShapeDtypeStruct((N, D), bf16),
        compiler_params=_cparams("parallel"), name="norm_mod")(x, g.reshape(1, D), sc, sh)


def _mm_kernel(a_ref, w_ref, o_ref, *, gate):
    r = _dot(a_ref[...], w_ref[...])
    o_ref[...] = (_sigmoid(r) if gate else r).astype(o_ref.dtype)


def matmul(a, w, tn, out_dtype=f32, gate=False):
    M, K = a.shape
    N = w.shape[1]
    tm = min(M, 1024)
    return pl.pallas_call(
        functools.partial(_mm_kernel, gate=gate), grid=(M // tm, N // tn),
        in_specs=[pl.BlockSpec((tm, K), lambda i, j: (i, 0)),
                  pl.BlockSpec((K, tn), lambda i, j: (0, j))],
        out_specs=pl.BlockSpec((tm, tn), lambda i, j: (i, j)),
        out_shape=jax.ShapeDtypeStruct((M, N), out_dtype),
        compiler_params=_cparams("parallel", "arbitrary"), name="matmul")(a, w)


def _mm_norm_res_kernel(a_ref, w_ref, x_ref, g_ref, gate_ref, o_ref):
    f = _dot(a_ref[...].astype(bf16), w_ref[...])
    ms = jnp.mean(f * f, axis=-1, keepdims=True)
    y = f * lax.rsqrt(ms + NORM_EPS) * g_ref[...]
    o_ref[...] = x_ref[...] + gate_ref[...] * y


def mm_norm_res(a, w, x, g, gate, T):
    M, K = a.shape
    D = w.shape[1]
    tm = _row_tile(M, T, 512 if K <= D_MODEL else 256)
    gt, gt_blk, gt_map = _group_vec(gate, T, tm)
    return pl.pallas_call(
        _mm_norm_res_kernel, grid=(M // tm,),
        in_specs=[pl.BlockSpec((tm, K), lambda i: (i, 0)),
                  pl.BlockSpec((K, D), lambda i: (0, 0), pipeline_mode=pl.Buffered(1)),
                  pl.BlockSpec((tm, D), lambda i: (i, 0)),
                  pl.BlockSpec((1, D), lambda i: (0, 0)),
                  pl.BlockSpec(gt_blk, gt_map)],
        out_specs=pl.BlockSpec((tm, D), lambda i: (i, 0)),
        out_shape=jax.ShapeDtypeStruct((M, D), f32),
        compiler_params=_cparams("parallel"), name="mm_norm_res")(
            a, w, x, g.reshape(1, D), gt)


def _merge_kernel(orw_ref, oml_ref, ofx_ref, wrw_ref, wml_ref, wfx_ref, g0_ref, g1_ref, g2_ref, o_ref):
    m = g0_ref[...] * _dot(orw_ref[...].astype(bf16), wrw_ref[...])
    m += g1_ref[...] * _dot(oml_ref[...].astype(bf16), wml_ref[...])
    m += g2_ref[...] * _dot(ofx_ref[...].astype(bf16), wfx_ref[...])
    o_ref[...] = m.astype(o_ref.dtype)


def merge(o_rw, o_ml, o_fx, w_rw, w_ml, w_fx, G):
    N = o_rw.shape[0]
    D = D_MODEL
    tm = min(N, 512)
    tn = 512
    nb = D // tn
    gspec = lambda b: pl.BlockSpec((tm, tn), lambda i, j: (i, b * nb + j))
    return pl.pallas_call(
        _merge_kernel, grid=(N // tm, nb),
        in_specs=[pl.BlockSpec((tm, RW_W), lambda i, j: (i, 0)),
                  pl.BlockSpec((tm, ML_W), lambda i, j: (i, 0)),
                  pl.BlockSpec((tm, FX_W), lambda i, j: (i, 0)),
                  pl.BlockSpec((RW_W, tn), lambda i, j: (0, j)),
                  pl.BlockSpec((ML_W, tn), lambda i, j: (0, j)),
                  pl.BlockSpec((FX_W, tn), lambda i, j: (0, j)),
                  gspec(0), gspec(1), gspec(2)],
        out_specs=pl.BlockSpec((tm, tn), lambda i, j: (i, j)),
        out_shape=jax.ShapeDtypeStruct((N, D), bf16),
        compiler_params=_cparams("parallel", "arbitrary"), name="merge")(
            o_rw, o_ml, o_fx, w_rw, w_ml, w_fx, G, G, G)


def _ffn_act_kernel(a_ref, halo_ref, buf_ref, val_ref, cw_ref, cb_ref, o_ref, scr_ref, *, per, tm):
    first = (pl.program_id(0) % per) == 0
    scr_ref[0:SUBLANES, :] = jnp.where(first, buf_ref[...], halo_ref[...])
    scr_ref[SUBLANES:SUBLANES + tm, :] = a_ref[...]
    y = cb_ref[...]
    for j in range(FFN_CONV):
        off = SUBLANES - (FFN_CONV - 1) + j
        y = y + scr_ref[off:off + tm, :] * cw_ref[j:j + 1, :]
    c0 = math.sqrt(2.0 / math.pi)
    gelu = 0.5 * y * (1.0 + jnp.tanh(c0 * (y + 0.044715 * (y * y * y))))
    o_ref[...] = (gelu * val_ref[...]).astype(o_ref.dtype)


def ffn_act(av, buf8, conv_w, conv_b, T):
    N = av.shape[0]
    tm = min(T, 512)
    per = T // tm
    tn = FF_TK
    nj = D_FF_PAD // tn
    hb = tm // SUBLANES
    out_dtype = bf16 if tm % 16 == 0 else f32
    return pl.pallas_call(
        functools.partial(_ffn_act_kernel, per=per, tm=tm), grid=(N // tm, nj),
        in_specs=[pl.BlockSpec((tm, tn), lambda i, j: (i, j)),
                  pl.BlockSpec((SUBLANES, tn), lambda i, j: (jnp.maximum(i * hb - 1, 0), j)),
                  pl.BlockSpec((None, SUBLANES, tn), lambda i, j: (i // per, 0, j)),
                  pl.BlockSpec((tm, tn), lambda i, j: (i, nj + j)),
                  pl.BlockSpec((SUBLANES, tn), lambda i, j: (0, j)),
                  pl.BlockSpec((1, tn), lambda i, j: (0, j))],
        out_specs=pl.BlockSpec((tm, tn), lambda i, j: (i, j)),
        out_shape=jax.ShapeDtypeStruct((N, D_FF_PAD), out_dtype),
        scratch_shapes=[pltpu.VMEM((tm + SUBLANES, tn), f32)],
        compiler_params=_cparams("parallel", "arbitrary"), name="ffn_act")(
            av, av, buf8, av, conv_w, conv_b)


BF16_ROWS = 16


def _ffn_up_kernel(z_ref, zh_ref, wg_ref, wv_ref, buf_ref, cw_ref, cb_ref, h_ref, tail_ref, scr_ref, *, per, tm):
    first = (pl.program_id(0) % per) == 0
    wg = wg_ref[...]
    a = _dot(z_ref[...], wg)
    val = _dot(z_ref[...], wv_ref[...])
    a_prev = _dot(zh_ref[...], wg)[BF16_ROWS - SUBLANES:, :]
    scr_ref[0:SUBLANES, :] = jnp.where(first, buf_ref[...], a_prev)
    scr_ref[SUBLANES:SUBLANES + tm, :] = a
    y = cb_ref[...]
    for j in range(FFN_CONV):
        off = SUBLANES - (FFN_CONV - 1) + j
        y = y + scr_ref[off:off + tm, :] * cw_ref[j:j + 1, :]
    c0 = math.sqrt(2.0 / math.pi)
    gelu = 0.5 * y * (1.0 + jnp.tanh(c0 * (y + 0.044715 * (y * y * y))))
    h_ref[...] = (gelu * val).astype(h_ref.dtype)
    tail_ref[...] = a[tm - SUBLANES:tm, :]


def ffn_up(z, w_gv, buf8, conv_w, conv_b, T):
    N, D = z.shape
    tm = min(T, 1024)
    per = T // tm
    tn = 512
    nj = D_FF_PAD // tn
    hb = tm // BF16_ROWS
    return pl.pallas_call(
        functools.partial(_ffn_up_kernel, per=per, tm=tm), grid=(N // tm, nj),
        in_specs=[pl.BlockSpec((tm, D), lambda i, j: (i, 0)),
                  pl.BlockSpec((BF16_ROWS, D), lambda i, j: (jnp.maximum(i * hb - 1, 0), 0)),
                  pl.BlockSpec((D, tn), lambda i, j: (0, j)),
                  pl.BlockSpec((D, tn), lambda i, j: (0, nj + j)),
                  pl.BlockSpec((None, SUBLANES, tn), lambda i, j: (i // per, 0, j)),
                  pl.BlockSpec((SUBLANES, tn), lambda i, j: (0, j)),
                  pl.BlockSpec((1, tn), lambda i, j: (0, j))],
        out_specs=[pl.BlockSpec((tm, tn), lambda i, j: (i, j)),
                   pl.BlockSpec((SUBLANES, tn), lambda i, j: (i, j))],
        out_shape=[jax.ShapeDtypeStruct((N, D_FF_PAD), bf16),
                   jax.ShapeDtypeStruct((N // tm * SUBLANES, D_FF_PAD), f32)],
        scratch_shapes=[pltpu.VMEM((tm + SUBLANES, tn), f32)],
        compiler_params=_cparams("parallel", "arbitrary"), name="ffn_up")(
            z, z, w_gv, w_gv, buf8, conv_w, conv_b)


def _head_sum(x, ones_blk, terms=3):
    return _dot_r(x, ones_blk) if terms == 3 else _dot_r2(x, ones_blk)


def _rwkv_prep_kernel(p_ref, halo_ref, sp_ref, mu_ref, w0_ref, a0_ref, kk_ref, ka_ref, rk_ref,
                      wup_ref, aup_ref, gup_ref, ones_ref,
                      nkk_o, wr_o, w_o, b_o, k_o, v_o, br_o, kr_o, rkr_o, g_o, scr_ref, *, per, tm):
    first = (pl.program_id(0) % per) == 0
    scr_ref[0:SUBLANES, :] = jnp.where(first, sp_ref[...], halo_ref[...])
    p = p_ref[...]
    scr_ref[SUBLANES:SUBLANES + tm, :] = p
    prev = scr_ref[SUBLANES - 1:SUBLANES - 1 + tm, :]
    xs = p + (prev - p) * mu_ref[...]
    W = RW_W
    r, k, v = xs[:, 0:W], xs[:, W:2 * W], xs[:, 2 * W:3 * W]
    o = 3 * W
    dw = xs[:, o:o + RW_LORA_PAD]
    da = xs[:, o + RW_LORA_PAD:o + 2 * RW_LORA_PAD]
    dg = xs[:, o + 2 * RW_LORA_PAD:o + 2 * RW_LORA_PAD + R_G]
    w_raw = -_softplus(-(w0_ref[...] + _dot(jnp.tanh(dw).astype(bf16), wup_ref[...]))) - RW_DECAY_CLAMP
    w = jnp.exp(-jnp.exp(w_raw))
    a = _sigmoid(a0_ref[...] + _dot(da.astype(bf16), aup_ref[...]))
    g = _dot(_sigmoid(dg).astype(bf16), gup_ref[...])
    ones_blk = ones_ref[...]
    kk = k * kk_ref[...]
    nrm = jnp.sqrt(_head_sum(kk * kk, ones_blk))
    kk = kk / jnp.maximum(nrm, 1e-12)
    k2 = k * (1.0 + (a - 1.0) * ka_ref[...])
    b = kk * a
    nkk_o[...] = -kk
    wr_o[...] = w * r
    w_o[...] = w
    b_o[...] = b
    k_o[...] = k2
    v_o[...] = v
    br_o[...] = _head_sum(b * r, ones_blk, terms=2)
    kr_o[...] = _head_sum(k2 * r, ones_blk, terms=2)
    rkr_o[...] = _head_sum(r * k2 * rk_ref[...], ones_blk, terms=2)
    g_o[...] = g


def rwkv_prep(P, col_blk, sp8, mu, w0, a0, k_k, k_a, r_k, w_up, a_up, g_up, ones_blk, T):
    N = P.shape[0]
    tm = min(T, 256)
    per = T // tm
    hb = tm // SUBLANES
    C = RW_PCOLS
    vec = lambda n: pl.BlockSpec((1, n), lambda i: (0, 0))
    full = lambda a: pl.BlockSpec(a.shape, lambda i: (0, 0))
    out = jax.ShapeDtypeStruct((N, RW_W), f32)
    ospec = pl.BlockSpec((tm, RW_W), lambda i: (i, 0))
    return pl.pallas_call(
        functools.partial(_rwkv_prep_kernel, per=per, tm=tm), grid=(N // tm,),
        in_specs=[pl.BlockSpec((tm, C), lambda i: (i, col_blk)),
                  pl.BlockSpec((SUBLANES, C), lambda i: (jnp.maximum(i * hb - 1, 0), col_blk)),
                  pl.BlockSpec((None, SUBLANES, C), lambda i: (i // per, 0, 0)),
                  vec(C), vec(RW_W), vec(RW_W), vec(RW_W), vec(RW_W), vec(RW_W),
                  full(w_up), full(a_up), full(g_up), full(ones_blk)],
        out_specs=[ospec] * 10, out_shape=[out] * 10,
        scratch_shapes=[pltpu.VMEM((tm + SUBLANES, C), f32)],
        compiler_params=_cparams("parallel"), name="rwkv_prep")(
            P, P, sp8, mu, w0, a0, k_k, k_a, r_k, w_up, a_up, g_up, ones_blk)


N_PAIR = RW_H // 2


def _rwkv_scan_kernel(nkk_ref, wr_ref, w_ref, b_ref, k_ref, br_ref, v_ref, s0_ref, ones_ref,
                      z_ref, so_ref, S_ref, vT_ref, *, n_sub, n_steps, nb):
    c = pl.program_id(1)
    chains = [(bb, j) for bb in range(nb) for j in range(N_PAIR)]

    @pl.when(c == 0)
    def _():
        S_ref[...] = s0_ref[...]

    if n_steps < CHUNK:
        z_ref[...] = jnp.zeros_like(z_ref)

    lo = _iota((1, LANES), 1) < RW_DH
    lane = _iota((1, LANES), 1)
    diag = _iota((RW_DH, LANES), 1) % RW_DH == _iota((RW_DH, LANES), 0)
    pair_ones = ones_ref[...]

    def as_row(x):
        return jnp.sum(jnp.where(diag, x, 0.0), axis=0, keepdims=True)

    for sub in range(n_sub):
        r0 = sub * CHUNK
        for q, (bb, j) in enumerate(chains):
            vT_ref[q] = v_ref[bb, r0:r0 + CHUNK, j * LANES:(j + 1) * LANES].T

        def group(g, carry):
            base = pl.multiple_of(r0 + g * SUBLANES, SUBLANES)
            tiles = [[ref[bb, pl.ds(base, SUBLANES), j * LANES:(j + 1) * LANES]
                      for ref in (nkk_ref, wr_ref, w_ref, b_ref, k_ref, br_ref)] for bb, j in chains]
            z_rows = [[] for _ in chains]
            for i in range(SUBLANES):
                tmask = lane == g * SUBLANES + i
                for q in range(len(chains)):
                    nkk_r, wr_r, w_r, b_r, k_r, br_r = [tl[i:i + 1, :] for tl in tiles[q]]
                    S = S_ref[q]
                    xa = S * nkk_r
                    hi = jnp.concatenate([xa, S * wr_r], axis=0).astype(bf16)
                    low = (xa - hi[0:RW_DH].astype(f32)).astype(bf16)
                    red = _dot(jnp.concatenate([hi, low], axis=0), pair_ones)
                    sa = red[0:RW_DH] + red[2 * RW_DH:3 * RW_DH]
                    zz = red[RW_DH:2 * RW_DH] + sa * br_r
                    va = jnp.sum(jnp.where(tmask, vT_ref[q, 0:RW_DH, :], 0.0), axis=-1, keepdims=True)
                    vb = jnp.sum(jnp.where(tmask, vT_ref[q, RW_DH:2 * RW_DH, :], 0.0), axis=-1, keepdims=True)
                    vp = jnp.where(lo, va, vb)
                    S_ref[q] = S * w_r + sa * b_r + vp * k_r
                    z_rows[q].append(as_row(zz))
            for q, (bb, j) in enumerate(chains):
                cs = slice(j * LANES, (j + 1) * LANES)
                z_ref[bb, pl.ds(base, SUBLANES), cs] = jnp.concatenate(z_rows[q], axis=0)
            return carry

        lax.fori_loop(0, n_steps // SUBLANES, group, 0)

    so_ref[...] = S_ref[...]


def rwkv_scan(nkk, wr, w, b, k, br, v, s0, pair_ones, n_steps):
    B, Tp, W = nkk.shape
    nb = 2 if B % 2 == 0 else 1
    tc = min(Tp, 4 * CHUNK)
    n_sub = tc // CHUNK
    rows = pl.BlockSpec((None, nb, tc, W), lambda bi, c: (bi, 0, c, 0))
    st = pl.BlockSpec((None, nb * N_PAIR, RW_DH, LANES), lambda bi, c: (bi, 0, 0, 0))
    grp = lambda a: a.reshape(B // nb, nb, Tp, W)
    z, so = pl.pallas_call(
        functools.partial(_rwkv_scan_kernel, n_sub=n_sub, n_steps=min(n_steps, CHUNK), nb=nb),
        grid=(B // nb, Tp // tc),
        in_specs=[rows] * 7 + [st, pl.BlockSpec((LANES, LANES), lambda bi, c: (0, 0))],
        out_specs=[rows, st],
        out_shape=[jax.ShapeDtypeStruct((B // nb, nb, Tp, W), f32),
                   jax.ShapeDtypeStruct((B // nb, nb * N_PAIR, RW_DH, LANES), f32)],
        scratch_shapes=[pltpu.VMEM((nb * N_PAIR, RW_DH, LANES), f32), pltpu.VMEM((nb * N_PAIR, LANES, LANES), f32)],
        compiler_params=_cparams("parallel", "arbitrary"), name="rwkv_scan")(
            grp(nkk), grp(wr), grp(w), grp(b), grp(k), grp(br), grp(v),
            s0.reshape(B // nb, nb * N_PAIR, RW_DH, LANES), pair_ones)
    return z.reshape(B, Tp, W), so.reshape(s0.shape)


def _rwkv_post_kernel(z_ref, v_ref, kr_ref, rkr_ref, g_ref, lnw_ref, lnb_ref, ones_ref, o_ref):
    ones_blk = ones_ref[...]
    v = v_ref[...]
    y = z_ref[...] + v * kr_ref[...]
    mu = _head_sum(y, ones_blk, terms=2) * (1.0 / RW_DH)
    yc = y - mu
    var = _head_sum(yc * yc, ones_blk, terms=2) * (1.0 / RW_DH)
    yn = yc * lax.rsqrt(var + RW_LN_EPS) * lnw_ref[...] + lnb_ref[...]
    o_ref[...] = ((yn + rkr_ref[...] * v) * g_ref[...]).astype(o_ref.dtype)


def rwkv_post(z, v, kr, rkr, g, ln_w, ln_b, ones_blk):
    N, W = z.shape
    tm = min(N, 512)
    rows = pl.BlockSpec((tm, W), lambda i: (i, 0))
    vec = pl.BlockSpec((1, W), lambda i: (0, 0))
    return pl.pallas_call(
        _rwkv_post_kernel, grid=(N // tm,),
        in_specs=[rows] * 5 + [vec, vec, pl.BlockSpec(ones_blk.shape, lambda i: (0, 0))],
        out_specs=rows, out_shape=jax.ShapeDtypeStruct((N, W), bf16),
        compiler_params=_cparams("parallel"), name="rwkv_post")(z, v, kr, rkr, g, ln_w, ln_b, ones_blk)


def _mlstm_kernel(main_ref, halo_ref, buf_ref, sm_ref, bias_ref, cw_ref, cb_ref, gn_ref, tri_ref,
                  c0_ref, n0_ref, m0_ref,
                  o_ref, lf_o, F_o, FT_o, c_o, n_o, m_o,
                  scr_ref, ct_ref, n_ref, m_ref, carry_ref, *, t_real):
    c = pl.program_id(1)
    L = CHUNK

    @pl.when(c == 0)
    def _():
        ct_ref[...] = c0_ref[...]
        n_ref[...] = n0_ref[...]
        m_ref[...] = m0_ref[...]
        carry_ref[...] = jnp.zeros_like(carry_ref)

    main = main_ref[...]
    scr_ref[0:SUBLANES, :] = jnp.where(c == 0, buf_ref[...], halo_ref[...])
    scr_ref[SUBLANES:SUBLANES + L, :] = main[:, 0:ML_QK]
    qk = cb_ref[...]
    for j in range(ML_CONV):
        off = SUBLANES - (ML_CONV - 1) + j
        qk = qk + scr_ref[off:off + L, :] * cw_ref[j:j + 1, :]
    qk = qk * _sigmoid(qk)
    q_all = qk[:, 0:ML_W]
    k_all = qk[:, ML_W:ML_QK] * (ML_DH ** -0.5)
    v_all = main[:, ML_QK:ML_QK + ML_W]
    og_all = main[:, ML_QK + ML_W:ML_QK + 2 * ML_W]

    valid = (c * L + _iota((L, 1), 0)) < t_real
    pre = sm_ref[...] + bias_ref[...]
    li_all = jnp.where(valid, pre, NEG)
    lf_all = jnp.where(valid, _log_sigmoid(pre), 0.0)
    cum = _dot_l(tri_ref[...], lf_all)
    F = cum + carry_ref[...]
    carry_ref[...] = F[L - 1:L, :]
    lf_o[...] = lf_all
    F_o[...] = F
    FT_o[...] = F.T
    liT = li_all.T
    cumT = cum.T

    row = _iota((L, L), 0)
    col = _iota((L, L), 1)
    causal = col <= row
    for h in range(ML_H):
        hs = slice(h * ML_DH, (h + 1) * ML_DH)
        b_col = cum[:, SM_LF + h:SM_LF + h + 1]
        b_row = cumT[SM_LF + h:SM_LF + h + 1, :]
        li_col = li_all[:, SM_LI + h:SM_LI + h + 1]
        li_row = liT[SM_LI + h:SM_LI + h + 1, :]
        m_prev = m_ref[h][:, 0:1]
        d = jnp.where(causal, b_col - b_row + li_row, NEG)
        inter = b_col + m_prev
        m_t = jnp.maximum(inter, jnp.max(d, axis=-1, keepdims=True))
        w_intra = jnp.exp(d - m_t)
        w_state = jnp.exp(inter - m_t)
        q = q_all[:, hs]
        k = k_all[:, hs]
        v = v_all[:, hs]
        qb, kb = q.astype(bf16), k.astype(bf16)
        s = _dot_nt(qb, kb) * w_intra
        ct = ct_ref[h]
        num = w_state * _dot(qb, ct.astype(bf16)) + _dot(s.astype(bf16), v.astype(bf16))
        n_row = n_ref[h]
        den = w_state * jnp.sum(q * n_row, axis=-1, keepdims=True) + jnp.sum(s, axis=-1, keepdims=True)
        hh = num / jnp.maximum(jnp.abs(den), jnp.exp(-m_t))
        g_end = b_col[L - 1:L, :]
        lw_s = g_end - b_col + li_col
        m_new = jnp.maximum(g_end + m_prev, jnp.max(lw_s, axis=0, keepdims=True))
        w_s = jnp.exp(lw_s - m_new)
        decay = jnp.exp(g_end + m_prev - m_new)
        ct_ref[h] = decay * ct + _dot(kb.T, (w_s * v).astype(bf16))
        n_ref[h] = decay * n_row + jnp.sum(w_s * k, axis=0, keepdims=True)
        m_ref[h] = jnp.broadcast_to(m_new, (1, LANES))
        mu = jnp.mean(hh, axis=-1, keepdims=True)
        hc = hh - mu
        var = jnp.mean(hc * hc, axis=-1, keepdims=True)
        hn = hc * lax.rsqrt(var + ML_GN_EPS) * gn_ref[:, hs]
        o_ref[:, hs] = (_sigmoid(og_all[:, hs]) * hn).astype(o_ref.dtype)

    c_o[...] = ct_ref[...]
    n_o[...] = n_ref[...]
    m_o[...] = m_ref[...]


def mlstm_fox_prep(main_arr, main_blk, small_arr, small_blk, buf8, bias_row, conv_w, conv_b, gn_w, tri,
                   c0t, n0, m0, B, Tp, t_real):
    L = CHUNK
    nc = Tp // L
    hb = L // SUBLANES
    st_c = pl.BlockSpec((None, ML_H, ML_DH, ML_DH), lambda b, c: (b, 0, 0, 0))
    st_n = pl.BlockSpec((None, ML_H, 1, ML_DH), lambda b, c: (b, 0, 0, 0))
    rows = lambda w: pl.BlockSpec((L, w), lambda b, c: (b * nc + c, 0))
    vec = lambda n: pl.BlockSpec((1, n), lambda b, c: (0, 0))
    N = B * Tp
    return pl.pallas_call(
        functools.partial(_mlstm_kernel, t_real=t_real), grid=(B, nc),
        in_specs=[pl.BlockSpec((L, ML_PCOLS), lambda b, c: (b * nc + c, main_blk)),
                  pl.BlockSpec((SUBLANES, ML_QK), lambda b, c: (jnp.maximum((b * nc + c) * hb - 1, 0), 2 * main_blk)),
                  pl.BlockSpec((None, SUBLANES, ML_QK), lambda b, c: (b, 0, 0)),
                  pl.BlockSpec((L, LANES), lambda b, c: (b * nc + c, small_blk)),
                  vec(LANES),
                  pl.BlockSpec((SUBLANES, ML_QK), lambda b, c: (0, 0)),
                  vec(ML_QK), vec(ML_W),
                  pl.BlockSpec((L, L), lambda b, c: (0, 0)),
                  st_c, st_n, st_n],
        out_specs=[rows(ML_W), rows(LANES), rows(LANES),
                   pl.BlockSpec((None, LANES, L), lambda b, c: (b, 0, c)),
                   st_c, st_n, st_n],
        out_shape=[jax.ShapeDtypeStruct((N, ML_W), bf16),
                   jax.ShapeDtypeStruct((N, LANES), f32),
                   jax.ShapeDtypeStruct((N, LANES), f32),
                   jax.ShapeDtypeStruct((B, LANES, Tp), f32),
                   jax.ShapeDtypeStruct((B, ML_H, ML_DH, ML_DH), f32),
                   jax.ShapeDtypeStruct((B, ML_H, 1, ML_DH), f32),
                   jax.ShapeDtypeStruct((B, ML_H, 1, ML_DH), f32)],
        scratch_shapes=[pltpu.VMEM((L + SUBLANES, ML_QK), f32),
                        pltpu.VMEM((ML_H, ML_DH, ML_DH), f32),
                        pltpu.VMEM((ML_H, 1, ML_DH), f32),
                        pltpu.VMEM((ML_H, 1, LANES), f32),
                        pltpu.VMEM((1, LANES), f32)],
        compiler_params=_cparams("parallel", "arbitrary"), name="mlstm")(
            main_arr, main_arr, buf8, small_arr, bias_row, conv_w, conv_b, gn_w, tri, c0t, n0, m0)


FOX_HG = 4


def _fox_attn_kernel(q_ref, k_ref, v_ref, ft_ref, o_ref, m_ref, l_ref, acc_ref, *, tq, tk):
    h0 = pl.program_id(1) * FOX_HG
    qi = pl.program_id(2)
    m_ref[...] = jnp.full_like(m_ref, NEG)
    l_ref[...] = jnp.zeros_like(l_ref)
    acc_ref[...] = jnp.zeros_like(acc_ref)
    qbs = [(q_ref[:, u * FX_DH:(u + 1) * FX_DH] * (FX_DH ** -0.5)).astype(bf16) for u in range(FOX_HG)]

    def block(j, masked):
        k0 = pl.multiple_of(j * tk, tk)
        for u in range(FOX_HG):
            hs = slice(u * FX_DH, (u + 1) * FX_DH)
            kb = k_ref[pl.ds(k0, tk), hs].astype(bf16)
            vb = v_ref[pl.ds(k0, tk), hs].astype(bf16)
            s = _dot_nt(qbs[u], kb) - ft_ref[pl.ds(h0 + u, 1), pl.ds(k0, tk)]
            if masked:
                s = jnp.where(_iota((tq, tk), 1) + k0 <= qi * tq + _iota((tq, tk), 0), s, NEG)
            m_new, l_new, alpha, (p,) = _softmax_step([s], m_ref[u], l_ref[u])
            acc_ref[:, hs] = alpha * acc_ref[:, hs] + _dot(p, vb)
            m_ref[u] = m_new
            l_ref[u] = l_new

    n_full = (qi * tq) // tk

    def body(j, carry):
        block(j, False)
        return carry

    lax.fori_loop(0, n_full, body, 0)
    for d in range(max(tq // tk, 1)):
        block(n_full + d, True)
    for u in range(FOX_HG):
        hs = slice(u * FX_DH, (u + 1) * FX_DH)
        o_ref[:, hs] = (acc_ref[:, hs] / l_ref[u]).astype(o_ref.dtype)


def fox_attn(P, Kx, Vx, FT, B, T):
    tq = 256
    tk = 512
    nq = T // tq
    wb = FOX_HG * FX_DH
    qb = OFF_FXQ // wb
    return pl.pallas_call(
        functools.partial(_fox_attn_kernel, tq=tq, tk=tk), grid=(B, FX_H // FOX_HG, nq),
        in_specs=[pl.BlockSpec((tq, wb), lambda b, h, i: (b * nq + i, qb + h)),
                  pl.BlockSpec((T, wb), lambda b, h, i: (b, h)),
                  pl.BlockSpec((T, wb), lambda b, h, i: (b, h)),
                  pl.BlockSpec((None, SUBLANES, T), lambda b, h, i: (b, SM_FX // SUBLANES, 0))],
        out_specs=pl.BlockSpec((tq, wb), lambda b, h, i: (b * nq + i, h)),
        out_shape=jax.ShapeDtypeStruct((B * T, FX_W), bf16),
        scratch_shapes=[pltpu.VMEM((FOX_HG, tq, LANES), f32), pltpu.VMEM((FOX_HG, tq, LANES), f32),
                        pltpu.VMEM((tq, wb), f32)],
        compiler_params=_cparams("parallel", "parallel", "arbitrary"), name="fox_attn")(P, Kx, Vx, FT)


def _fox_suffix_kernel(pt_ref, lf_ref, triu_ref, later_ref, o_ref, x_ref, *, n_pages):
    b = pl.program_id(0)

    def gather(j, carry):
        x_ref[pl.ds(pl.multiple_of(j * FX_H, FX_H), FX_H), :] = lf_ref[pt_ref[b, j]]
        return carry

    lax.fori_loop(0, n_pages, gather, 0)
    x = x_ref[...]
    within = _dot_r(x, triu_ref[...])
    tot = jnp.broadcast_to(jnp.sum(x, axis=-1, keepdims=True), x.shape)
    o_ref[...] = within + _dot_l(later_ref[...], tot)


def fox_suffix(page_table, cache_lft, triu, later, layer):
    B, n_pages = page_table.shape
    n_pool = cache_lft.shape[1]
    rows = n_pages * FX_H
    grid_spec = pltpu.PrefetchScalarGridSpec(
        num_scalar_prefetch=1, grid=(B,),
        in_specs=[pl.BlockSpec((None, n_pool, FX_H, PAGE_SIZE), lambda b, pt: (layer, 0, 0, 0)),
                  pl.BlockSpec((PAGE_SIZE, PAGE_SIZE), lambda b, pt: (0, 0)),
                  pl.BlockSpec((rows, rows), lambda b, pt: (0, 0))],
        out_specs=pl.BlockSpec((None, rows, PAGE_SIZE), lambda b, pt: (b, 0, 0)),
        scratch_shapes=[pltpu.VMEM((rows, PAGE_SIZE), f32)])
    return pl.pallas_call(
        functools.partial(_fox_suffix_kernel, n_pages=n_pages), grid_spec=grid_spec,
        out_shape=jax.ShapeDtypeStruct((B, rows, PAGE_SIZE), f32),
        compiler_params=_cparams("parallel"), name="fox_suffix")(page_table, cache_lft, triu, later)


DEC_PP = 8


def _fox_decode_kernel(pt_ref, q_ref, *refs, n_steps, pp, t_new):
    kc = refs[0:pp]
    vc = refs[pp:2 * pp]
    suf_ref, hmask_ref, kn_ref, vn_ref, fkn_ref, o_ref, m_ref, l_ref, acc_ref = refs[2 * pp:]
    j = pl.program_id(1)
    nq = FX_H * t_new
    cols = PAGE_SIZE * FX_H

    @pl.when(j == 0)
    def _():
        m_ref[...] = jnp.full_like(m_ref, NEG)
        l_ref[...] = jnp.zeros_like(l_ref)
        acc_ref[...] = jnp.zeros_like(acc_ref)

    qb = (q_ref[...] * (FX_DH ** -0.5)).astype(bf16)

    def update(ss, vs):
        m_new, l_new, alpha, ps = _softmax_step(ss, m_ref[...], l_ref[...])
        acc = alpha * acc_ref[...]
        for p, vb in zip(ps, vs):
            acc = acc + _dot(p, vb)
        l_ref[...] = l_new
        acc_ref[...] = acc
        m_ref[...] = m_new

    @pl.when(j < n_steps)
    def _():
        hmask = hmask_ref[...]
        ss = [_dot_nt(qb, kc[i][...].reshape(cols, FX_DH).astype(bf16)) + hmask + suf_ref[i] for i in range(pp)]
        update(ss, [vc[i][...].reshape(cols, FX_DH).astype(bf16) for i in range(pp)])

    @pl.when(j == n_steps)
    def _():
        s = _dot_nt(qb, kn_ref[...].astype(bf16)) - fkn_ref[...]
        col = _iota((nq, LANES), 1)
        row = _iota((nq, LANES), 0)
        ok = (col % FX_H == row // t_new) & (col // FX_H <= row % t_new)
        update([jnp.where(ok, s, NEG)], [vn_ref[...].astype(bf16)])
        o_ref[...] = acc_ref[...] / l_ref[...]


def fox_decode(page_table, q2, cache_k, cache_v, suffix, hmask, k_new, v_new, fkn, layer, t_new):
    B, n_pages = page_table.shape
    pp = DEC_PP
    n_steps = n_pages // pp
    nq = FX_H * t_new
    cols = PAGE_SIZE * FX_H

    def page_spec(i):
        return pl.BlockSpec((None, None, PAGE_SIZE, FX_H, FX_DH),
                            lambda b, j, pt: (layer, pt[b, jnp.minimum(j, n_steps - 1) * pp + i], 0, 0, 0))

    per_b = lambda shape: pl.BlockSpec((None,) + shape, lambda b, j, pt: (b, 0, 0))
    grid_spec = pltpu.PrefetchScalarGridSpec(
        num_scalar_prefetch=1, grid=(B, n_steps + 1),
        in_specs=[per_b((nq, FX_DH))] + [page_spec(i) for i in range(pp)] * 2
        + [pl.BlockSpec((None, pp, 1, cols), lambda b, j, pt: (b, jnp.minimum(j, n_steps - 1), 0, 0)),
           pl.BlockSpec((nq, cols), lambda b, j, pt: (0, 0)),
           per_b((LANES, FX_DH)), per_b((LANES, FX_DH)), per_b((1, LANES))],
        out_specs=per_b((nq, FX_DH)),
        scratch_shapes=[pltpu.VMEM((nq, LANES), f32), pltpu.VMEM((nq, LANES), f32), pltpu.VMEM((nq, FX_DH), f32)])
    return pl.pallas_call(
        functools.partial(_fox_decode_kernel, n_steps=n_steps, pp=pp, t_new=t_new), grid_spec=grid_spec,
        out_shape=jax.ShapeDtypeStruct((B, nq, FX_DH), f32),
        compiler_params=_cparams("parallel", "arbitrary"), name="fox_decode")(
            page_table, q2, *([cache_k] * pp), *([cache_v] * pp), suffix, hmask, k_new, v_new, fkn)


def _pad_cols(w, n):
    return jnp.pad(w, ((0, 0), (0, n - w.shape[1])))


def _rw_cols(x):
    W = RW_W
    pad = [(0, 0)] * (x.ndim - 1) + [(0, RW_LORA_PAD - R_W)]
    return jnp.concatenate([x[..., 0:3 * W], jnp.pad(x[..., 3 * W:3 * W + R_W], pad),
                            jnp.pad(x[..., 3 * W + R_W:3 * W + R_W + R_A], pad),
                            x[..., 3 * W + R_W + R_A:]], axis=-1)


def _rw_cols_inv(x):
    W = RW_W
    o = 3 * W
    return jnp.concatenate([x[..., 0:o], x[..., o:o + R_W], x[..., o + RW_LORA_PAD:o + RW_LORA_PAD + R_A],
                            x[..., o + 2 * RW_LORA_PAD:]], axis=-1)


def _w_in_split(w):
    o_ml = RW_COLS
    o_fx = o_ml + ML_COLS
    o_gt = o_fx + FX_COLS
    ml = w[:, o_ml:o_fx]
    fx = w[:, o_fx:o_gt]
    ml_main = jnp.concatenate([ml[:, 0:ML_QK + ML_W], ml[:, ML_QK + ML_W + 2 * ML_H:]], axis=1)
    small = _pad_cols(jnp.concatenate([ml[:, ML_QK + ML_W:ML_QK + ML_W + 2 * ML_H], fx[:, 3 * FX_W:]], axis=1), LANES)
    main = jnp.concatenate([_rw_cols(w[:, 0:RW_COLS]), ml_main, fx[:, 0:FX_W]], axis=1)
    return (main.astype(bf16), w[:, o_gt:].astype(bf16), fx[:, FX_W:2 * FX_W].astype(bf16),
            fx[:, 2 * FX_W:3 * FX_W].astype(bf16), small.astype(bf16))


def _pad_rows(w, n):
    return jnp.pad(w, ((0, n - w.shape[0]), (0, 0)))


def _prev_rows8(buf, width):
    B, r, C = buf.shape
    return jnp.pad(buf, ((0, 0), (SUBLANES - r, 0), (0, width - C)))


def _layer(x, mod, st, past, W, page_table, l, B, T):
    N = B * T
    shift1, scale1, gate1, shift2, scale2, gate2 = jnp.split(mod, N_MOD, axis=-1)
    u = norm_mod(x, W['norm_pre_mix'], scale1, shift1, T)
    P = matmul(u, W['w_in'], tn=1024)
    G = matmul(u, W['w_in_gate'], tn=1024, out_dtype=bf16, gate=True)
    Kx = matmul(u, W['w_in_k'], tn=FX_W)
    Vx = matmul(u, W['w_in_v'], tn=FX_W)
    SM = matmul(u, W['w_in_small'], tn=LANES)

    padded = T % CHUNK != 0
    Tp = T if not padded else CHUNK

    def pad_t(a):
        if not padded:
            return a
        return jnp.pad(a.reshape(B, T, -1), ((0, 0), (0, Tp - T), (0, 0))).reshape(B * Tp, -1)

    def unpad_t(a):
        if not padded:
            return a
        return a.reshape(B, Tp, -1)[:, :T].reshape(N, -1)

    sp8 = _prev_rows8(_rw_cols(st['rw_shift'])[:, None, :], RW_PCOLS)
    prep = rwkv_prep(P, OFF_RW // RW_PCOLS, sp8, W['rw_mu'], W['rw_w0'], W['rw_a0'], W['rw_k_k'], W['rw_k_a'],
                     W['rw_r_k'], W['rw_w_up'], W['rw_a_up'], W['rw_g_up'], W['ones_blk'], T)
    nkk, wr, w_, b_, k2, v_rw, br, kr, rkr, g_rw = prep
    s0 = st['rw_wkv'].reshape(B, N_PAIR, 2, RW_DH, RW_DH).transpose(0, 1, 3, 2, 4).reshape(B, N_PAIR, RW_DH, LANES)
    seq = [pad_t(a).reshape(B, Tp, RW_W) for a in (nkk, wr, w_, b_, k2, br, v_rw)]
    z, s_out = rwkv_scan(*seq, s0, W['ones_blk'][0:LANES, 0:LANES], T)
    z = unpad_t(z.reshape(B * Tp, RW_W))
    o_rw = rwkv_post(z, v_rw, kr, rkr, g_rw, W['rw_ln_w'], W['rw_ln_b'], W['ones_blk'])
    rw_wkv = s_out.reshape(B, N_PAIR, RW_DH, 2, RW_DH).transpose(0, 1, 3, 2, 4).reshape(B, RW_H, RW_DH, RW_DH)
    rw_shift = _rw_cols_inv(P[:, OFF_RW:OFF_RW + RW_PCOLS].reshape(B, T, RW_PCOLS)[:, -1])

    buf8 = _prev_rows8(st['ml_conv'], ML_QK)
    c0t = jnp.swapaxes(st['ml_c'], -1, -2)
    n0 = st['ml_n'][:, :, None, :]
    m0 = jnp.broadcast_to(st['ml_m'][:, :, None, None], (B, ML_H, 1, LANES))
    if padded:
        main_arr, main_blk = pad_t(P[:, OFF_ML:OFF_ML + ML_PCOLS]), 0
        small_arr, small_blk = pad_t(SM), 0
    else:
        main_arr, main_blk, small_arr, small_blk = P, OFF_ML // ML_PCOLS, SM, 0
    o_ml, lf_all, F, FT, ct, n_new, m_new = mlstm_fox_prep(
        main_arr, main_blk, small_arr, small_blk, buf8, W['small_bias'], W['ml_conv_w'], W['ml_conv_b'],
        W['ml_gn_w'], W['tri'], c0t, n0, m0, B, Tp, T)
    o_ml = unpad_t(o_ml)
    ml_c = jnp.swapaxes(ct, -1, -2)
    ml_n = n_new[:, :, 0, :]
    ml_m = m_new[:, :, 0, 0]
    ml_conv = P[:, OFF_ML:OFF_ML + ML_QK].reshape(B, T, ML_QK)[:, T - (ML_CONV - 1):]
    fox_logf = unpad_t(lf_all)[:, SM_FX:SM_FX + FX_H].reshape(B, T, FX_H)
    fox_k = Kx.reshape(B, T, FX_H, FX_DH)
    fox_v = Vx.reshape(B, T, FX_H, FX_DH)

    if past is None:
        o_fx = fox_attn(P, Kx, Vx, FT, B, T)
    else:
        cache_k, cache_v, cache_lft = past
        n_pages = page_table.shape[1]
        nq = FX_H * T
        assert nq <= LANES and n_pages % DEC_PP == 0
        suffix = fox_suffix(page_table, cache_lft, W['triu'], W['later'], l)
        suffix = suffix.reshape(B, n_pages, FX_H, PAGE_SIZE).transpose(0, 1, 3, 2).reshape(
            B, n_pages, 1, PAGE_SIZE * FX_H)
        q2 = P[:, OFF_FXQ:OFF_FXQ + FX_W].reshape(B, T, FX_H, FX_DH).transpose(0, 2, 1, 3).reshape(B, nq, FX_DH)
        padk = lambda a: jnp.pad(a.reshape(B, nq, FX_DH), ((0, 0), (0, LANES - nq), (0, 0)))
        fkn = F.reshape(B, Tp, LANES)[:, :T, SM_FX:SM_FX + FX_H].reshape(B, 1, nq)
        fkn = jnp.pad(fkn, ((0, 0), (0, 0), (0, LANES - nq)))
        hmask = jnp.where(jnp.arange(PAGE_SIZE * FX_H)[None, :] % FX_H == jnp.arange(nq)[:, None] // T, 0.0, NEG)
        o = fox_decode(page_table, q2, cache_k, cache_v, suffix, hmask.astype(f32), padk(fox_k), padk(fox_v),
                       fkn, l, T)
        o_fx = o.reshape(B, FX_H, T, FX_DH).transpose(0, 2, 1, 3).reshape(N, FX_W)

    merged = merge(o_rw, o_ml, o_fx, W['w_br_rwkv'], W['w_br_mlstm'], W['w_br_fox'], G)
    x = mm_norm_res(merged, W['w_out'], x, W['norm_post_mix'], gate1, T)

    zf = norm_mod(x, W['norm_pre_ffn'], scale2, shift2, T)
    fbuf8 = _prev_rows8(st['ffn_conv'], D_FF_PAD)
    if T % BF16_ROWS == 0:
        hmid, tails = ffn_up(zf, W['ffn_w_gv'], fbuf8, W['ffn_conv_w'], W['ffn_conv_b'], T)
        per = N // B // min(T, 1024)
        ffn_conv = tails.reshape(B, per, SUBLANES, D_FF_PAD)[:, -1, SUBLANES - (FFN_CONV - 1):, 0:D_FF]
    else:
        av = matmul(zf, W['ffn_w_gv'], tn=1024)
        hmid = ffn_act(av, fbuf8, W['ffn_conv_w'], W['ffn_conv_b'], T)
        ffn_conv = av[:, 0:D_FF].reshape(B, T, D_FF)[:, T - (FFN_CONV - 1):]
    x = mm_norm_res(hmid, W['ffn_w_down'], x, W['norm_post_ffn'], gate2, T)

    new = dict(fox_k=fox_k, fox_v=fox_v, fox_logf=fox_logf, rw_shift=rw_shift, rw_wkv=rw_wkv,
               ml_conv=ml_conv, ml_c=ml_c, ml_n=ml_n, ml_m=ml_m, ffn_conv=ffn_conv)
    return x, new


STATE_NAMES = ("fox_k", "fox_v", "fox_logf", "rw_shift", "rw_wkv", "ml_conv", "ml_c", "ml_n", "ml_m", "ffn_conv")


def _layer_weights(Pm, l, n_pages):
    row = lambda v: v.reshape(1, -1)
    W = {}
    for name in ('norm_pre_mix', 'norm_post_mix', 'norm_pre_ffn', 'norm_post_ffn'):
        W[name] = Pm[name][l]
    W['w_in'], W['w_in_gate'], W['w_in_k'], W['w_in_v'], W['w_in_small'] = _w_in_split(Pm['w_in'][l])
    W['rw_mu'] = row(_rw_cols(Pm['rw_mu'][l]))
    for name in ('rw_w0', 'rw_a0', 'rw_k_k', 'rw_k_a', 'rw_r_k', 'rw_ln_w', 'rw_ln_b'):
        W[name] = row(Pm[name][l])
    W['rw_w_up'] = _pad_rows(Pm['rw_w_up'][l], RW_LORA_PAD).astype(bf16)
    W['rw_a_up'] = _pad_rows(Pm['rw_a_up'][l], RW_LORA_PAD).astype(bf16)
    W['rw_g_up'] = Pm['rw_g_up'][l].astype(bf16)
    hid = jnp.arange(RW_W) // RW_DH
    W['ones_blk'] = (hid[:, None] == hid[None, :]).astype(bf16)
    W['small_bias'] = row(jnp.pad(jnp.concatenate([Pm['ml_b_i'][l], Pm['ml_b_f'][l], Pm['fx_b_f'][l]]),
                                  (0, LANES - 2 * ML_H - FX_H)))
    W['ml_conv_w'] = _pad_rows(Pm['ml_conv_w'][l], SUBLANES)
    W['ml_conv_b'] = row(Pm['ml_conv_b'][l])
    W['ml_gn_w'] = row(Pm['ml_gn_w'][l])
    idx = jnp.arange(CHUNK)
    W['tri'] = (idx[None, :] <= idx[:, None]).astype(bf16)
    W['triu'] = (idx[:, None] > idx[None, :]).astype(bf16)
    if n_pages:
        r = jnp.arange(n_pages * FX_H)
        W['later'] = ((r[:, None] % FX_H == r[None, :] % FX_H) & (r[None, :] // FX_H > r[:, None] // FX_H)).astype(bf16)
    W['w_br_rwkv'] = Pm['w_br_rwkv'][l].astype(bf16)
    W['w_br_mlstm'] = Pm['w_br_mlstm'][l].astype(bf16)
    W['w_br_fox'] = Pm['w_br_fox'][l].astype(bf16)
    W['w_out'] = Pm['w_out'][l].astype(bf16)
    W['ffn_w_gv'] = jnp.concatenate([_pad_cols(Pm['ffn_w_gate'][l], D_FF_PAD),
                                     _pad_cols(Pm['ffn_w_val'][l], D_FF_PAD)], axis=1).astype(bf16)
    W['ffn_conv_w'] = _pad_rows(_pad_cols(Pm['ffn_conv_w'][l], D_FF_PAD), SUBLANES)
    W['ffn_conv_b'] = _pad_cols(row(Pm['ffn_conv_b'][l]), D_FF_PAD)
    W['ffn_w_down'] = _pad_rows(Pm['ffn_w_down'][l], D_FF_PAD).astype(bf16)
    return W


def kernel(x_prompt, x_sample, cache_fox_k, cache_fox_v, cache_fox_logf, state_rwkv_shift, state_rwkv_wkv,
           state_mlstm_conv, state_mlstm_c, state_mlstm_n, state_mlstm_m, state_ffn_conv, page_table,
           c_prompt, c_sample, w_ada, b_ada, norm_pre_mix, norm_post_mix, norm_pre_ffn, norm_post_ffn, w_in,
           rw_mu, rw_w0, rw_w_up, rw_a0, rw_a_up, rw_g_up, rw_k_k, rw_k_a, rw_r_k, rw_ln_w, rw_ln_b,
           ml_conv_w, ml_conv_b, ml_b_i, ml_b_f, ml_gn_w, fx_b_f, w_br_rwkv, w_br_mlstm, w_br_fox, w_out,
           ffn_w_gate, ffn_w_val, ffn_conv_w, ffn_conv_b, ffn_w_down):
    Pm = dict(norm_pre_mix=norm_pre_mix, norm_post_mix=norm_post_mix, norm_pre_ffn=norm_pre_ffn,
              norm_post_ffn=norm_post_ffn, w_in=w_in, rw_mu=rw_mu, rw_w0=rw_w0, rw_w_up=rw_w_up, rw_a0=rw_a0,
              rw_a_up=rw_a_up, rw_g_up=rw_g_up, rw_k_k=rw_k_k, rw_k_a=rw_k_a,
              rw_r_k=rw_r_k.reshape(DEPTH, RW_W), rw_ln_w=rw_ln_w, rw_ln_b=rw_ln_b, ml_conv_w=ml_conv_w,
              ml_conv_b=ml_conv_b, ml_b_i=ml_b_i, ml_b_f=ml_b_f, ml_gn_w=ml_gn_w, fx_b_f=fx_b_f,
              w_br_rwkv=w_br_rwkv, w_br_mlstm=w_br_mlstm, w_br_fox=w_br_fox, w_out=w_out, ffn_w_gate=ffn_w_gate,
              ffn_w_val=ffn_w_val, ffn_conv_w=ffn_conv_w, ffn_conv_b=ffn_conv_b, ffn_w_down=ffn_w_down)
    Bp, Tpr, D = x_prompt.shape
    Bs, Ts, _ = x_sample.shape
    cache_k, cache_v = cache_fox_k, cache_fox_v
    cache_lft = jnp.swapaxes(cache_fox_logf, -1, -2)

    zeros = lambda *s: jnp.zeros(s, f32)
    xp = x_prompt.reshape(Bp * Tpr, D)
    xs = x_sample.reshape(Bs * Ts, D)
    c_all = jnp.pad(jnp.concatenate([c_prompt, c_sample], axis=0), ((0, 16 - Bp - Bs), (0, 0)))
    new_p = {n: [] for n in STATE_NAMES}
    new_s = {n: [] for n in STATE_NAMES}
    for l in range(DEPTH):
        W = _layer_weights(Pm, l, page_table.shape[1])
        mod = ada_mod(c_all, w_ada, b_ada[:, None, :], l)
        st_p = dict(rw_shift=zeros(Bp, RW_COLS), rw_wkv=zeros(Bp, RW_H, RW_DH, RW_DH),
                    ml_conv=zeros(Bp, ML_CONV - 1, ML_QK), ml_c=zeros(Bp, ML_H, ML_DH, ML_DH),
                    ml_n=zeros(Bp, ML_H, ML_DH), ml_m=zeros(Bp, ML_H), ffn_conv=zeros(Bp, FFN_CONV - 1, D_FF))
        st_s = dict(rw_shift=state_rwkv_shift[l], rw_wkv=state_rwkv_wkv[l], ml_conv=state_mlstm_conv[l],
                    ml_c=state_mlstm_c[l], ml_n=state_mlstm_n[l], ml_m=state_mlstm_m[l],
                    ffn_conv=state_ffn_conv[l])
        xp, lp = _layer(xp, mod[:Bp], st_p, None, W, None, l, Bp, Tpr)
        xs, ls = _layer(xs, mod[Bp:Bp + Bs], st_s, (cache_k, cache_v, cache_lft), W, page_table, l, Bs, Ts)
        for n in STATE_NAMES:
            new_p[n].append(lp[n])
            new_s[n].append(ls[n])
    sp = {n: jnp.stack(v) for n, v in new_p.items()}
    ss = {n: jnp.stack(v) for n, v in new_s.items()}
    return (xp.reshape(Bp, Tpr, D), xs.reshape(Bs, Ts, D),
            sp['fox_k'], ss['fox_k'], sp['fox_v'], ss['fox_v'], sp['fox_logf'], ss['fox_logf'],
            sp['rw_shift'], ss['rw_shift'], sp['rw_wkv'], ss['rw_wkv'],
            sp['ml_conv'], ss['ml_conv'], sp['ml_c'], ss['ml_c'], sp['ml_n'], ss['ml_n'], sp['ml_m'], ss['ml_m'],
            sp['ffn_conv'], ss['ffn_conv'])
```

```python
import functools
import math

import jax
import jax.numpy as jnp
from jax import lax
from jax.experimental import pallas as pl
from jax.experimental.pallas import tpu as pltpu

f32 = jnp.float32
bf16 = jnp.bfloat16

D_MODEL = 2048
DEPTH = 2
PAGE_SIZE = 128
RW_W = D_MODEL // 4
RW_DH = 64
RW_H = RW_W // RW_DH
R_W = max(32, int(round(1.8 * D_MODEL ** 0.5 / 32)) * 32)
R_A = R_W
R_G = max(32, int(round(0.6 * D_MODEL ** 0.8 / 32)) * 32)
RW_COLS = 3 * RW_W + R_W + R_A + R_G
RW_LN_EPS = 64e-5
RW_DECAY_CLAMP = 0.5
ML_W = D_MODEL // 4
ML_H = 4
ML_DH = ML_W // ML_H
ML_QK = 2 * ML_W
ML_CONV = 4
ML_COLS = ML_QK + ML_W + 2 * ML_H + ML_W
ML_GN_EPS = 1e-5
FX_W = D_MODEL // 2
FX_DH = 128
FX_H = FX_W // FX_DH
FX_COLS = 3 * FX_W + FX_H
N_BRANCH = 3
GATE_COLS = N_BRANCH * D_MODEL
D_FF = ((8 * D_MODEL // 3 + 127) // 128) * 128
FFN_CONV = 3
N_MOD = 6
NORM_EPS = 1e-6

LANES = 128
SUBLANES = 8
VMEM_LIMIT = 56 * 1024 * 1024
CHUNK = 128
NEG = -1e30

RW_LORA_PAD = LANES
RW_PCOLS = 3 * RW_W + 2 * RW_LORA_PAD + R_G
OFF_RW = 0
OFF_ML = OFF_RW + RW_PCOLS
ML_PCOLS = ML_QK + 2 * ML_W
OFF_FXQ = OFF_ML + ML_PCOLS
P_COLS = OFF_FXQ + FX_W
SM_LI, SM_LF, SM_FX = 0, ML_H, 2 * ML_H
D_FF_PAD = 5632
FF_TK = 1408


def _cparams(*sem):
    return pltpu.CompilerParams(dimension_semantics=sem, vmem_limit_bytes=VMEM_LIMIT)


def _split3(x):
    hi = x.astype(bf16)
    r = x - hi.astype(f32)
    mid = r.astype(bf16)
    lo = (r - mid.astype(f32)).astype(bf16)
    return hi, mid, lo


def _dot(a, b):
    return jnp.dot(a, b, preferred_element_type=f32)


def _dot_nt(a, b):
    return lax.dot_general(a, b, (((1,), (1,)), ((), ())), preferred_element_type=f32)


def _dot_r(x, m):
    hi, mid, lo = _split3(x)
    return _dot(hi, m) + _dot(mid, m) + _dot(lo, m)


def _dot_r2(x, m):
    hi = x.astype(bf16)
    lo = (x - hi.astype(f32)).astype(bf16)
    return _dot(hi, m) + _dot(lo, m)


def _dot_l(m, x):
    hi, mid, lo = _split3(x)
    return _dot(m, hi) + _dot(m, mid) + _dot(m, lo)


def _sigmoid(x):
    return 0.5 * (jnp.tanh(0.5 * x) + 1.0)


def _log_sigmoid(x):
    return jnp.minimum(x, 0.0) - jnp.log(1.0 + jnp.exp(-jnp.abs(x)))


def _softplus(x):
    return jnp.maximum(x, 0.0) + jnp.log(1.0 + jnp.exp(-jnp.abs(x)))


def _iota(shape, dim):
    return lax.broadcasted_iota(jnp.int32, shape, dim)


def _lane_tiles(x):
    return [x[:, i:i + LANES] for i in range(0, x.shape[-1], LANES)]


def _row_max(x):
    return jnp.max(functools.reduce(jnp.maximum, _lane_tiles(x)), axis=-1, keepdims=True)


def _row_sum(x):
    return jnp.sum(functools.reduce(jnp.add, _lane_tiles(x)), axis=-1, keepdims=True)


def _softmax_step(scores, m_old, l_old):
    tiles = [_lane_tiles(s) for s in scores]
    flat = [t for ts in tiles for t in ts]
    m_new = jnp.maximum(m_old, jnp.max(functools.reduce(jnp.maximum, flat), axis=-1, keepdims=True))
    alpha = jnp.exp(m_old - m_new)
    p_tiles = [[jnp.exp(t - m_new) for t in ts] for ts in tiles]
    total = functools.reduce(jnp.add, [t for ts in p_tiles for t in ts])
    l_new = alpha * l_old + jnp.sum(total, axis=-1, keepdims=True)
    ps = [jnp.concatenate([t.astype(bf16) for t in ts], axis=1) for ts in p_tiles]
    return m_new, l_new, alpha, ps


def _group_vec(v, T, tm):
    B, D = v.shape
    if tm <= T:
        per = T // tm
        return v[:, None, :], (None, 1, D), lambda i, *_: (i // per, 0, 0)
    rows = jnp.repeat(v, T, axis=0)
    return rows.reshape(-1, tm, D), (None, tm, D), lambda i, *_: (i, 0, 0)


def _row_tile(n_rows, T, want):
    return want if T >= want else n_rows


def _ada_kernel(c_ref, w_ref, b_ref, o_ref):
    c = c_ref[...]
    a = (c * _sigmoid(c)).astype(bf16)
    o_ref[...] = _dot(a, w_ref[...].astype(bf16)) + b_ref[...]


def ada_mod(c, w_all, b_all, layer):
    M, K = c.shape
    N = w_all.shape[2]
    tn = 1024
    return pl.pallas_call(
        _ada_kernel, grid=(N // tn,),
        in_specs=[pl.BlockSpec((M, K), lambda j: (0, 0)),
                  pl.BlockSpec((None, K, tn), lambda j: (layer, 0, j)),
                  pl.BlockSpec((None, 1, tn), lambda j: (layer, 0, j))],
        out_specs=pl.BlockSpec((M, tn), lambda j: (0, j)),
        out_shape=jax.ShapeDtypeStruct((M, N), f32),
        compiler_params=_cparams("parallel"), name="ada_mod")(c, w_all, b_all)


def _norm_mod_kernel(x_ref, g_ref, sc_ref, sh_ref, o_ref):
    x = x_ref[...]
    ms = jnp.mean(x * x, axis=-1, keepdims=True)
    y = x * lax.rsqrt(ms + NORM_EPS) * g_ref[...]
    o_ref[...] = (y * (1.0 + sc_ref[...]) + sh_ref[...]).astype(o_ref.dtype)


def norm_mod(x, g, scale, shift, T):
    N, D = x.shape
    tm = _row_tile(N, T, 1024)
    sc, sc_blk, sc_map = _group_vec(scale, T, tm)
    sh, _, _ = _group_vec(shift, T, tm)
    return pl.pallas_call(
        _norm_mod_kernel, grid=(N // tm,),
        in_specs=[pl.BlockSpec((tm, D), lambda i: (i, 0)),
                  pl.BlockSpec((1, D), lambda i: (0, 0)),
                  pl.BlockSpec(sc_blk, sc_map), pl.BlockSpec(sc_blk, sc_map)],
        out_specs=pl.BlockSpec((tm, D), lambda i: (i, 0)),
        out_shape=jax.ShapeDtypeStruct((N, D), bf16),
        compiler_params=_cparams("parallel"), name="norm_mod")(x, g.reshape(1, D), sc, sh)


def _mm_kernel(a_ref, w_ref, o_ref, *, gate):
    r = _dot(a_ref[...], w_ref[...])
    o_ref[...] = (_sigmoid(r) if gate else r).astype(o_ref.dtype)


def _layer_spec(block, index_map, layer, **kw):
    if layer is None:
        return pl.BlockSpec(block, index_map, **kw)
    return pl.BlockSpec((None,) + block, lambda *idx: (layer,) + index_map(*idx), **kw)


def matmul(a, w, tn, out_dtype=f32, gate=False, layer=None):
    M, K = a.shape
    N = w.shape[-1]
    tm = min(M, 1024)
    return pl.pallas_call(
        functools.partial(_mm_kernel, gate=gate), grid=(M // tm, N // tn),
        in_specs=[pl.BlockSpec((tm, K), lambda i, j: (i, 0)),
                  _layer_spec((K, tn), lambda i, j: (0, j), layer)],
        out_specs=pl.BlockSpec((tm, tn), lambda i, j: (i, j)),
        out_shape=jax.ShapeDtypeStruct((M, N), out_dtype),
        compiler_params=_cparams("parallel", "arbitrary"), name="matmul")(a, w)


def _mm_norm_res_kernel(a_ref, w_ref, x_ref, g_ref, gate_ref, o_ref):
    f = _dot(a_ref[...].astype(bf16), w_ref[...])
    ms = jnp.mean(f * f, axis=-1, keepdims=True)
    y = f * lax.rsqrt(ms + NORM_EPS) * g_ref[...]
    o_ref[...] = x_ref[...] + gate_ref[...] * y


def mm_norm_res(a, w, layer, x, g, gate, T):
    M, K = a.shape
    D = w.shape[-1]
    tm = _row_tile(M, T, 512 if K <= D_MODEL else 256)
    gt, gt_blk, gt_map = _group_vec(gate, T, tm)
    return pl.pallas_call(
        _mm_norm_res_kernel, grid=(M // tm,),
        in_specs=[pl.BlockSpec((tm, K), lambda i: (i, 0)),
                  _layer_spec((K, D), lambda i: (0, 0), layer, pipeline_mode=pl.Buffered(1)),
                  pl.BlockSpec((tm, D), lambda i: (i, 0)),
                  pl.BlockSpec((1, D), lambda i: (0, 0)),
                  pl.BlockSpec(gt_blk, gt_map)],
        out_specs=pl.BlockSpec((tm, D), lambda i: (i, 0)),
        out_shape=jax.ShapeDtypeStruct((M, D), f32),
        compiler_params=_cparams("parallel"), name="mm_norm_res")(
            a, w, x, g.reshape(1, D), gt)


def _merge_kernel(orw_ref, oml_ref, ofx_ref, wrw_ref, wml_ref, wfx_ref, g0_ref, g1_ref, g2_ref, o_ref):
    m = g0_ref[...] * _dot(orw_ref[...].astype(bf16), wrw_ref[...])
    m += g1_ref[...] * _dot(oml_ref[...].astype(bf16), wml_ref[...])
    m += g2_ref[...] * _dot(ofx_ref[...].astype(bf16), wfx_ref[...])
    o_ref[...] = m.astype(o_ref.dtype)


def merge(o_rw, o_ml, o_fx, w_rw, w_ml, w_fx, G):
    N = o_rw.shape[0]
    D = D_MODEL
    tm = min(N, 512)
    tn = 512
    nb = D // tn
    gspec = lambda b: pl.BlockSpec((tm, tn), lambda i, j: (i, b * nb + j))
    return pl.pallas_call(
        _merge_kernel, grid=(N // tm, nb),
        in_specs=[pl.BlockSpec((tm, RW_W), lambda i, j: (i, 0)),
                  pl.BlockSpec((tm, ML_W), lambda i, j: (i, 0)),
                  pl.BlockSpec((tm, FX_W), lambda i, j: (i, 0)),
                  pl.BlockSpec((RW_W, tn), lambda i, j: (0, j)),
                  pl.BlockSpec((ML_W, tn), lambda i, j: (0, j)),
                  pl.BlockSpec((FX_W, tn), lambda i, j: (0, j)),
                  gspec(0), gspec(1), gspec(2)],
        out_specs=pl.BlockSpec((tm, tn), lambda i, j: (i, j)),
        out_shape=jax.ShapeDtypeStruct((N, D), bf16),
        compiler_params=_cparams("parallel", "arbitrary"), name="merge")(
            o_rw, o_ml, o_fx, w_rw, w_ml, w_fx, G, G, G)


def _ffn_act_kernel(a_ref, halo_ref, buf_ref, val_ref, cw_ref, cb_ref, o_ref, scr_ref, *, per, tm):
    first = (pl.program_id(0) % per) == 0
    scr_ref[0:SUBLANES, :] = jnp.where(first, buf_ref[...], halo_ref[...])
    scr_ref[SUBLANES:SUBLANES + tm, :] = a_ref[...]
    y = cb_ref[...]
    for j in range(FFN_CONV):
        off = SUBLANES - (FFN_CONV - 1) + j
        y = y + scr_ref[off:off + tm, :] * cw_ref[j:j + 1, :]
    c0 = math.sqrt(2.0 / math.pi)
    gelu = 0.5 * y * (1.0 + jnp.tanh(c0 * (y + 0.044715 * (y * y * y))))
    o_ref[...] = (gelu * val_ref[...]).astype(o_ref.dtype)


def ffn_act(av, buf8, conv_w, conv_b, T):
    N = av.shape[0]
    tm = min(T, 512)
    per = T // tm
    tn = FF_TK
    nj = D_FF_PAD // tn
    hb = tm // SUBLANES
    out_dtype = bf16 if tm % 16 == 0 else f32
    return pl.pallas_call(
        functools.partial(_ffn_act_kernel, per=per, tm=tm), grid=(N // tm, nj),
        in_specs=[pl.BlockSpec((tm, tn), lambda i, j: (i, j)),
                  pl.BlockSpec((SUBLANES, tn), lambda i, j: (jnp.maximum(i * hb - 1, 0), j)),
                  pl.BlockSpec((None, SUBLANES, tn), lambda i, j: (i // per, 0, j)),
                  pl.BlockSpec((tm, tn), lambda i, j: (i, nj + j)),
                  pl.BlockSpec((SUBLANES, tn), lambda i, j: (0, j)),
                  pl.BlockSpec((1, tn), lambda i, j: (0, j))],
        out_specs=pl.BlockSpec((tm, tn), lambda i, j: (i, j)),
        out_shape=jax.ShapeDtypeStruct((N, D_FF_PAD), out_dtype),
        scratch_shapes=[pltpu.VMEM((tm + SUBLANES, tn), f32)],
        compiler_params=_cparams("parallel", "arbitrary"), name="ffn_act")(
            av, av, buf8, av, conv_w, conv_b)


BF16_ROWS = 16


def _ffn_up_kernel(z_ref, zh_ref, wg_ref, wv_ref, buf_ref, cw_ref, cb_ref, h_ref, tail_ref, scr_ref, *, per, tm):
    first = (pl.program_id(0) % per) == 0
    wg = wg_ref[...]
    a = _dot(z_ref[...], wg)
    val = _dot(z_ref[...], wv_ref[...])
    a_prev = _dot(zh_ref[...], wg)[BF16_ROWS - SUBLANES:, :]
    scr_ref[0:SUBLANES, :] = jnp.where(first, buf_ref[...], a_prev)
    scr_ref[SUBLANES:SUBLANES + tm, :] = a
    y = cb_ref[...]
    for j in range(FFN_CONV):
        off = SUBLANES - (FFN_CONV - 1) + j
        y = y + scr_ref[off:off + tm, :] * cw_ref[j:j + 1, :]
    c0 = math.sqrt(2.0 / math.pi)
    gelu = 0.5 * y * (1.0 + jnp.tanh(c0 * (y + 0.044715 * (y * y * y))))
    h_ref[...] = (gelu * val).astype(h_ref.dtype)
    tail_ref[...] = a[tm - SUBLANES:tm, :]


def ffn_up(z, w_gv, layer, buf8, conv_w, conv_b, T):
    N, D = z.shape
    tm = min(T, 1024)
    per = T // tm
    tn = 512
    nj = D_FF_PAD // tn
    hb = tm // BF16_ROWS
    return pl.pallas_call(
        functools.partial(_ffn_up_kernel, per=per, tm=tm), grid=(N // tm, nj),
        in_specs=[pl.BlockSpec((tm, D), lambda i, j: (i, 0)),
                  pl.BlockSpec((BF16_ROWS, D), lambda i, j: (jnp.maximum(i * hb - 1, 0), 0)),
                  _layer_spec((D, tn), lambda i, j: (0, j), layer),
                  _layer_spec((D, tn), lambda i, j: (0, nj + j), layer),
                  pl.BlockSpec((None, SUBLANES, tn), lambda i, j: (i // per, 0, j)),
                  pl.BlockSpec((SUBLANES, tn), lambda i, j: (0, j)),
                  pl.BlockSpec((1, tn), lambda i, j: (0, j))],
        out_specs=[pl.BlockSpec((tm, tn), lambda i, j: (i, j)),
                   pl.BlockSpec((SUBLANES, tn), lambda i, j: (i, j))],
        out_shape=[jax.ShapeDtypeStruct((N, D_FF_PAD), bf16),
                   jax.ShapeDtypeStruct((N // tm * SUBLANES, D_FF_PAD), f32)],
        scratch_shapes=[pltpu.VMEM((tm + SUBLANES, tn), f32)],
        compiler_params=_cparams("parallel", "arbitrary"), name="ffn_up")(
            z, z, w_gv, w_gv, buf8, conv_w, conv_b)


def _head_sum(x, ones_blk, terms=3):
    return _dot_r(x, ones_blk) if terms == 3 else _dot_r2(x, ones_blk)


def _rwkv_prep_kernel(p_ref, halo_ref, sp_ref, mu_ref, w0_ref, a0_ref, kk_ref, ka_ref, rk_ref,
                      wup_ref, aup_ref, gup_ref, ones_ref,
                      nkk_o, wr_o, w_o, b_o, k_o, v_o, br_o, kr_o, rkr_o, g_o, scr_ref, *, per, tm):
    first = (pl.program_id(0) % per) == 0
    scr_ref[0:SUBLANES, :] = jnp.where(first, sp_ref[...], halo_ref[...])
    p = p_ref[...]
    scr_ref[SUBLANES:SUBLANES + tm, :] = p
    prev = scr_ref[SUBLANES - 1:SUBLANES - 1 + tm, :]
    xs = p + (prev - p) * mu_ref[...]
    W = RW_W
    r, k, v = xs[:, 0:W], xs[:, W:2 * W], xs[:, 2 * W:3 * W]
    o = 3 * W
    dw = xs[:, o:o + RW_LORA_PAD]
    da = xs[:, o + RW_LORA_PAD:o + 2 * RW_LORA_PAD]
    dg = xs[:, o + 2 * RW_LORA_PAD:o + 2 * RW_LORA_PAD + R_G]
    w_raw = -_softplus(-(w0_ref[...] + _dot(jnp.tanh(dw).astype(bf16), wup_ref[...]))) - RW_DECAY_CLAMP
    w = jnp.exp(-jnp.exp(w_raw))
    a = _sigmoid(a0_ref[...] + _dot(da.astype(bf16), aup_ref[...]))
    g = _dot(_sigmoid(dg).astype(bf16), gup_ref[...])
    ones_blk = ones_ref[...]
    kk = k * kk_ref[...]
    nrm = jnp.sqrt(_head_sum(kk * kk, ones_blk))
    kk = kk / jnp.maximum(nrm, 1e-12)
    k2 = k * (1.0 + (a - 1.0) * ka_ref[...])
    b = kk * a
    nkk_o[...] = -kk
    wr_o[...] = w * r
    w_o[...] = w
    b_o[...] = b
    k_o[...] = k2
    v_o[...] = v
    br_o[...] = _head_sum(b * r, ones_blk, terms=2)
    kr_o[...] = _head_sum(k2 * r, ones_blk, terms=2)
    rkr_o[...] = _head_sum(r * k2 * rk_ref[...], ones_blk, terms=2)
    g_o[...] = g


def rwkv_prep(P, col_blk, sp8, mu, w0, a0, k_k, k_a, r_k, w_up, a_up, g_up, ones_blk, T):
    N = P.shape[0]
    tm = min(T, 256)
    per = T // tm
    hb = tm // SUBLANES
    C = RW_PCOLS
    vec = lambda n: pl.BlockSpec((1, n), lambda i: (0, 0))
    full = lambda a: pl.BlockSpec(a.shape, lambda i: (0, 0))
    out = jax.ShapeDtypeStruct((N, RW_W), f32)
    ospec = pl.BlockSpec((tm, RW_W), lambda i: (i, 0))
    return pl.pallas_call(
        functools.partial(_rwkv_prep_kernel, per=per, tm=tm), grid=(N // tm,),
        in_specs=[pl.BlockSpec((tm, C), lambda i: (i, col_blk)),
                  pl.BlockSpec((SUBLANES, C), lambda i: (jnp.maximum(i * hb - 1, 0), col_blk)),
                  pl.BlockSpec((None, SUBLANES, C), lambda i: (i // per, 0, 0)),
                  vec(C), vec(RW_W), vec(RW_W), vec(RW_W), vec(RW_W), vec(RW_W),
                  full(w_up), full(a_up), full(g_up), full(ones_blk)],
        out_specs=[ospec] * 10, out_shape=[out] * 10,
        scratch_shapes=[pltpu.VMEM((tm + SUBLANES, C), f32)],
        compiler_params=_cparams("parallel"), name="rwkv_prep")(
            P, P, sp8, mu, w0, a0, k_k, k_a, r_k, w_up, a_up, g_up, ones_blk)


N_PAIR = RW_H // 2


def _rwkv_scan_kernel(nkk_ref, wr_ref, w_ref, b_ref, k_ref, br_ref, v_ref, s0_ref, ones_ref,
                      z_ref, so_ref, S_ref, vT_ref, *, n_sub, n_steps, nb):
    c = pl.program_id(1)
    chains = [(bb, j) for bb in range(nb) for j in range(N_PAIR)]

    @pl.when(c == 0)
    def _():
        S_ref[...] = s0_ref[...]

    if n_steps < CHUNK:
        z_ref[...] = jnp.zeros_like(z_ref)

    lo = _iota((1, LANES), 1) < RW_DH
    lane = _iota((1, LANES), 1)
    diag = _iota((RW_DH, LANES), 1) % RW_DH == _iota((RW_DH, LANES), 0)
    pair_ones = ones_ref[...]

    def as_row(x):
        return jnp.sum(jnp.where(diag, x, 0.0), axis=0, keepdims=True)

    for sub in range(n_sub):
        r0 = sub * CHUNK
        for q, (bb, j) in enumerate(chains):
            vT_ref[q] = v_ref[bb, r0:r0 + CHUNK, j * LANES:(j + 1) * LANES].T

        def group(g, carry):
            base = pl.multiple_of(r0 + g * SUBLANES, SUBLANES)
            tiles = [[ref[bb, pl.ds(base, SUBLANES), j * LANES:(j + 1) * LANES]
                      for ref in (nkk_ref, wr_ref, w_ref, b_ref, k_ref, br_ref)] for bb, j in chains]
            z_rows = [[] for _ in chains]
            for i in range(SUBLANES):
                tmask = lane == g * SUBLANES + i
                for q in range(len(chains)):
                    nkk_r, wr_r, w_r, b_r, k_r, br_r = [tl[i:i + 1, :] for tl in tiles[q]]
                    S = S_ref[q]
                    x = jnp.concatenate([S * nkk_r, S * wr_r], axis=0).astype(bf16)
                    red = _dot(x, pair_ones)
                    sa = red[0:RW_DH]
                    zz = red[RW_DH:2 * RW_DH] + sa * br_r
                    va = jnp.sum(jnp.where(tmask, vT_ref[q, 0:RW_DH, :], 0.0), axis=-1, keepdims=True)
                    vb = jnp.sum(jnp.where(tmask, vT_ref[q, RW_DH:2 * RW_DH, :], 0.0), axis=-1, keepdims=True)
                    vp = jnp.where(lo, va, vb)
                    S_ref[q] = S * w_r + sa * b_r + vp * k_r
                    z_rows[q].append(as_row(zz))
            for q, (bb, j) in enumerate(chains):
                cs = slice(j * LANES, (j + 1) * LANES)
                z_ref[bb, pl.ds(base, SUBLANES), cs] = jnp.concatenate(z_rows[q], axis=0)
            return carry

        lax.fori_loop(0, n_steps // SUBLANES, group, 0)

    so_ref[...] = S_ref[...]


def rwkv_scan(nkk, wr, w, b, k, br, v, s0, pair_ones, n_steps):
    B, Tp, W = nkk.shape
    nb = 2 if B % 2 == 0 else 1
    tc = min(Tp, 4 * CHUNK)
    n_sub = tc // CHUNK
    rows = pl.BlockSpec((None, nb, tc, W), lambda bi, c: (bi, 0, c, 0))
    st = pl.BlockSpec((None, nb * N_PAIR, RW_DH, LANES), lambda bi, c: (bi, 0, 0, 0))
    grp = lambda a: a.reshape(B // nb, nb, Tp, W)
    z, so = pl.pallas_call(
        functools.partial(_rwkv_scan_kernel, n_sub=n_sub, n_steps=min(n_steps, CHUNK), nb=nb),
        grid=(B // nb, Tp // tc),
        in_specs=[rows] * 7 + [st, pl.BlockSpec((LANES, LANES), lambda bi, c: (0, 0))],
        out_specs=[rows, st],
        out_shape=[jax.ShapeDtypeStruct((B // nb, nb, Tp, W), f32),
                   jax.ShapeDtypeStruct((B // nb, nb * N_PAIR, RW_DH, LANES), f32)],
        scratch_shapes=[pltpu.VMEM((nb * N_PAIR, RW_DH, LANES), f32), pltpu.VMEM((nb * N_PAIR, LANES, LANES), f32)],
        compiler_params=_cparams("parallel", "arbitrary"), name="rwkv_scan")(
            grp(nkk), grp(wr), grp(w), grp(b), grp(k), grp(br), grp(v),
            s0.reshape(B // nb, nb * N_PAIR, RW_DH, LANES), pair_ones)
    return z.reshape(B, Tp, W), so.reshape(s0.shape)


def _rwkv_post_kernel(z_ref, v_ref, kr_ref, rkr_ref, g_ref, lnw_ref, lnb_ref, ones_ref, o_ref):
    ones_blk = ones_ref[...]
    v = v_ref[...]
    y = z_ref[...] + v * kr_ref[...]
    mu = _head_sum(y, ones_blk, terms=2) * (1.0 / RW_DH)
    yc = y - mu
    var = _head_sum(yc * yc, ones_blk, terms=2) * (1.0 / RW_DH)
    yn = yc * lax.rsqrt(var + RW_LN_EPS) * lnw_ref[...] + lnb_ref[...]
    o_ref[...] = ((yn + rkr_ref[...] * v) * g_ref[...]).astype(o_ref.dtype)


def rwkv_post(z, v, kr, rkr, g, ln_w, ln_b, ones_blk):
    N, W = z.shape
    tm = min(N, 512)
    rows = pl.BlockSpec((tm, W), lambda i: (i, 0))
    vec = pl.BlockSpec((1, W), lambda i: (0, 0))
    return pl.pallas_call(
        _rwkv_post_kernel, grid=(N // tm,),
        in_specs=[rows] * 5 + [vec, vec, pl.BlockSpec(ones_blk.shape, lambda i: (0, 0))],
        out_specs=rows, out_shape=jax.ShapeDtypeStruct((N, W), bf16),
        compiler_params=_cparams("parallel"), name="rwkv_post")(z, v, kr, rkr, g, ln_w, ln_b, ones_blk)


def _mlstm_kernel(main_ref, halo_ref, buf_ref, sm_ref, bias_ref, cw_ref, cb_ref, gn_ref, tri_ref,
                  c0_ref, n0_ref, m0_ref,
                  o_ref, lf_o, F_o, FT_o, c_o, n_o, m_o,
                  scr_ref, ct_ref, n_ref, m_ref, carry_ref, *, t_real):
    c = pl.program_id(1)
    L = CHUNK

    @pl.when(c == 0)
    def _():
        ct_ref[...] = c0_ref[...]
        n_ref[...] = n0_ref[...]
        m_ref[...] = m0_ref[...]
        carry_ref[...] = jnp.zeros_like(carry_ref)

    main = main_ref[...]
    scr_ref[0:SUBLANES, :] = jnp.where(c == 0, buf_ref[...], halo_ref[...])
    scr_ref[SUBLANES:SUBLANES + L, :] = main[:, 0:ML_QK]
    qk = cb_ref[...]
    for j in range(ML_CONV):
        off = SUBLANES - (ML_CONV - 1) + j
        qk = qk + scr_ref[off:off + L, :] * cw_ref[j:j + 1, :]
    qk = qk * _sigmoid(qk)
    q_all = qk[:, 0:ML_W]
    k_all = qk[:, ML_W:ML_QK] * (ML_DH ** -0.5)
    v_all = main[:, ML_QK:ML_QK + ML_W]
    og_all = main[:, ML_QK + ML_W:ML_QK + 2 * ML_W]

    valid = (c * L + _iota((L, 1), 0)) < t_real
    pre = sm_ref[...] + bias_ref[...]
    li_all = jnp.where(valid, pre, NEG)
    lf_all = jnp.where(valid, _log_sigmoid(pre), 0.0)
    cum = _dot_l(tri_ref[...], lf_all)
    F = cum + carry_ref[...]
    carry_ref[...] = F[L - 1:L, :]
    lf_o[...] = lf_all
    F_o[...] = F
    FT_o[...] = F.T
    liT = li_all.T
    cumT = cum.T

    row = _iota((L, L), 0)
    col = _iota((L, L), 1)
    causal = col <= row
    for h in range(ML_H):
        hs = slice(h * ML_DH, (h + 1) * ML_DH)
        b_col = cum[:, SM_LF + h:SM_LF + h + 1]
        b_row = cumT[SM_LF + h:SM_LF + h + 1, :]
        li_col = li_all[:, SM_LI + h:SM_LI + h + 1]
        li_row = liT[SM_LI + h:SM_LI + h + 1, :]
        m_prev = m_ref[h][:, 0:1]
        d = jnp.where(causal, b_col - b_row + li_row, NEG)
        inter = b_col + m_prev
        m_t = jnp.maximum(inter, jnp.max(d, axis=-1, keepdims=True))
        w_intra = jnp.exp(d - m_t)
        w_state = jnp.exp(inter - m_t)
        q = q_all[:, hs]
        k = k_all[:, hs]
        v = v_all[:, hs]
        qb, kb = q.astype(bf16), k.astype(bf16)
        s = _dot_nt(qb, kb) * w_intra
        ct = ct_ref[h]
        num = w_state * _dot(qb, ct.astype(bf16)) + _dot(s.astype(bf16), v.astype(bf16))
        n_row = n_ref[h]
        den = w_state * jnp.sum(q * n_row, axis=-1, keepdims=True) + jnp.sum(s, axis=-1, keepdims=True)
        hh = num / jnp.maximum(jnp.abs(den), jnp.exp(-m_t))
        g_end = b_col[L - 1:L, :]
        lw_s = g_end - b_col + li_col
        m_new = jnp.maximum(g_end + m_prev, jnp.max(lw_s, axis=0, keepdims=True))
        w_s = jnp.exp(lw_s - m_new)
        decay = jnp.exp(g_end + m_prev - m_new)
        ct_ref[h] = decay * ct + _dot(kb.T, (w_s * v).astype(bf16))
        n_ref[h] = decay * n_row + jnp.sum(w_s * k, axis=0, keepdims=True)
        m_ref[h] = jnp.broadcast_to(m_new, (1, LANES))
        mu = jnp.mean(hh, axis=-1, keepdims=True)
        hc = hh - mu
        var = jnp.mean(hc * hc, axis=-1, keepdims=True)
        hn = hc * lax.rsqrt(var + ML_GN_EPS) * gn_ref[:, hs]
        o_ref[:, hs] = (_sigmoid(og_all[:, hs]) * hn).astype(o_ref.dtype)

    c_o[...] = ct_ref[...]
    n_o[...] = n_ref[...]
    m_o[...] = m_ref[...]


def mlstm_fox_prep(main_arr, main_blk, small_arr, small_blk, buf8, bias_row, conv_w, conv_b, gn_w, tri,
                   c0t, n0, m0, B, Tp, t_real):
    L = CHUNK
    nc = Tp // L
    hb = L // SUBLANES
    st_c = pl.BlockSpec((None, ML_H, ML_DH, ML_DH), lambda b, c: (b, 0, 0, 0))
    st_n = pl.BlockSpec((None, ML_H, 1, ML_DH), lambda b, c: (b, 0, 0, 0))
    rows = lambda w: pl.BlockSpec((L, w), lambda b, c: (b * nc + c, 0))
    vec = lambda n: pl.BlockSpec((1, n), lambda b, c: (0, 0))
    N = B * Tp
    return pl.pallas_call(
        functools.partial(_mlstm_kernel, t_real=t_real), grid=(B, nc),
        in_specs=[pl.BlockSpec((L, ML_PCOLS), lambda b, c: (b * nc + c, main_blk)),
                  pl.BlockSpec((SUBLANES, ML_QK), lambda b, c: (jnp.maximum((b * nc + c) * hb - 1, 0), 2 * main_blk)),
                  pl.BlockSpec((None, SUBLANES, ML_QK), lambda b, c: (b, 0, 0)),
                  pl.BlockSpec((L, LANES), lambda b, c: (b * nc + c, small_blk)),
                  vec(LANES),
                  pl.BlockSpec((SUBLANES, ML_QK), lambda b, c: (0, 0)),
                  vec(ML_QK), vec(ML_W),
                  pl.BlockSpec((L, L), lambda b, c: (0, 0)),
                  st_c, st_n, st_n],
        out_specs=[rows(ML_W), rows(LANES), rows(LANES),
                   pl.BlockSpec((None, LANES, L), lambda b, c: (b, 0, c)),
                   st_c, st_n, st_n],
        out_shape=[jax.ShapeDtypeStruct((N, ML_W), bf16),
                   jax.ShapeDtypeStruct((N, LANES), f32),
                   jax.ShapeDtypeStruct((N, LANES), f32),
                   jax.ShapeDtypeStruct((B, LANES, Tp), f32),
                   jax.ShapeDtypeStruct((B, ML_H, ML_DH, ML_DH), f32),
                   jax.ShapeDtypeStruct((B, ML_H, 1, ML_DH), f32),
                   jax.ShapeDtypeStruct((B, ML_H, 1, ML_DH), f32)],
        scratch_shapes=[pltpu.VMEM((L + SUBLANES, ML_QK), f32),
                        pltpu.VMEM((ML_H, ML_DH, ML_DH), f32),
                        pltpu.VMEM((ML_H, 1, ML_DH), f32),
                        pltpu.VMEM((ML_H, 1, LANES), f32),
                        pltpu.VMEM((1, LANES), f32)],
        compiler_params=_cparams("parallel", "arbitrary"), name="mlstm")(
            main_arr, main_arr, buf8, small_arr, bias_row, conv_w, conv_b, gn_w, tri, c0t, n0, m0)


FOX_HG = 4


def _fox_attn_kernel(q_ref, k_ref, v_ref, ft_ref, o_ref, m_ref, l_ref, acc_ref, *, tq, tk):
    h0 = pl.program_id(1) * FOX_HG
    qi = pl.program_id(2)
    m_ref[...] = jnp.full_like(m_ref, NEG)
    l_ref[...] = jnp.zeros_like(l_ref)
    acc_ref[...] = jnp.zeros_like(acc_ref)
    qbs = [(q_ref[:, u * FX_DH:(u + 1) * FX_DH] * (FX_DH ** -0.5)).astype(bf16) for u in range(FOX_HG)]

    def block(j, masked):
        k0 = pl.multiple_of(j * tk, tk)
        for u in range(FOX_HG):
            hs = slice(u * FX_DH, (u + 1) * FX_DH)
            kb = k_ref[pl.ds(k0, tk), hs].astype(bf16)
            vb = v_ref[pl.ds(k0, tk), hs].astype(bf16)
            s = _dot_nt(qbs[u], kb) - ft_ref[pl.ds(h0 + u, 1), pl.ds(k0, tk)]
            if masked:
                s = jnp.where(_iota((tq, tk), 1) + k0 <= qi * tq + _iota((tq, tk), 0), s, NEG)
            m_new, l_new, alpha, (p,) = _softmax_step([s], m_ref[u], l_ref[u])
            acc_ref[:, hs] = alpha * acc_ref[:, hs] + _dot(p, vb)
            m_ref[u] = m_new
            l_ref[u] = l_new

    n_full = (qi * tq) // tk

    def body(j, carry):
        block(j, False)
        return carry

    lax.fori_loop(0, n_full, body, 0)
    for d in range(max(tq // tk, 1)):
        block(n_full + d, True)
    for u in range(FOX_HG):
        hs = slice(u * FX_DH, (u + 1) * FX_DH)
        o_ref[:, hs] = (acc_ref[:, hs] / l_ref[u]).astype(o_ref.dtype)


def fox_attn(P, Kx, Vx, FT, B, T):
    tq = 256
    tk = 512
    nq = T // tq
    wb = FOX_HG * FX_DH
    qb = OFF_FXQ // wb
    return pl.pallas_call(
        functools.partial(_fox_attn_kernel, tq=tq, tk=tk), grid=(B, FX_H // FOX_HG, nq),
        in_specs=[pl.BlockSpec((tq, wb), lambda b, h, i: (b * nq + i, qb + h)),
                  pl.BlockSpec((T, wb), lambda b, h, i: (b, h)),
                  pl.BlockSpec((T, wb), lambda b, h, i: (b, h)),
                  pl.BlockSpec((None, SUBLANES, T), lambda b, h, i: (b, SM_FX // SUBLANES, 0))],
        out_specs=pl.BlockSpec((tq, wb), lambda b, h, i: (b * nq + i, h)),
        out_shape=jax.ShapeDtypeStruct((B * T, FX_W), bf16),
        scratch_shapes=[pltpu.VMEM((FOX_HG, tq, LANES), f32), pltpu.VMEM((FOX_HG, tq, LANES), f32),
                        pltpu.VMEM((tq, wb), f32)],
        compiler_params=_cparams("parallel", "parallel", "arbitrary"), name="fox_attn")(P, Kx, Vx, FT)


def _fox_suffix_kernel(pt_ref, lf_ref, triu_ref, later_ref, o_ref, x_ref, *, n_pages):
    b = pl.program_id(0)

    def gather(j, carry):
        x_ref[pl.ds(pl.multiple_of(j * FX_H, FX_H), FX_H), :] = lf_ref[pt_ref[b, j]]
        return carry

    lax.fori_loop(0, n_pages, gather, 0)
    x = x_ref[...]
    within = _dot_r(x, triu_ref[...])
    tot = jnp.broadcast_to(jnp.sum(x, axis=-1, keepdims=True), x.shape)
    o_ref[...] = within + _dot_l(later_ref[...], tot)


def fox_suffix(page_table, cache_lft, triu, later, layer):
    B, n_pages = page_table.shape
    n_pool = cache_lft.shape[1]
    rows = n_pages * FX_H
    grid_spec = pltpu.PrefetchScalarGridSpec(
        num_scalar_prefetch=1, grid=(B,),
        in_specs=[pl.BlockSpec((None, n_pool, FX_H, PAGE_SIZE), lambda b, pt: (layer, 0, 0, 0)),
                  pl.BlockSpec((PAGE_SIZE, PAGE_SIZE), lambda b, pt: (0, 0)),
                  pl.BlockSpec((rows, rows), lambda b, pt: (0, 0))],
        out_specs=pl.BlockSpec((None, rows, PAGE_SIZE), lambda b, pt: (b, 0, 0)),
        scratch_shapes=[pltpu.VMEM((rows, PAGE_SIZE), f32)])
    return pl.pallas_call(
        functools.partial(_fox_suffix_kernel, n_pages=n_pages), grid_spec=grid_spec,
        out_shape=jax.ShapeDtypeStruct((B, rows, PAGE_SIZE), f32),
        compiler_params=_cparams("parallel"), name="fox_suffix")(page_table, cache_lft, triu, later)


DEC_PP = 8


def _fox_decode_kernel(pt_ref, q_ref, *refs, n_steps, pp, t_new):
    kc = refs[0:pp]
    vc = refs[pp:2 * pp]
    suf_ref, hmask_ref, kn_ref, vn_ref, fkn_ref, o_ref, m_ref, l_ref, acc_ref = refs[2 * pp:]
    j = pl.program_id(1)
    nq = FX_H * t_new
    cols = PAGE_SIZE * FX_H

    @pl.when(j == 0)
    def _():
        m_ref[...] = jnp.full_like(m_ref, NEG)
        l_ref[...] = jnp.zeros_like(l_ref)
        acc_ref[...] = jnp.zeros_like(acc_ref)

    qb = (q_ref[...] * (FX_DH ** -0.5)).astype(bf16)

    def update(ss, vs):
        m_new, l_new, alpha, ps = _softmax_step(ss, m_ref[...], l_ref[...])
        acc = alpha * acc_ref[...]
        for p, vb in zip(ps, vs):
            acc = acc + _dot(p, vb)
        l_ref[...] = l_new
        acc_ref[...] = acc
        m_ref[...] = m_new

    @pl.when(j < n_steps)
    def _():
        hmask = hmask_ref[...]
        ss = [_dot_nt(qb, kc[i][...].reshape(cols, FX_DH).astype(bf16)) + hmask + suf_ref[i] for i in range(pp)]
        update(ss, [vc[i][...].reshape(cols, FX_DH).astype(bf16) for i in range(pp)])

    @pl.when(j == n_steps)
    def _():
        s = _dot_nt(qb, kn_ref[...].astype(bf16)) - fkn_ref[...]
        col = _iota((nq, LANES), 1)
        row = _iota((nq, LANES), 0)
        ok = (col % FX_H == row // t_new) & (col // FX_H <= row % t_new)
        update([jnp.where(ok, s, NEG)], [vn_ref[...].astype(bf16)])
        o_ref[...] = acc_ref[...] / l_ref[...]


def fox_decode(page_table, q2, cache_k, cache_v, suffix, hmask, k_new, v_new, fkn, layer, t_new):
    B, n_pages = page_table.shape
    pp = DEC_PP
    n_steps = n_pages // pp
    nq = FX_H * t_new
    cols = PAGE_SIZE * FX_H

    def page_spec(i):
        return pl.BlockSpec((None, None, PAGE_SIZE, FX_H, FX_DH),
                            lambda b, j, pt: (layer, pt[b, jnp.minimum(j, n_steps - 1) * pp + i], 0, 0, 0))

    per_b = lambda shape: pl.BlockSpec((None,) + shape, lambda b, j, pt: (b, 0, 0))
    grid_spec = pltpu.PrefetchScalarGridSpec(
        num_scalar_prefetch=1, grid=(B, n_steps + 1),
        in_specs=[per_b((nq, FX_DH))] + [page_spec(i) for i in range(pp)] * 2
        + [pl.BlockSpec((None, pp, 1, cols), lambda b, j, pt: (b, jnp.minimum(j, n_steps - 1), 0, 0)),
           pl.BlockSpec((nq, cols), lambda b, j, pt: (0, 0)),
           per_b((LANES, FX_DH)), per_b((LANES, FX_DH)), per_b((1, LANES))],
        out_specs=per_b((nq, FX_DH)),
        scratch_shapes=[pltpu.VMEM((nq, LANES), f32), pltpu.VMEM((nq, LANES), f32), pltpu.VMEM((nq, FX_DH), f32)])
    return pl.pallas_call(
        functools.partial(_fox_decode_kernel, n_steps=n_steps, pp=pp, t_new=t_new), grid_spec=grid_spec,
        out_shape=jax.ShapeDtypeStruct((B, nq, FX_DH), f32),
        compiler_params=_cparams("parallel", "arbitrary"), name="fox_decode")(
            page_table, q2, *([cache_k] * pp), *([cache_v] * pp), suffix, hmask, k_new, v_new, fkn)


def _pad_cols(w, n):
    return jnp.pad(w, ((0, 0), (0, n - w.shape[1])))


def _rw_cols(x):
    W = RW_W
    pad = [(0, 0)] * (x.ndim - 1) + [(0, RW_LORA_PAD - R_W)]
    return jnp.concatenate([x[..., 0:3 * W], jnp.pad(x[..., 3 * W:3 * W + R_W], pad),
                            jnp.pad(x[..., 3 * W + R_W:3 * W + R_W + R_A], pad),
                            x[..., 3 * W + R_W + R_A:]], axis=-1)


def _rw_cols_inv(x):
    W = RW_W
    o = 3 * W
    return jnp.concatenate([x[..., 0:o], x[..., o:o + R_W], x[..., o + RW_LORA_PAD:o + RW_LORA_PAD + R_A],
                            x[..., o + 2 * RW_LORA_PAD:]], axis=-1)


def _w_in_split(w):
    o_ml = RW_COLS
    o_fx = o_ml + ML_COLS
    o_gt = o_fx + FX_COLS
    ml = w[:, o_ml:o_fx]
    fx = w[:, o_fx:o_gt]
    ml_main = jnp.concatenate([ml[:, 0:ML_QK + ML_W], ml[:, ML_QK + ML_W + 2 * ML_H:]], axis=1)
    small = _pad_cols(jnp.concatenate([ml[:, ML_QK + ML_W:ML_QK + ML_W + 2 * ML_H], fx[:, 3 * FX_W:]], axis=1), LANES)
    main = jnp.concatenate([_rw_cols(w[:, 0:RW_COLS]), ml_main, fx[:, 0:FX_W]], axis=1)
    return (main.astype(bf16), w[:, o_gt:].astype(bf16), fx[:, FX_W:2 * FX_W].astype(bf16),
            fx[:, 2 * FX_W:3 * FX_W].astype(bf16), small.astype(bf16))


def _pad_rows(w, n):
    return jnp.pad(w, ((0, n - w.shape[0]), (0, 0)))


def _prev_rows8(buf, width):
    B, r, C = buf.shape
    return jnp.pad(buf, ((0, 0), (SUBLANES - r, 0), (0, width - C)))


def _layer(x, mod, st, past, W, page_table, l, B, T):
    N = B * T
    shift1, scale1, gate1, shift2, scale2, gate2 = jnp.split(mod, N_MOD, axis=-1)
    u = norm_mod(x, W['norm_pre_mix'], scale1, shift1, T)
    P = matmul(u, W['w_in'], tn=1024)
    G = matmul(u, W['w_in_gate'], tn=1024, out_dtype=bf16, gate=True)
    Kx = matmul(u, W['w_in_k'], tn=FX_W)
    Vx = matmul(u, W['w_in_v'], tn=FX_W)
    SM = matmul(u, W['w_in_small'], tn=LANES)

    padded = T % CHUNK != 0
    Tp = T if not padded else CHUNK

    def pad_t(a):
        if not padded:
            return a
        return jnp.pad(a.reshape(B, T, -1), ((0, 0), (0, Tp - T), (0, 0))).reshape(B * Tp, -1)

    def unpad_t(a):
        if not padded:
            return a
        return a.reshape(B, Tp, -1)[:, :T].reshape(N, -1)

    sp8 = _prev_rows8(_rw_cols(st['rw_shift'])[:, None, :], RW_PCOLS)
    prep = rwkv_prep(P, OFF_RW // RW_PCOLS, sp8, W['rw_mu'], W['rw_w0'], W['rw_a0'], W['rw_k_k'], W['rw_k_a'],
                     W['rw_r_k'], W['rw_w_up'], W['rw_a_up'], W['rw_g_up'], W['ones_blk'], T)
    nkk, wr, w_, b_, k2, v_rw, br, kr, rkr, g_rw = prep
    s0 = st['rw_wkv'].reshape(B, N_PAIR, 2, RW_DH, RW_DH).transpose(0, 1, 3, 2, 4).reshape(B, N_PAIR, RW_DH, LANES)
    seq = [pad_t(a).reshape(B, Tp, RW_W) for a in (nkk, wr, w_, b_, k2, br, v_rw)]
    z, s_out = rwkv_scan(*seq, s0, W['ones_blk'][0:LANES, 0:LANES], T)
    z = unpad_t(z.reshape(B * Tp, RW_W))
    o_rw = rwkv_post(z, v_rw, kr, rkr, g_rw, W['rw_ln_w'], W['rw_ln_b'], W['ones_blk'])
    rw_wkv = s_out.reshape(B, N_PAIR, RW_DH, 2, RW_DH).transpose(0, 1, 3, 2, 4).reshape(B, RW_H, RW_DH, RW_DH)
    P3 = P.reshape(B, T, P_COLS)
    rw_shift = _rw_cols_inv(P3[:, T - 1, OFF_RW:OFF_RW + RW_PCOLS])

    buf8 = _prev_rows8(st['ml_conv'], ML_QK)
    c0t = jnp.swapaxes(st['ml_c'], -1, -2)
    n0 = st['ml_n'][:, :, None, :]
    m0 = jnp.broadcast_to(st['ml_m'][:, :, None, None], (B, ML_H, 1, LANES))
    if padded:
        main_arr, main_blk = pad_t(P[:, OFF_ML:OFF_ML + ML_PCOLS]), 0
        small_arr, small_blk = pad_t(SM), 0
    else:
        main_arr, main_blk, small_arr, small_blk = P, OFF_ML // ML_PCOLS, SM, 0
    o_ml, lf_all, F, FT, ct, n_new, m_new = mlstm_fox_prep(
        main_arr, main_blk, small_arr, small_blk, buf8, W['small_bias'], W['ml_conv_w'], W['ml_conv_b'],
        W['ml_gn_w'], W['tri'], c0t, n0, m0, B, Tp, T)
    o_ml = unpad_t(o_ml)
    ml_c = jnp.swapaxes(ct, -1, -2)
    ml_n = n_new[:, :, 0, :]
    ml_m = m_new[:, :, 0, 0]
    ml_conv = P3[:, T - (ML_CONV - 1):, OFF_ML:OFF_ML + ML_QK]
    fox_logf = unpad_t(lf_all)[:, SM_FX:SM_FX + FX_H].reshape(B, T, FX_H)
    fox_k = Kx.reshape(B, T, FX_H, FX_DH)
    fox_v = Vx.reshape(B, T, FX_H, FX_DH)

    if past is None:
        o_fx = fox_attn(P, Kx, Vx, FT, B, T)
    else:
        cache_k, cache_v, cache_lft = past
        n_pages = page_table.shape[1]
        nq = FX_H * T
        assert nq <= LANES and n_pages % DEC_PP == 0
        suffix = fox_suffix(page_table, cache_lft, W['triu'], W['later'], l)
        suffix = suffix.reshape(B, n_pages, FX_H, PAGE_SIZE).transpose(0, 1, 3, 2).reshape(
            B, n_pages, 1, PAGE_SIZE * FX_H)
        q2 = P[:, OFF_FXQ:OFF_FXQ + FX_W].reshape(B, T, FX_H, FX_DH).transpose(0, 2, 1, 3).reshape(B, nq, FX_DH)
        padk = lambda a: jnp.pad(a.reshape(B, nq, FX_DH), ((0, 0), (0, LANES - nq), (0, 0)))
        fkn = F.reshape(B, Tp, LANES)[:, :T, SM_FX:SM_FX + FX_H].reshape(B, 1, nq)
        fkn = jnp.pad(fkn, ((0, 0), (0, 0), (0, LANES - nq)))
        hmask = jnp.where(jnp.arange(PAGE_SIZE * FX_H)[None, :] % FX_H == jnp.arange(nq)[:, None] // T, 0.0, NEG)
        o = fox_decode(page_table, q2, cache_k, cache_v, suffix, hmask.astype(f32), padk(fox_k), padk(fox_v),
                       fkn, l, T)
        o_fx = o.reshape(B, FX_H, T, FX_DH).transpose(0, 2, 1, 3).reshape(N, FX_W)

    merged = merge(o_rw, o_ml, o_fx, W['w_br_rwkv'], W['w_br_mlstm'], W['w_br_fox'], G)
    x = mm_norm_res(merged, W['w_out'], l, x, W['norm_post_mix'], gate1, T)

    zf = norm_mod(x, W['norm_pre_ffn'], scale2, shift2, T)
    fbuf8 = _prev_rows8(st['ffn_conv'], D_FF_PAD)
    if T % BF16_ROWS == 0:
        hmid, tails = ffn_up(zf, W['ffn_w_gv'], l, fbuf8, W['ffn_conv_w'], W['ffn_conv_b'], T)
        per = N // B // min(T, 1024)
        ffn_conv = tails.reshape(B, per, SUBLANES, D_FF_PAD)[:, -1, SUBLANES - (FFN_CONV - 1):, 0:D_FF]
    else:
        av = matmul(zf, W['ffn_w_gv'], tn=1024, layer=l)
        hmid = ffn_act(av, fbuf8, W['ffn_conv_w'], W['ffn_conv_b'], T)
        ffn_conv = av[:, 0:D_FF].reshape(B, T, D_FF)[:, T - (FFN_CONV - 1):]
    x = mm_norm_res(hmid, W['ffn_w_down'], l, x, W['norm_post_ffn'], gate2, T)

    new = dict(fox_k=fox_k, fox_v=fox_v, fox_logf=fox_logf, rw_shift=rw_shift, rw_wkv=rw_wkv,
               ml_conv=ml_conv, ml_c=ml_c, ml_n=ml_n, ml_m=ml_m, ffn_conv=ffn_conv)
    return x, new


STATE_NAMES = ("fox_k", "fox_v", "fox_logf", "rw_shift", "rw_wkv", "ml_conv", "ml_c", "ml_n", "ml_m", "ffn_conv")


def _layer_weights(Pm, l, n_pages):
    row = lambda v: v.reshape(1, -1)
    W = {}
    for name in ('norm_pre_mix', 'norm_post_mix', 'norm_pre_ffn', 'norm_post_ffn'):
        W[name] = Pm[name][l]
    W['w_in'], W['w_in_gate'], W['w_in_k'], W['w_in_v'], W['w_in_small'] = _w_in_split(Pm['w_in'][l])
    W['rw_mu'] = row(_rw_cols(Pm['rw_mu'][l]))
    for name in ('rw_w0', 'rw_a0', 'rw_k_k', 'rw_k_a', 'rw_r_k', 'rw_ln_w', 'rw_ln_b'):
        W[name] = row(Pm[name][l])
    W['rw_w_up'] = _pad_rows(Pm['rw_w_up'][l], RW_LORA_PAD).astype(bf16)
    W['rw_a_up'] = _pad_rows(Pm['rw_a_up'][l], RW_LORA_PAD).astype(bf16)
    W['rw_g_up'] = Pm['rw_g_up'][l].astype(bf16)
    hid = jnp.arange(RW_W) // RW_DH
    W['ones_blk'] = (hid[:, None] == hid[None, :]).astype(bf16)
    W['small_bias'] = row(jnp.pad(jnp.concatenate([Pm['ml_b_i'][l], Pm['ml_b_f'][l], Pm['fx_b_f'][l]]),
                                  (0, LANES - 2 * ML_H - FX_H)))
    W['ml_conv_w'] = _pad_rows(Pm['ml_conv_w'][l], SUBLANES)
    W['ml_conv_b'] = row(Pm['ml_conv_b'][l])
    W['ml_gn_w'] = row(Pm['ml_gn_w'][l])
    idx = jnp.arange(CHUNK)
    W['tri'] = (idx[None, :] <= idx[:, None]).astype(bf16)
    W['triu'] = (idx[:, None] > idx[None, :]).astype(bf16)
    if n_pages:
        r = jnp.arange(n_pages * FX_H)
        W['later'] = ((r[:, None] % FX_H == r[None, :] % FX_H) & (r[None, :] // FX_H > r[:, None] // FX_H)).astype(bf16)
    W['w_br_rwkv'] = Pm['w_br_rwkv'][l].astype(bf16)
    W['w_br_mlstm'] = Pm['w_br_mlstm'][l].astype(bf16)
    W['w_br_fox'] = Pm['w_br_fox'][l].astype(bf16)
    W['ffn_conv_w'] = _pad_rows(_pad_cols(Pm['ffn_conv_w'][l], D_FF_PAD), SUBLANES)
    W['ffn_conv_b'] = _pad_cols(row(Pm['ffn_conv_b'][l]), D_FF_PAD)
    return W


def _stacked_weights(w_out, ffn_w_gate, ffn_w_val, ffn_w_down):
    padc = lambda w: jnp.pad(w, ((0, 0), (0, 0), (0, D_FF_PAD - D_FF)))
    return dict(w_out=w_out.astype(bf16),
                ffn_w_gv=jnp.concatenate([padc(ffn_w_gate), padc(ffn_w_val)], axis=2).astype(bf16),
                ffn_w_down=jnp.pad(ffn_w_down, ((0, 0), (0, D_FF_PAD - D_FF), (0, 0))).astype(bf16))


def kernel(x_prompt, x_sample, cache_fox_k, cache_fox_v, cache_fox_logf, state_rwkv_shift, state_rwkv_wkv,
           state_mlstm_conv, state_mlstm_c, state_mlstm_n, state_mlstm_m, state_ffn_conv, page_table,
           c_prompt, c_sample, w_ada, b_ada, norm_pre_mix, norm_post_mix, norm_pre_ffn, norm_post_ffn, w_in,
           rw_mu, rw_w0, rw_w_up, rw_a0, rw_a_up, rw_g_up, rw_k_k, rw_k_a, rw_r_k, rw_ln_w, rw_ln_b,
           ml_conv_w, ml_conv_b, ml_b_i, ml_b_f, ml_gn_w, fx_b_f, w_br_rwkv, w_br_mlstm, w_br_fox, w_out,
           ffn_w_gate, ffn_w_val, ffn_conv_w, ffn_conv_b, ffn_w_down):
    Pm = dict(norm_pre_mix=norm_pre_mix, norm_post_mix=norm_post_mix, norm_pre_ffn=norm_pre_ffn,
              norm_post_ffn=norm_post_ffn, w_in=w_in, rw_mu=rw_mu, rw_w0=rw_w0, rw_w_up=rw_w_up, rw_a0=rw_a0,
              rw_a_up=rw_a_up, rw_g_up=rw_g_up, rw_k_k=rw_k_k, rw_k_a=rw_k_a,
              rw_r_k=rw_r_k.reshape(DEPTH, RW_W), rw_ln_w=rw_ln_w, rw_ln_b=rw_ln_b, ml_conv_w=ml_conv_w,
              ml_conv_b=ml_conv_b, ml_b_i=ml_b_i, ml_b_f=ml_b_f, ml_gn_w=ml_gn_w, fx_b_f=fx_b_f,
              w_br_rwkv=w_br_rwkv, w_br_mlstm=w_br_mlstm, w_br_fox=w_br_fox, w_out=w_out, ffn_w_gate=ffn_w_gate,
              ffn_w_val=ffn_w_val, ffn_conv_w=ffn_conv_w, ffn_conv_b=ffn_conv_b, ffn_w_down=ffn_w_down)
    Bp, Tpr, D = x_prompt.shape
    Bs, Ts, _ = x_sample.shape
    cache_k, cache_v = cache_fox_k, cache_fox_v
    cache_lft = jnp.swapaxes(cache_fox_logf, -1, -2)

    stacked = _stacked_weights(w_out, ffn_w_gate, ffn_w_val, ffn_w_down)
    zeros = lambda *s: jnp.zeros(s, f32)
    xp = x_prompt.reshape(Bp * Tpr, D)
    xs = x_sample.reshape(Bs * Ts, D)
    c_all = jnp.pad(jnp.concatenate([c_prompt, c_sample], axis=0), ((0, 16 - Bp - Bs), (0, 0)))
    new_p = {n: [] for n in STATE_NAMES}
    new_s = {n: [] for n in STATE_NAMES}
    for l in range(DEPTH):
        W = dict(_layer_weights(Pm, l, page_table.shape[1]), **stacked)
        mod = ada_mod(c_all, w_ada, b_ada[:, None, :], l)
        st_p = dict(rw_shift=zeros(Bp, RW_COLS), rw_wkv=zeros(Bp, RW_H, RW_DH, RW_DH),
                    ml_conv=zeros(Bp, ML_CONV - 1, ML_QK), ml_c=zeros(Bp, ML_H, ML_DH, ML_DH),
                    ml_n=zeros(Bp, ML_H, ML_DH), ml_m=zeros(Bp, ML_H), ffn_conv=zeros(Bp, FFN_CONV - 1, D_FF))
        st_s = dict(rw_shift=state_rwkv_shift[l], rw_wkv=state_rwkv_wkv[l], ml_conv=state_mlstm_conv[l],
                    ml_c=state_mlstm_c[l], ml_n=state_mlstm_n[l], ml_m=state_mlstm_m[l],
                    ffn_conv=state_ffn_conv[l])
        xp, lp = _layer(xp, mod[:Bp], st_p, None, W, None, l, Bp, Tpr)
        xs, ls = _layer(xs, mod[Bp:Bp + Bs], st_s, (cache_k, cache_v, cache_lft), W, page_table, l, Bs, Ts)
        for n in STATE_NAMES:
            new_p[n].append(lp[n])
            new_s[n].append(ls[n])
    sp = {n: jnp.stack(v) for n, v in new_p.items()}
    ss = {n: jnp.stack(v) for n, v in new_s.items()}
    return (xp.reshape(Bp, Tpr, D), xs.reshape(Bs, Ts, D),
            sp['fox_k'], ss['fox_k'], sp['fox_v'], ss['fox_v'], sp['fox_logf'], ss['fox_logf'],
            sp['rw_shift'], ss['rw_shift'], sp['rw_wkv'], ss['rw_wkv'],
            sp['ml_conv'], ss['ml_conv'], sp['ml_c'], ss['ml_c'], sp['ml_n'], ss['ml_n'], sp['ml_m'], ss['ml_m'],
            sp['ffn_conv'], ss['ffn_conv'])
```

```python
import functools
import math

import jax
import jax.numpy as jnp
from jax import lax
from jax.experimental import pallas as pl
from jax.experimental.pallas import tpu as pltpu

f32 = jnp.float32
bf16 = jnp.bfloat16

D_MODEL = 2048
DEPTH = 2
PAGE_SIZE = 128
RW_W = D_MODEL // 4
RW_DH = 64
RW_H = RW_W // RW_DH
R_W = max(32, int(round(1.8 * D_MODEL ** 0.5 / 32)) * 32)
R_A = R_W
R_G = max(32, int(round(0.6 * D_MODEL ** 0.8 / 32)) * 32)
RW_COLS = 3 * RW_W + R_W + R_A + R_G
RW_LN_EPS = 64e-5
RW_DECAY_CLAMP = 0.5
ML_W = D_MODEL // 4
ML_H = 4
ML_DH = ML_W // ML_H
ML_QK = 2 * ML_W
ML_CONV = 4
ML_COLS = ML_QK + ML_W + 2 * ML_H + ML_W
ML_GN_EPS = 1e-5
FX_W = D_MODEL // 2
FX_DH = 128
FX_H = FX_W // FX_DH
FX_COLS = 3 * FX_W + FX_H
N_BRANCH = 3
GATE_COLS = N_BRANCH * D_MODEL
D_FF = ((8 * D_MODEL // 3 + 127) // 128) * 128
FFN_CONV = 3
N_MOD = 6
NORM_EPS = 1e-6

LANES = 128
SUBLANES = 8
VMEM_LIMIT = 56 * 1024 * 1024
CHUNK = 128
NEG = -1e30

RW_LORA_PAD = LANES
RW_PCOLS = 3 * RW_W + 2 * RW_LORA_PAD + R_G
OFF_RW = 0
OFF_ML = OFF_RW + RW_PCOLS
ML_PCOLS = ML_QK + 2 * ML_W
OFF_FXQ = OFF_ML + ML_PCOLS
P_COLS = OFF_FXQ + FX_W
SM_LI, SM_LF, SM_FX = 0, ML_H, 2 * ML_H
D_FF_PAD = 5632
FF_TK = 1408


def _cparams(*sem):
    return pltpu.CompilerParams(dimension_semantics=sem, vmem_limit_bytes=VMEM_LIMIT)


def _split3(x):
    hi = x.astype(bf16)
    r = x - hi.astype(f32)
    mid = r.astype(bf16)
    lo = (r - mid.astype(f32)).astype(bf16)
    return hi, mid, lo


def _dot(a, b):
    return jnp.dot(a, b, preferred_element_type=f32)


def _dot_nt(a, b):
    return lax.dot_general(a, b, (((1,), (1,)), ((), ())), preferred_element_type=f32)


def _dot_r(x, m):
    hi, mid, lo = _split3(x)
    return _dot(hi, m) + _dot(mid, m) + _dot(lo, m)


def _dot_r2(x, m):
    hi = x.astype(bf16)
    lo = (x - hi.astype(f32)).astype(bf16)
    return _dot(hi, m) + _dot(lo, m)


def _dot_l(m, x):
    hi, mid, lo = _split3(x)
    return _dot(m, hi) + _dot(m, mid) + _dot(m, lo)


def _sigmoid(x):
    return 0.5 * (jnp.tanh(0.5 * x) + 1.0)


def _log_sigmoid(x):
    return jnp.minimum(x, 0.0) - jnp.log(1.0 + jnp.exp(-jnp.abs(x)))


def _softplus(x):
    return jnp.maximum(x, 0.0) + jnp.log(1.0 + jnp.exp(-jnp.abs(x)))


def _iota(shape, dim):
    return lax.broadcasted_iota(jnp.int32, shape, dim)


def _lane_tiles(x):
    return [x[:, i:i + LANES] for i in range(0, x.shape[-1], LANES)]


def _row_max(x):
    return jnp.max(functools.reduce(jnp.maximum, _lane_tiles(x)), axis=-1, keepdims=True)


def _row_sum(x):
    return jnp.sum(functools.reduce(jnp.add, _lane_tiles(x)), axis=-1, keepdims=True)


def _softmax_step(scores, m_old, l_old):
    tiles = [_lane_tiles(s) for s in scores]
    flat = [t for ts in tiles for t in ts]
    m_new = jnp.maximum(m_old, jnp.max(functools.reduce(jnp.maximum, flat), axis=-1, keepdims=True))
    alpha = jnp.exp(m_old - m_new)
    p_tiles = [[jnp.exp(t - m_new) for t in ts] for ts in tiles]
    total = functools.reduce(jnp.add, [t for ts in p_tiles for t in ts])
    l_new = alpha * l_old + jnp.sum(total, axis=-1, keepdims=True)
    ps = [jnp.concatenate([t.astype(bf16) for t in ts], axis=1) for ts in p_tiles]
    return m_new, l_new, alpha, ps


def _group_vec(v, T, tm):
    B, D = v.shape
    if tm <= T:
        per = T // tm
        return v[:, None, :], (None, 1, D), lambda i, *_: (i // per, 0, 0)
    rows = jnp.repeat(v, T, axis=0)
    return rows.reshape(-1, tm, D), (None, tm, D), lambda i, *_: (i, 0, 0)


def _row_tile(n_rows, T, want):
    return want if T >= want else n_rows


def _ada_kernel(c_ref, w_ref, b_ref, o_ref):
    c = c_ref[...]
    a = (c * _sigmoid(c)).astype(bf16)
    o_ref[...] = _dot(a, w_ref[...].astype(bf16)) + b_ref[...]


def ada_mod(c, w_all, b_all, layer):
    M, K = c.shape
    N = w_all.shape[2]
    tn = 1024
    return pl.pallas_call(
        _ada_kernel, grid=(N // tn,),
        in_specs=[pl.BlockSpec((M, K), lambda j: (0, 0)),
                  pl.BlockSpec((None, K, tn), lambda j: (layer, 0, j)),
                  pl.BlockSpec((None, 1, tn), lambda j: (layer, 0, j))],
        out_specs=pl.BlockSpec((M, tn), lambda j: (0, j)),
        out_shape=jax.ShapeDtypeStruct((M, N), f32),
        compiler_params=_cparams("parallel"), name="ada_mod")(c, w_all, b_all)


def _norm_mod_kernel(x_ref, g_ref, sc_ref, sh_ref, o_ref):
    x = x_ref[...]
    ms = jnp.mean(x * x, axis=-1, keepdims=True)
    y = x * lax.rsqrt(ms + NORM_EPS) * g_ref[...]
    o_ref[...] = (y * (1.0 + sc_ref[...]) + sh_ref[...]).astype(o_ref.dtype)


def norm_mod(x, g, scale, shift, T):
    N, D = x.shape
    tm = _row_tile(N, T, 1024)
    sc, sc_blk, sc_map = _group_vec(scale, T, tm)
    sh, _, _ = _group_vec(shift, T, tm)
    return pl.pallas_call(
        _norm_mod_kernel, grid=(N // tm,),
        in_specs=[pl.BlockSpec((tm, D), lambda i: (i, 0)),
                  pl.BlockSpec((1, D), lambda i: (0, 0)),
                  pl.BlockSpec(sc_blk, sc_map), pl.BlockSpec(sc_blk, sc_map)],
        out_specs=pl.BlockSpec((tm, D), lambda i: (i, 0)),
        out_shape=jax.ShapeDtypeStruct((N, D), bf16),
        compiler_params=_cparams("parallel"), name="norm_mod")(x, g.reshape(1, D), sc, sh)


def _mm_kernel(a_ref, w_ref, o_ref, *, gate):
    r = _dot(a_ref[...], w_ref[...])
    o_ref[...] = (_sigmoid(r) if gate else r).astype(o_ref.dtype)


def _layer_spec(block, index_map, layer, **kw):
    if layer is None:
        return pl.BlockSpec(block, index_map, **kw)
    return pl.BlockSpec((None,) + block, lambda *idx: (layer,) + index_map(*idx), **kw)


def matmul(a, w, tn, out_dtype=f32, gate=False, layer=None):
    M, K = a.shape
    N = w.shape[-1]
    tm = min(M, 2048)
    return pl.pallas_call(
        functools.partial(_mm_kernel, gate=gate), grid=(M // tm, N // tn),
        in_specs=[pl.BlockSpec((tm, K), lambda i, j: (i, 0)),
                  _layer_spec((K, tn), lambda i, j: (0, j), layer)],
        out_specs=pl.BlockSpec((tm, tn), lambda i, j: (i, j)),
        out_shape=jax.ShapeDtypeStruct((M, N), out_dtype),
        compiler_params=_cparams("parallel", "arbitrary"), name="matmul")(a, w)


def _mm_norm_res_kernel(a_ref, w_ref, x_ref, g_ref, gate_ref, o_ref):
    f = _dot(a_ref[...].astype(bf16), w_ref[...])
    ms = jnp.mean(f * f, axis=-1, keepdims=True)
    y = f * lax.rsqrt(ms + NORM_EPS) * g_ref[...]
    o_ref[...] = x_ref[...] + gate_ref[...] * y


def mm_norm_res(a, w, layer, x, g, gate, T):
    M, K = a.shape
    D = w.shape[-1]
    tm = _row_tile(M, T, 512 if K <= D_MODEL else 256)
    gt, gt_blk, gt_map = _group_vec(gate, T, tm)
    return pl.pallas_call(
        _mm_norm_res_kernel, grid=(M // tm,),
        in_specs=[pl.BlockSpec((tm, K), lambda i: (i, 0)),
                  _layer_spec((K, D), lambda i: (0, 0), layer, pipeline_mode=pl.Buffered(1)),
                  pl.BlockSpec((tm, D), lambda i: (i, 0)),
                  pl.BlockSpec((1, D), lambda i: (0, 0)),
                  pl.BlockSpec(gt_blk, gt_map)],
        out_specs=pl.BlockSpec((tm, D), lambda i: (i, 0)),
        out_shape=jax.ShapeDtypeStruct((M, D), f32),
        compiler_params=_cparams("parallel"), name="mm_norm_res")(
            a, w, x, g.reshape(1, D), gt)


def _merge_kernel(orw_ref, oml_ref, ofx_ref, wrw_ref, wml_ref, wfx_ref, g0_ref, g1_ref, g2_ref, o_ref):
    m = g0_ref[...] * _dot(orw_ref[...].astype(bf16), wrw_ref[...])
    m += g1_ref[...] * _dot(oml_ref[...].astype(bf16), wml_ref[...])
    m += g2_ref[...] * _dot(ofx_ref[...].astype(bf16), wfx_ref[...])
    o_ref[...] = m.astype(o_ref.dtype)


def merge(o_rw, o_ml, o_fx, w_rw, w_ml, w_fx, G):
    N = o_rw.shape[0]
    D = D_MODEL
    tm = min(N, 1024)
    tn = 512
    nb = D // tn
    gspec = lambda b: pl.BlockSpec((tm, tn), lambda i, j: (i, b * nb + j))
    return pl.pallas_call(
        _merge_kernel, grid=(N // tm, nb),
        in_specs=[pl.BlockSpec((tm, RW_W), lambda i, j: (i, 0)),
                  pl.BlockSpec((tm, ML_W), lambda i, j: (i, 0)),
                  pl.BlockSpec((tm, FX_W), lambda i, j: (i, 0)),
                  pl.BlockSpec((RW_W, tn), lambda i, j: (0, j)),
                  pl.BlockSpec((ML_W, tn), lambda i, j: (0, j)),
                  pl.BlockSpec((FX_W, tn), lambda i, j: (0, j)),
                  gspec(0), gspec(1), gspec(2)],
        out_specs=pl.BlockSpec((tm, tn), lambda i, j: (i, j)),
        out_shape=jax.ShapeDtypeStruct((N, D), bf16),
        compiler_params=_cparams("parallel", "arbitrary"), name="merge")(
            o_rw, o_ml, o_fx, w_rw, w_ml, w_fx, G, G, G)


def _ffn_act_kernel(a_ref, halo_ref, buf_ref, val_ref, cw_ref, cb_ref, o_ref, scr_ref, *, per, tm):
    first = (pl.program_id(0) % per) == 0
    scr_ref[0:SUBLANES, :] = jnp.where(first, buf_ref[...], halo_ref[...])
    scr_ref[SUBLANES:SUBLANES + tm, :] = a_ref[...]
    y = cb_ref[...]
    for j in range(FFN_CONV):
        off = SUBLANES - (FFN_CONV - 1) + j
        y = y + scr_ref[off:off + tm, :] * cw_ref[j:j + 1, :]
    c0 = math.sqrt(2.0 / math.pi)
    gelu = 0.5 * y * (1.0 + jnp.tanh(c0 * (y + 0.044715 * (y * y * y))))
    o_ref[...] = (gelu * val_ref[...]).astype(o_ref.dtype)


def ffn_act(av, buf8, conv_w, conv_b, T):
    N = av.shape[0]
    tm = min(T, 512)
    per = T // tm
    tn = FF_TK
    nj = D_FF_PAD // tn
    hb = tm // SUBLANES
    out_dtype = bf16 if tm % 16 == 0 else f32
    return pl.pallas_call(
        functools.partial(_ffn_act_kernel, per=per, tm=tm), grid=(N // tm, nj),
        in_specs=[pl.BlockSpec((tm, tn), lambda i, j: (i, j)),
                  pl.BlockSpec((SUBLANES, tn), lambda i, j: (jnp.maximum(i * hb - 1, 0), j)),
                  pl.BlockSpec((None, SUBLANES, tn), lambda i, j: (i // per, 0, j)),
                  pl.BlockSpec((tm, tn), lambda i, j: (i, nj + j)),
                  pl.BlockSpec((SUBLANES, tn), lambda i, j: (0, j)),
                  pl.BlockSpec((1, tn), lambda i, j: (0, j))],
        out_specs=pl.BlockSpec((tm, tn), lambda i, j: (i, j)),
        out_shape=jax.ShapeDtypeStruct((N, D_FF_PAD), out_dtype),
        scratch_shapes=[pltpu.VMEM((tm + SUBLANES, tn), f32)],
        compiler_params=_cparams("parallel", "arbitrary"), name="ffn_act")(
            av, av, buf8, av, conv_w, conv_b)


BF16_ROWS = 16


def _ffn_up_kernel(z_ref, zh_ref, wg_ref, wv_ref, buf_ref, cw_ref, cb_ref, h_ref, tail_ref, scr_ref, *, per, tm):
    first = (pl.program_id(0) % per) == 0
    wg = wg_ref[...]
    a = _dot(z_ref[...], wg)
    val = _dot(z_ref[...], wv_ref[...])
    a_prev = _dot(zh_ref[...], wg)[BF16_ROWS - SUBLANES:, :]
    scr_ref[0:SUBLANES, :] = jnp.where(first, buf_ref[...], a_prev)
    scr_ref[SUBLANES:SUBLANES + tm, :] = a
    y = cb_ref[...]
    for j in range(FFN_CONV):
        off = SUBLANES - (FFN_CONV - 1) + j
        y = y + scr_ref[off:off + tm, :] * cw_ref[j:j + 1, :]
    c0 = math.sqrt(2.0 / math.pi)
    gelu = 0.5 * y * (1.0 + jnp.tanh(c0 * (y + 0.044715 * (y * y * y))))
    h_ref[...] = (gelu * val).astype(h_ref.dtype)
    tail_ref[...] = a[tm - SUBLANES:tm, :]


def ffn_up(z, w_gv, layer, buf8, conv_w, conv_b, T):
    N, D = z.shape
    tm = min(T, 1024)
    per = T // tm
    tn = 512
    nj = D_FF_PAD // tn
    hb = tm // BF16_ROWS
    return pl.pallas_call(
        functools.partial(_ffn_up_kernel, per=per, tm=tm), grid=(N // tm, nj),
        in_specs=[pl.BlockSpec((tm, D), lambda i, j: (i, 0)),
                  pl.BlockSpec((BF16_ROWS, D), lambda i, j: (jnp.maximum(i * hb - 1, 0), 0)),
                  _layer_spec((D, tn), lambda i, j: (0, j), layer),
                  _layer_spec((D, tn), lambda i, j: (0, nj + j), layer),
                  pl.BlockSpec((None, SUBLANES, tn), lambda i, j: (i // per, 0, j)),
                  pl.BlockSpec((SUBLANES, tn), lambda i, j: (0, j)),
                  pl.BlockSpec((1, tn), lambda i, j: (0, j))],
        out_specs=[pl.BlockSpec((tm, tn), lambda i, j: (i, j)),
                   pl.BlockSpec((SUBLANES, tn), lambda i, j: (i, j))],
        out_shape=[jax.ShapeDtypeStruct((N, D_FF_PAD), bf16),
                   jax.ShapeDtypeStruct((N // tm * SUBLANES, D_FF_PAD), f32)],
        scratch_shapes=[pltpu.VMEM((tm + SUBLANES, tn), f32)],
        compiler_params=_cparams("parallel", "arbitrary"), name="ffn_up")(
            z, z, w_gv, w_gv, buf8, conv_w, conv_b)


def _head_sum(x, ones_blk, terms=3):
    return _dot_r(x, ones_blk) if terms == 3 else _dot_r2(x, ones_blk)


def _rwkv_prep_kernel(p_ref, halo_ref, sp_ref, mu_ref, w0_ref, a0_ref, kk_ref, ka_ref, rk_ref,
                      wup_ref, aup_ref, gup_ref, ones_ref,
                      nkk_o, wr_o, w_o, b_o, k_o, v_o, br_o, kr_o, rkr_o, g_o, scr_ref, *, per, tm):
    first = (pl.program_id(0) % per) == 0
    scr_ref[0:SUBLANES, :] = jnp.where(first, sp_ref[...], halo_ref[...])
    p = p_ref[...]
    scr_ref[SUBLANES:SUBLANES + tm, :] = p
    prev = scr_ref[SUBLANES - 1:SUBLANES - 1 + tm, :]
    xs = p + (prev - p) * mu_ref[...]
    W = RW_W
    r, k, v = xs[:, 0:W], xs[:, W:2 * W], xs[:, 2 * W:3 * W]
    o = 3 * W
    dw = xs[:, o:o + RW_LORA_PAD]
    da = xs[:, o + RW_LORA_PAD:o + 2 * RW_LORA_PAD]
    dg = xs[:, o + 2 * RW_LORA_PAD:o + 2 * RW_LORA_PAD + R_G]
    w_raw = -_softplus(-(w0_ref[...] + _dot(jnp.tanh(dw).astype(bf16), wup_ref[...]))) - RW_DECAY_CLAMP
    w = jnp.exp(-jnp.exp(w_raw))
    a = _sigmoid(a0_ref[...] + _dot(da.astype(bf16), aup_ref[...]))
    g = _dot(_sigmoid(dg).astype(bf16), gup_ref[...])
    ones_blk = ones_ref[...]
    kk = k * kk_ref[...]
    nrm = jnp.sqrt(_head_sum(kk * kk, ones_blk))
    kk = kk / jnp.maximum(nrm, 1e-12)
    k2 = k * (1.0 + (a - 1.0) * ka_ref[...])
    b = kk * a
    nkk_o[...] = -kk
    wr_o[...] = w * r
    w_o[...] = w
    b_o[...] = b
    k_o[...] = k2
    v_o[...] = v
    br_o[...] = _head_sum(b * r, ones_blk, terms=2)
    kr_o[...] = _head_sum(k2 * r, ones_blk, terms=2)
    rkr_o[...] = _head_sum(r * k2 * rk_ref[...], ones_blk, terms=2)
    g_o[...] = g


def rwkv_prep(P, col_blk, sp8, mu, w0, a0, k_k, k_a, r_k, w_up, a_up, g_up, ones_blk, T):
    N = P.shape[0]
    tm = min(T, 256)
    per = T // tm
    hb = tm // SUBLANES
    C = RW_PCOLS
    vec = lambda n: pl.BlockSpec((1, n), lambda i: (0, 0))
    full = lambda a: pl.BlockSpec(a.shape, lambda i: (0, 0))
    out = jax.ShapeDtypeStruct((N, RW_W), f32)
    ospec = pl.BlockSpec((tm, RW_W), lambda i: (i, 0))
    return pl.pallas_call(
        functools.partial(_rwkv_prep_kernel, per=per, tm=tm), grid=(N // tm,),
        in_specs=[pl.BlockSpec((tm, C), lambda i: (i, col_blk)),
                  pl.BlockSpec((SUBLANES, C), lambda i: (jnp.maximum(i * hb - 1, 0), col_blk)),
                  pl.BlockSpec((None, SUBLANES, C), lambda i: (i // per, 0, 0)),
                  vec(C), vec(RW_W), vec(RW_W), vec(RW_W), vec(RW_W), vec(RW_W),
                  full(w_up), full(a_up), full(g_up), full(ones_blk)],
        out_specs=[ospec] * 10, out_shape=[out] * 10,
        scratch_shapes=[pltpu.VMEM((tm + SUBLANES, C), f32)],
        compiler_params=_cparams("parallel"), name="rwkv_prep")(
            P, P, sp8, mu, w0, a0, k_k, k_a, r_k, w_up, a_up, g_up, ones_blk)


N_PAIR = RW_H // 2


def _rwkv_scan_kernel(nkk_ref, wr_ref, w_ref, b_ref, k_ref, br_ref, v_ref, s0_ref, ones_ref,
                      z_ref, so_ref, S_ref, vT_ref, *, n_sub, n_steps, nb):
    c = pl.program_id(1)
    chains = [(bb, j) for bb in range(nb) for j in range(N_PAIR)]

    @pl.when(c == 0)
    def _():
        S_ref[...] = s0_ref[...]

    if n_steps < CHUNK:
        z_ref[...] = jnp.zeros_like(z_ref)

    lo = _iota((1, LANES), 1) < RW_DH
    lane = _iota((1, LANES), 1)
    diag = _iota((RW_DH, LANES), 1) % RW_DH == _iota((RW_DH, LANES), 0)
    pair_ones = ones_ref[...]

    def as_row(x):
        return jnp.sum(jnp.where(diag, x, 0.0), axis=0, keepdims=True)

    for sub in range(n_sub):
        r0 = sub * CHUNK
        for q, (bb, j) in enumerate(chains):
            vT_ref[q] = v_ref[bb, r0:r0 + CHUNK, j * LANES:(j + 1) * LANES].T

        def group(g, carry):
            base = pl.multiple_of(r0 + g * SUBLANES, SUBLANES)
            tiles = [[ref[bb, pl.ds(base, SUBLANES), j * LANES:(j + 1) * LANES]
                      for ref in (nkk_ref, wr_ref, w_ref, b_ref, k_ref, br_ref)] for bb, j in chains]
            z_rows = [[] for _ in chains]
            for i in range(SUBLANES):
                tmask = lane == g * SUBLANES + i
                for q in range(len(chains)):
                    nkk_r, wr_r, w_r, b_r, k_r, br_r = [tl[i:i + 1, :] for tl in tiles[q]]
                    S = S_ref[q]
                    x = jnp.concatenate([S * nkk_r, S * wr_r], axis=0).astype(bf16)
                    red = _dot(x, pair_ones)
                    sa = red[0:RW_DH]
                    zz = red[RW_DH:2 * RW_DH] + sa * br_r
                    va = jnp.sum(jnp.where(tmask, vT_ref[q, 0:RW_DH, :], 0.0), axis=-1, keepdims=True)
                    vb = jnp.sum(jnp.where(tmask, vT_ref[q, RW_DH:2 * RW_DH, :], 0.0), axis=-1, keepdims=True)
                    vp = jnp.where(lo, va, vb)
                    S_ref[q] = S * w_r + sa * b_r + vp * k_r
                    z_rows[q].append(as_row(zz))
            for q, (bb, j) in enumerate(chains):
                cs = slice(j * LANES, (j + 1) * LANES)
                z_ref[bb, pl.ds(base, SUBLANES), cs] = jnp.concatenate(z_rows[q], axis=0)
            return carry

        lax.fori_loop(0, n_steps // SUBLANES, group, 0)

    so_ref[...] = S_ref[...]


def rwkv_scan(nkk, wr, w, b, k, br, v, s0, pair_ones, n_steps):
    B, Tp, W = nkk.shape
    nb = 2 if B % 2 == 0 else 1
    tc = min(Tp, 4 * CHUNK)
    n_sub = tc // CHUNK
    rows = pl.BlockSpec((None, nb, tc, W), lambda bi, c: (bi, 0, c, 0))
    st = pl.BlockSpec((None, nb * N_PAIR, RW_DH, LANES), lambda bi, c: (bi, 0, 0, 0))
    grp = lambda a: a.reshape(B // nb, nb, Tp, W)
    z, so = pl.pallas_call(
        functools.partial(_rwkv_scan_kernel, n_sub=n_sub, n_steps=min(n_steps, CHUNK), nb=nb),
        grid=(B // nb, Tp // tc),
        in_specs=[rows] * 7 + [st, pl.BlockSpec((LANES, LANES), lambda bi, c: (0, 0))],
        out_specs=[rows, st],
        out_shape=[jax.ShapeDtypeStruct((B // nb, nb, Tp, W), f32),
                   jax.ShapeDtypeStruct((B // nb, nb * N_PAIR, RW_DH, LANES), f32)],
        scratch_shapes=[pltpu.VMEM((nb * N_PAIR, RW_DH, LANES), f32), pltpu.VMEM((nb * N_PAIR, LANES, LANES), f32)],
        compiler_params=_cparams("parallel", "arbitrary"), name="rwkv_scan")(
            grp(nkk), grp(wr), grp(w), grp(b), grp(k), grp(br), grp(v),
            s0.reshape(B // nb, nb * N_PAIR, RW_DH, LANES), pair_ones)
    return z.reshape(B, Tp, W), so.reshape(s0.shape)


def _rwkv_post_kernel(z_ref, v_ref, kr_ref, rkr_ref, g_ref, lnw_ref, lnb_ref, ones_ref, o_ref):
    ones_blk = ones_ref[...]
    v = v_ref[...]
    y = z_ref[...] + v * kr_ref[...]
    mu = _head_sum(y, ones_blk, terms=2) * (1.0 / RW_DH)
    yc = y - mu
    var = _head_sum(yc * yc, ones_blk, terms=2) * (1.0 / RW_DH)
    yn = yc * lax.rsqrt(var + RW_LN_EPS) * lnw_ref[...] + lnb_ref[...]
    o_ref[...] = ((yn + rkr_ref[...] * v) * g_ref[...]).astype(o_ref.dtype)


def rwkv_post(z, v, kr, rkr, g, ln_w, ln_b, ones_blk):
    N, W = z.shape
    tm = min(N, 512)
    rows = pl.BlockSpec((tm, W), lambda i: (i, 0))
    vec = pl.BlockSpec((1, W), lambda i: (0, 0))
    return pl.pallas_call(
        _rwkv_post_kernel, grid=(N // tm,),
        in_specs=[rows] * 5 + [vec, vec, pl.BlockSpec(ones_blk.shape, lambda i: (0, 0))],
        out_specs=rows, out_shape=jax.ShapeDtypeStruct((N, W), bf16),
        compiler_params=_cparams("parallel"), name="rwkv_post")(z, v, kr, rkr, g, ln_w, ln_b, ones_blk)


def _mlstm_kernel(main_ref, halo_ref, buf_ref, sm_ref, bias_ref, cw_ref, cb_ref, gn_ref, tri_ref,
                  c0_ref, n0_ref, m0_ref,
                  o_ref, lf_o, F_o, FT_o, c_o, n_o, m_o,
                  scr_ref, ct_ref, n_ref, m_ref, carry_ref, *, t_real):
    c = pl.program_id(1)
    L = CHUNK

    @pl.when(c == 0)
    def _():
        ct_ref[...] = c0_ref[...]
        n_ref[...] = n0_ref[...]
        m_ref[...] = m0_ref[...]
        carry_ref[...] = jnp.zeros_like(carry_ref)

    main = main_ref[...]
    scr_ref[0:SUBLANES, :] = jnp.where(c == 0, buf_ref[...], halo_ref[...])
    scr_ref[SUBLANES:SUBLANES + L, :] = main[:, 0:ML_QK]
    qk = cb_ref[...]
    for j in range(ML_CONV):
        off = SUBLANES - (ML_CONV - 1) + j
        qk = qk + scr_ref[off:off + L, :] * cw_ref[j:j + 1, :]
    qk = qk * _sigmoid(qk)
    q_all = qk[:, 0:ML_W]
    k_all = qk[:, ML_W:ML_QK] * (ML_DH ** -0.5)
    v_all = main[:, ML_QK:ML_QK + ML_W]
    og_all = main[:, ML_QK + ML_W:ML_QK + 2 * ML_W]

    valid = (c * L + _iota((L, 1), 0)) < t_real
    pre = sm_ref[...] + bias_ref[...]
    li_all = jnp.where(valid, pre, NEG)
    lf_all = jnp.where(valid, _log_sigmoid(pre), 0.0)
    cum = _dot_l(tri_ref[...], lf_all)
    F = cum + carry_ref[...]
    carry_ref[...] = F[L - 1:L, :]
    lf_o[...] = lf_all
    F_o[...] = F
    FT_o[...] = F.T
    liT = li_all.T
    cumT = cum.T

    row = _iota((L, L), 0)
    col = _iota((L, L), 1)
    causal = col <= row
    for h in range(ML_H):
        hs = slice(h * ML_DH, (h + 1) * ML_DH)
        b_col = cum[:, SM_LF + h:SM_LF + h + 1]
        b_row = cumT[SM_LF + h:SM_LF + h + 1, :]
        li_col = li_all[:, SM_LI + h:SM_LI + h + 1]
        li_row = liT[SM_LI + h:SM_LI + h + 1, :]
        m_prev = m_ref[h][:, 0:1]
        d = jnp.where(causal, b_col - b_row + li_row, NEG)
        inter = b_col + m_prev
        m_t = jnp.maximum(inter, jnp.max(d, axis=-1, keepdims=True))
        w_intra = jnp.exp(d - m_t)
        w_state = jnp.exp(inter - m_t)
        q = q_all[:, hs]
        k = k_all[:, hs]
        v = v_all[:, hs]
        qb, kb = q.astype(bf16), k.astype(bf16)
        s = _dot_nt(qb, kb) * w_intra
        ct = ct_ref[h]
        num = w_state * _dot(qb, ct.astype(bf16)) + _dot(s.astype(bf16), v.astype(bf16))
        n_row = n_ref[h]
        den = w_state * jnp.sum(q * n_row, axis=-1, keepdims=True) + jnp.sum(s, axis=-1, keepdims=True)
        hh = num / jnp.maximum(jnp.abs(den), jnp.exp(-m_t))
        g_end = b_col[L - 1:L, :]
        lw_s = g_end - b_col + li_col
        m_new = jnp.maximum(g_end + m_prev, jnp.max(lw_s, axis=0, keepdims=True))
        w_s = jnp.exp(lw_s - m_new)
        decay = jnp.exp(g_end + m_prev - m_new)
        ct_ref[h] = decay * ct + _dot(kb.T, (w_s * v).astype(bf16))
        n_ref[h] = decay * n_row + jnp.sum(w_s * k, axis=0, keepdims=True)
        m_ref[h] = jnp.broadcast_to(m_new, (1, LANES))
        mu = jnp.mean(hh, axis=-1, keepdims=True)
        hc = hh - mu
        var = jnp.mean(hc * hc, axis=-1, keepdims=True)
        hn = hc * lax.rsqrt(var + ML_GN_EPS) * gn_ref[:, hs]
        o_ref[:, hs] = (_sigmoid(og_all[:, hs]) * hn).astype(o_ref.dtype)

    c_o[...] = ct_ref[...]
    n_o[...] = n_ref[...]
    m_o[...] = m_ref[...]


def mlstm_fox_prep(main_arr, main_blk, small_arr, small_blk, buf8, bias_row, conv_w, conv_b, gn_w, tri,
                   c0t, n0, m0, B, Tp, t_real):
    L = CHUNK
    nc = Tp // L
    hb = L // SUBLANES
    st_c = pl.BlockSpec((None, ML_H, ML_DH, ML_DH), lambda b, c: (b, 0, 0, 0))
    st_n = pl.BlockSpec((None, ML_H, 1, ML_DH), lambda b, c: (b, 0, 0, 0))
    rows = lambda w: pl.BlockSpec((L, w), lambda b, c: (b * nc + c, 0))
    vec = lambda n: pl.BlockSpec((1, n), lambda b, c: (0, 0))
    N = B * Tp
    return pl.pallas_call(
        functools.partial(_mlstm_kernel, t_real=t_real), grid=(B, nc),
        in_specs=[pl.BlockSpec((L, ML_PCOLS), lambda b, c: (b * nc + c, main_blk)),
                  pl.BlockSpec((SUBLANES, ML_QK), lambda b, c: (jnp.maximum((b * nc + c) * hb - 1, 0), 2 * main_blk)),
                  pl.BlockSpec((None, SUBLANES, ML_QK), lambda b, c: (b, 0, 0)),
                  pl.BlockSpec((L, LANES), lambda b, c: (b * nc + c, small_blk)),
                  vec(LANES),
                  pl.BlockSpec((SUBLANES, ML_QK), lambda b, c: (0, 0)),
                  vec(ML_QK), vec(ML_W),
                  pl.BlockSpec((L, L), lambda b, c: (0, 0)),
                  st_c, st_n, st_n],
        out_specs=[rows(ML_W), rows(LANES), rows(LANES),
                   pl.BlockSpec((None, LANES, L), lambda b, c: (b, 0, c)),
                   st_c, st_n, st_n],
        out_shape=[jax.ShapeDtypeStruct((N, ML_W), bf16),
                   jax.ShapeDtypeStruct((N, LANES), f32),
                   jax.ShapeDtypeStruct((N, LANES), f32),
                   jax.ShapeDtypeStruct((B, LANES, Tp), f32),
                   jax.ShapeDtypeStruct((B, ML_H, ML_DH, ML_DH), f32),
                   jax.ShapeDtypeStruct((B, ML_H, 1, ML_DH), f32),
                   jax.ShapeDtypeStruct((B, ML_H, 1, ML_DH), f32)],
        scratch_shapes=[pltpu.VMEM((L + SUBLANES, ML_QK), f32),
                        pltpu.VMEM((ML_H, ML_DH, ML_DH), f32),
                        pltpu.VMEM((ML_H, 1, ML_DH), f32),
                        pltpu.VMEM((ML_H, 1, LANES), f32),
                        pltpu.VMEM((1, LANES), f32)],
        compiler_params=_cparams("parallel", "arbitrary"), name="mlstm")(
            main_arr, main_arr, buf8, small_arr, bias_row, conv_w, conv_b, gn_w, tri, c0t, n0, m0)


FOX_HG = 4


def _fox_attn_kernel(q_ref, k_ref, v_ref, ft_ref, o_ref, m_ref, l_ref, acc_ref, *, tq, tk):
    h0 = pl.program_id(1) * FOX_HG
    qi = pl.program_id(2)
    m_ref[...] = jnp.full_like(m_ref, NEG)
    l_ref[...] = jnp.zeros_like(l_ref)
    acc_ref[...] = jnp.zeros_like(acc_ref)
    qbs = [(q_ref[:, u * FX_DH:(u + 1) * FX_DH] * (FX_DH ** -0.5)).astype(bf16) for u in range(FOX_HG)]

    def block(j, masked):
        k0 = pl.multiple_of(j * tk, tk)
        for u in range(FOX_HG):
            hs = slice(u * FX_DH, (u + 1) * FX_DH)
            kb = k_ref[pl.ds(k0, tk), hs].astype(bf16)
            vb = v_ref[pl.ds(k0, tk), hs].astype(bf16)
            s = _dot_nt(qbs[u], kb) - ft_ref[pl.ds(h0 + u, 1), pl.ds(k0, tk)]
            if masked:
                s = jnp.where(_iota((tq, tk), 1) + k0 <= qi * tq + _iota((tq, tk), 0), s, NEG)
            m_new, l_new, alpha, (p,) = _softmax_step([s], m_ref[u], l_ref[u])
            acc_ref[:, hs] = alpha * acc_ref[:, hs] + _dot(p, vb)
            m_ref[u] = m_new
            l_ref[u] = l_new

    n_full = (qi * tq) // tk

    def body(j, carry):
        block(j, False)
        return carry

    lax.fori_loop(0, n_full, body, 0)
    for d in range(max(tq // tk, 1)):
        block(n_full + d, True)
    for u in range(FOX_HG):
        hs = slice(u * FX_DH, (u + 1) * FX_DH)
        o_ref[:, hs] = (acc_ref[:, hs] / l_ref[u]).astype(o_ref.dtype)


def fox_attn(P, Kx, Vx, FT, B, T):
    tq = 256
    tk = 512
    nq = T // tq
    wb = FOX_HG * FX_DH
    qb = OFF_FXQ // wb
    return pl.pallas_call(
        functools.partial(_fox_attn_kernel, tq=tq, tk=tk), grid=(B, FX_H // FOX_HG, nq),
        in_specs=[pl.BlockSpec((tq, wb), lambda b, h, i: (b * nq + i, qb + h)),
                  pl.BlockSpec((T, wb), lambda b, h, i: (b, h)),
                  pl.BlockSpec((T, wb), lambda b, h, i: (b, h)),
                  pl.BlockSpec((None, SUBLANES, T), lambda b, h, i: (b, SM_FX // SUBLANES, 0))],
        out_specs=pl.BlockSpec((tq, wb), lambda b, h, i: (b * nq + i, h)),
        out_shape=jax.ShapeDtypeStruct((B * T, FX_W), bf16),
        scratch_shapes=[pltpu.VMEM((FOX_HG, tq, LANES), f32), pltpu.VMEM((FOX_HG, tq, LANES), f32),
                        pltpu.VMEM((tq, wb), f32)],
        compiler_params=_cparams("parallel", "parallel", "arbitrary"), name="fox_attn")(P, Kx, Vx, FT)


def _fox_suffix_kernel(pt_ref, lf_ref, triu_ref, later_ref, o_ref, x_ref, *, n_pages):
    b = pl.program_id(0)

    def gather(j, carry):
        x_ref[pl.ds(pl.multiple_of(j * FX_H, FX_H), FX_H), :] = lf_ref[pt_ref[b, j]]
        return carry

    lax.fori_loop(0, n_pages, gather, 0)
    x = x_ref[...]
    within = _dot_r(x, triu_ref[...])
    tot = jnp.broadcast_to(jnp.sum(x, axis=-1, keepdims=True), x.shape)
    o_ref[...] = within + _dot_l(later_ref[...], tot)


def fox_suffix(page_table, cache_lft, triu, later, layer):
    B, n_pages = page_table.shape
    n_pool = cache_lft.shape[1]
    rows = n_pages * FX_H
    grid_spec = pltpu.PrefetchScalarGridSpec(
        num_scalar_prefetch=1, grid=(B,),
        in_specs=[pl.BlockSpec((None, n_pool, FX_H, PAGE_SIZE), lambda b, pt: (layer, 0, 0, 0)),
                  pl.BlockSpec((PAGE_SIZE, PAGE_SIZE), lambda b, pt: (0, 0)),
                  pl.BlockSpec((rows, rows), lambda b, pt: (0, 0))],
        out_specs=pl.BlockSpec((None, rows, PAGE_SIZE), lambda b, pt: (b, 0, 0)),
        scratch_shapes=[pltpu.VMEM((rows, PAGE_SIZE), f32)])
    return pl.pallas_call(
        functools.partial(_fox_suffix_kernel, n_pages=n_pages), grid_spec=grid_spec,
        out_shape=jax.ShapeDtypeStruct((B, rows, PAGE_SIZE), f32),
        compiler_params=_cparams("parallel"), name="fox_suffix")(page_table, cache_lft, triu, later)


DEC_PP = 8


def _fox_decode_kernel(pt_ref, q_ref, *refs, n_steps, pp, t_new):
    kc = refs[0:pp]
    vc = refs[pp:2 * pp]
    suf_ref, hmask_ref, kn_ref, vn_ref, fkn_ref, o_ref, m_ref, l_ref, acc_ref = refs[2 * pp:]
    j = pl.program_id(1)
    nq = FX_H * t_new
    cols = PAGE_SIZE * FX_H

    @pl.when(j == 0)
    def _():
        m_ref[...] = jnp.full_like(m_ref, NEG)
        l_ref[...] = jnp.zeros_like(l_ref)
        acc_ref[...] = jnp.zeros_like(acc_ref)

    qb = (q_ref[...] * (FX_DH ** -0.5)).astype(bf16)

    def update(ss, vs):
        m_new, l_new, alpha, ps = _softmax_step(ss, m_ref[...], l_ref[...])
        acc = alpha * acc_ref[...]
        for p, vb in zip(ps, vs):
            acc = acc + _dot(p, vb)
        l_ref[...] = l_new
        acc_ref[...] = acc
        m_ref[...] = m_new

    @pl.when(j < n_steps)
    def _():
        hmask = hmask_ref[...]
        ss = [_dot_nt(qb, kc[i][...].reshape(cols, FX_DH).astype(bf16)) + hmask + suf_ref[i] for i in range(pp)]
        update(ss, [vc[i][...].reshape(cols, FX_DH).astype(bf16) for i in range(pp)])

    @pl.when(j == n_steps)
    def _():
        s = _dot_nt(qb, kn_ref[...].astype(bf16)) - fkn_ref[...]
        col = _iota((nq, LANES), 1)
        row = _iota((nq, LANES), 0)
        ok = (col % FX_H == row // t_new) & (col // FX_H <= row % t_new)
        update([jnp.where(ok, s, NEG)], [vn_ref[...].astype(bf16)])
        o_ref[...] = acc_ref[...] / l_ref[...]


def fox_decode(page_table, q2, cache_k, cache_v, suffix, hmask, k_new, v_new, fkn, layer, t_new):
    B, n_pages = page_table.shape
    pp = DEC_PP
    n_steps = n_pages // pp
    nq = FX_H * t_new
    cols = PAGE_SIZE * FX_H

    def page_spec(i):
        return pl.BlockSpec((None, None, PAGE_SIZE, FX_H, FX_DH),
                            lambda b, j, pt: (layer, pt[b, jnp.minimum(j, n_steps - 1) * pp + i], 0, 0, 0))

    per_b = lambda shape: pl.BlockSpec((None,) + shape, lambda b, j, pt: (b, 0, 0))
    grid_spec = pltpu.PrefetchScalarGridSpec(
        num_scalar_prefetch=1, grid=(B, n_steps + 1),
        in_specs=[per_b((nq, FX_DH))] + [page_spec(i) for i in range(pp)] * 2
        + [pl.BlockSpec((None, pp, 1, cols), lambda b, j, pt: (b, jnp.minimum(j, n_steps - 1), 0, 0)),
           pl.BlockSpec((nq, cols), lambda b, j, pt: (0, 0)),
           per_b((LANES, FX_DH)), per_b((LANES, FX_DH)), per_b((1, LANES))],
        out_specs=per_b((nq, FX_DH)),
        scratch_shapes=[pltpu.VMEM((nq, LANES), f32), pltpu.VMEM((nq, LANES), f32), pltpu.VMEM((nq, FX_DH), f32)])
    return pl.pallas_call(
        functools.partial(_fox_decode_kernel, n_steps=n_steps, pp=pp, t_new=t_new), grid_spec=grid_spec,
        out_shape=jax.ShapeDtypeStruct((B, nq, FX_DH), f32),
        compiler_params=_cparams("parallel", "arbitrary"), name="fox_decode")(
            page_table, q2, *([cache_k] * pp), *([cache_v] * pp), suffix, hmask, k_new, v_new, fkn)


def _pad_cols(w, n):
    return jnp.pad(w, ((0, 0), (0, n - w.shape[1])))


def _rw_cols(x):
    W = RW_W
    pad = [(0, 0)] * (x.ndim - 1) + [(0, RW_LORA_PAD - R_W)]
    return jnp.concatenate([x[..., 0:3 * W], jnp.pad(x[..., 3 * W:3 * W + R_W], pad),
                            jnp.pad(x[..., 3 * W + R_W:3 * W + R_W + R_A], pad),
                            x[..., 3 * W + R_W + R_A:]], axis=-1)


def _rw_cols_inv(x):
    W = RW_W
    o = 3 * W
    return jnp.concatenate([x[..., 0:o], x[..., o:o + R_W], x[..., o + RW_LORA_PAD:o + RW_LORA_PAD + R_A],
                            x[..., o + 2 * RW_LORA_PAD:]], axis=-1)


def _w_in_split(w):
    o_ml = RW_COLS
    o_fx = o_ml + ML_COLS
    o_gt = o_fx + FX_COLS
    ml = w[:, o_ml:o_fx]
    fx = w[:, o_fx:o_gt]
    ml_main = jnp.concatenate([ml[:, 0:ML_QK + ML_W], ml[:, ML_QK + ML_W + 2 * ML_H:]], axis=1)
    small = _pad_cols(jnp.concatenate([ml[:, ML_QK + ML_W:ML_QK + ML_W + 2 * ML_H], fx[:, 3 * FX_W:]], axis=1), LANES)
    main = jnp.concatenate([_rw_cols(w[:, 0:RW_COLS]), ml_main, fx[:, 0:FX_W]], axis=1)
    return (main.astype(bf16), w[:, o_gt:].astype(bf16), fx[:, FX_W:2 * FX_W].astype(bf16),
            fx[:, 2 * FX_W:3 * FX_W].astype(bf16), small.astype(bf16))


def _pad_rows(w, n):
    return jnp.pad(w, ((0, n - w.shape[0]), (0, 0)))


def _prev_rows8(buf, width):
    B, r, C = buf.shape
    return jnp.pad(buf, ((0, 0), (SUBLANES - r, 0), (0, width - C)))


def _layer(x, mod, st, past, W, page_table, l, B, T):
    N = B * T
    shift1, scale1, gate1, shift2, scale2, gate2 = jnp.split(mod, N_MOD, axis=-1)
    u = norm_mod(x, W['norm_pre_mix'], scale1, shift1, T)
    P = matmul(u, W['w_in'], tn=1024)
    G = matmul(u, W['w_in_gate'], tn=1024, out_dtype=bf16, gate=True)
    Kx = matmul(u, W['w_in_k'], tn=FX_W)
    Vx = matmul(u, W['w_in_v'], tn=FX_W)
    SM = matmul(u, W['w_in_small'], tn=LANES)

    padded = T % CHUNK != 0
    Tp = T if not padded else CHUNK

    def pad_t(a):
        if not padded:
            return a
        return jnp.pad(a.reshape(B, T, -1), ((0, 0), (0, Tp - T), (0, 0))).reshape(B * Tp, -1)

    def unpad_t(a):
        if not padded:
            return a
        return a.reshape(B, Tp, -1)[:, :T].reshape(N, -1)

    sp8 = _prev_rows8(_rw_cols(st['rw_shift'])[:, None, :], RW_PCOLS)
    prep = rwkv_prep(P, OFF_RW // RW_PCOLS, sp8, W['rw_mu'], W['rw_w0'], W['rw_a0'], W['rw_k_k'], W['rw_k_a'],
                     W['rw_r_k'], W['rw_w_up'], W['rw_a_up'], W['rw_g_up'], W['ones_blk'], T)
    nkk, wr, w_, b_, k2, v_rw, br, kr, rkr, g_rw = prep
    s0 = st['rw_wkv'].reshape(B, N_PAIR, 2, RW_DH, RW_DH).transpose(0, 1, 3, 2, 4).reshape(B, N_PAIR, RW_DH, LANES)
    seq = [pad_t(a).reshape(B, Tp, RW_W) for a in (nkk, wr, w_, b_, k2, br, v_rw)]
    z, s_out = rwkv_scan(*seq, s0, W['ones_blk'][0:LANES, 0:LANES], T)
    z = unpad_t(z.reshape(B * Tp, RW_W))
    o_rw = rwkv_post(z, v_rw, kr, rkr, g_rw, W['rw_ln_w'], W['rw_ln_b'], W['ones_blk'])
    rw_wkv = s_out.reshape(B, N_PAIR, RW_DH, 2, RW_DH).transpose(0, 1, 3, 2, 4).reshape(B, RW_H, RW_DH, RW_DH)
    P3 = P.reshape(B, T, P_COLS)
    rw_shift = _rw_cols_inv(P3[:, T - 1, OFF_RW:OFF_RW + RW_PCOLS])

    buf8 = _prev_rows8(st['ml_conv'], ML_QK)
    c0t = jnp.swapaxes(st['ml_c'], -1, -2)
    n0 = st['ml_n'][:, :, None, :]
    m0 = jnp.broadcast_to(st['ml_m'][:, :, None, None], (B, ML_H, 1, LANES))
    if padded:
        main_arr, main_blk = pad_t(P[:, OFF_ML:OFF_ML + ML_PCOLS]), 0
        small_arr, small_blk = pad_t(SM), 0
    else:
        main_arr, main_blk, small_arr, small_blk = P, OFF_ML // ML_PCOLS, SM, 0
    o_ml, lf_all, F, FT, ct, n_new, m_new = mlstm_fox_prep(
        main_arr, main_blk, small_arr, small_blk, buf8, W['small_bias'], W['ml_conv_w'], W['ml_conv_b'],
        W['ml_gn_w'], W['tri'], c0t, n0, m0, B, Tp, T)
    o_ml = unpad_t(o_ml)
    ml_c = jnp.swapaxes(ct, -1, -2)
    ml_n = n_new[:, :, 0, :]
    ml_m = m_new[:, :, 0, 0]
    ml_conv = P3[:, T - (ML_CONV - 1):, OFF_ML:OFF_ML + ML_QK]
    fox_logf = unpad_t(lf_all)[:, SM_FX:SM_FX + FX_H].reshape(B, T, FX_H)
    fox_k = Kx.reshape(B, T, FX_H, FX_DH)
    fox_v = Vx.reshape(B, T, FX_H, FX_DH)

    if past is None:
        o_fx = fox_attn(P, Kx, Vx, FT, B, T)
    else:
        cache_k, cache_v, cache_lft = past
        n_pages = page_table.shape[1]
        nq = FX_H * T
        assert nq <= LANES and n_pages % DEC_PP == 0
        suffix = fox_suffix(page_table, cache_lft, W['triu'], W['later'], l)
        suffix = suffix.reshape(B, n_pages, FX_H, PAGE_SIZE).transpose(0, 1, 3, 2).reshape(
            B, n_pages, 1, PAGE_SIZE * FX_H)
        q2 = P[:, OFF_FXQ:OFF_FXQ + FX_W].reshape(B, T, FX_H, FX_DH).transpose(0, 2, 1, 3).reshape(B, nq, FX_DH)
        padk = lambda a: jnp.pad(a.reshape(B, nq, FX_DH), ((0, 0), (0, LANES - nq), (0, 0)))
        fkn = F.reshape(B, Tp, LANES)[:, :T, SM_FX:SM_FX + FX_H].reshape(B, 1, nq)
        fkn = jnp.pad(fkn, ((0, 0), (0, 0), (0, LANES - nq)))
        hmask = jnp.where(jnp.arange(PAGE_SIZE * FX_H)[None, :] % FX_H == jnp.arange(nq)[:, None] // T, 0.0, NEG)
        o = fox_decode(page_table, q2, cache_k, cache_v, suffix, hmask.astype(f32), padk(fox_k), padk(fox_v),
                       fkn, l, T)
        o_fx = o.reshape(B, FX_H, T, FX_DH).transpose(0, 2, 1, 3).reshape(N, FX_W)

    merged = merge(o_rw, o_ml, o_fx, W['w_br_rwkv'], W['w_br_mlstm'], W['w_br_fox'], G)
    x = mm_norm_res(merged, W['w_out'], l, x, W['norm_post_mix'], gate1, T)

    zf = norm_mod(x, W['norm_pre_ffn'], scale2, shift2, T)
    fbuf8 = _prev_rows8(st['ffn_conv'], D_FF_PAD)
    if T % BF16_ROWS == 0:
        hmid, tails = ffn_up(zf, W['ffn_w_gv'], l, fbuf8, W['ffn_conv_w'], W['ffn_conv_b'], T)
        per = N // B // min(T, 1024)
        ffn_conv = tails.reshape(B, per, SUBLANES, D_FF_PAD)[:, -1, SUBLANES - (FFN_CONV - 1):, 0:D_FF]
    else:
        av = matmul(zf, W['ffn_w_gv'], tn=1024, layer=l)
        hmid = ffn_act(av, fbuf8, W['ffn_conv_w'], W['ffn_conv_b'], T)
        ffn_conv = av[:, 0:D_FF].reshape(B, T, D_FF)[:, T - (FFN_CONV - 1):]
    x = mm_norm_res(hmid, W['ffn_w_down'], l, x, W['norm_post_ffn'], gate2, T)

    new = dict(fox_k=fox_k, fox_v=fox_v, fox_logf=fox_logf, rw_shift=rw_shift, rw_wkv=rw_wkv,
               ml_conv=ml_conv, ml_c=ml_c, ml_n=ml_n, ml_m=ml_m, ffn_conv=ffn_conv)
    return x, new


STATE_NAMES = ("fox_k", "fox_v", "fox_logf", "rw_shift", "rw_wkv", "ml_conv", "ml_c", "ml_n", "ml_m", "ffn_conv")


def _layer_weights(Pm, l, n_pages):
    row = lambda v: v.reshape(1, -1)
    W = {}
    for name in ('norm_pre_mix', 'norm_post_mix', 'norm_pre_ffn', 'norm_post_ffn'):
        W[name] = Pm[name][l]
    W['w_in'], W['w_in_gate'], W['w_in_k'], W['w_in_v'], W['w_in_small'] = _w_in_split(Pm['w_in'][l])
    W['rw_mu'] = row(_rw_cols(Pm['rw_mu'][l]))
    for name in ('rw_w0', 'rw_a0', 'rw_k_k', 'rw_k_a', 'rw_r_k', 'rw_ln_w', 'rw_ln_b'):
        W[name] = row(Pm[name][l])
    W['rw_w_up'] = _pad_rows(Pm['rw_w_up'][l], RW_LORA_PAD).astype(bf16)
    W['rw_a_up'] = _pad_rows(Pm['rw_a_up'][l], RW_LORA_PAD).astype(bf16)
    W['rw_g_up'] = Pm['rw_g_up'][l].astype(bf16)
    hid = jnp.arange(RW_W) // RW_DH
    W['ones_blk'] = (hid[:, None] == hid[None, :]).astype(bf16)
    W['small_bias'] = row(jnp.pad(jnp.concatenate([Pm['ml_b_i'][l], Pm['ml_b_f'][l], Pm['fx_b_f'][l]]),
                                  (0, LANES - 2 * ML_H - FX_H)))
    W['ml_conv_w'] = _pad_rows(Pm['ml_conv_w'][l], SUBLANES)
    W['ml_conv_b'] = row(Pm['ml_conv_b'][l])
    W['ml_gn_w'] = row(Pm['ml_gn_w'][l])
    idx = jnp.arange(CHUNK)
    W['tri'] = (idx[None, :] <= idx[:, None]).astype(bf16)
    W['triu'] = (idx[:, None] > idx[None, :]).astype(bf16)
    if n_pages:
        r = jnp.arange(n_pages * FX_H)
        W['later'] = ((r[:, None] % FX_H == r[None, :] % FX_H) & (r[None, :] // FX_H > r[:, None] // FX_H)).astype(bf16)
    W['w_br_rwkv'] = Pm['w_br_rwkv'][l].astype(bf16)
    W['w_br_mlstm'] = Pm['w_br_mlstm'][l].astype(bf16)
    W['w_br_fox'] = Pm['w_br_fox'][l].astype(bf16)
    W['ffn_conv_w'] = _pad_rows(_pad_cols(Pm['ffn_conv_w'][l], D_FF_PAD), SUBLANES)
    W['ffn_conv_b'] = _pad_cols(row(Pm['ffn_conv_b'][l]), D_FF_PAD)
    return W


def _stacked_weights(w_out, ffn_w_gate, ffn_w_val, ffn_w_down):
    padc = lambda w: jnp.pad(w, ((0, 0), (0, 0), (0, D_FF_PAD - D_FF)))
    return dict(w_out=w_out.astype(bf16),
                ffn_w_gv=jnp.concatenate([padc(ffn_w_gate), padc(ffn_w_val)], axis=2).astype(bf16),
                ffn_w_down=jnp.pad(ffn_w_down, ((0, 0), (0, D_FF_PAD - D_FF), (0, 0))).astype(bf16))


def kernel(x_prompt, x_sample, cache_fox_k, cache_fox_v, cache_fox_logf, state_rwkv_shift, state_rwkv_wkv,
           state_mlstm_conv, state_mlstm_c, state_mlstm_n, state_mlstm_m, state_ffn_conv, page_table,
           c_prompt, c_sample, w_ada, b_ada, norm_pre_mix, norm_post_mix, norm_pre_ffn, norm_post_ffn, w_in,
           rw_mu, rw_w0, rw_w_up, rw_a0, rw_a_up, rw_g_up, rw_k_k, rw_k_a, rw_r_k, rw_ln_w, rw_ln_b,
           ml_conv_w, ml_conv_b, ml_b_i, ml_b_f, ml_gn_w, fx_b_f, w_br_rwkv, w_br_mlstm, w_br_fox, w_out,
           ffn_w_gate, ffn_w_val, ffn_conv_w, ffn_conv_b, ffn_w_down):
    Pm = dict(norm_pre_mix=norm_pre_mix, norm_post_mix=norm_post_mix, norm_pre_ffn=norm_pre_ffn,
              norm_post_ffn=norm_post_ffn, w_in=w_in, rw_mu=rw_mu, rw_w0=rw_w0, rw_w_up=rw_w_up, rw_a0=rw_a0,
              rw_a_up=rw_a_up, rw_g_up=rw_g_up, rw_k_k=rw_k_k, rw_k_a=rw_k_a,
              rw_r_k=rw_r_k.reshape(DEPTH, RW_W), rw_ln_w=rw_ln_w, rw_ln_b=rw_ln_b, ml_conv_w=ml_conv_w,
              ml_conv_b=ml_conv_b, ml_b_i=ml_b_i, ml_b_f=ml_b_f, ml_gn_w=ml_gn_w, fx_b_f=fx_b_f,
              w_br_rwkv=w_br_rwkv, w_br_mlstm=w_br_mlstm, w_br_fox=w_br_fox, w_out=w_out, ffn_w_gate=ffn_w_gate,
              ffn_w_val=ffn_w_val, ffn_conv_w=ffn_conv_w, ffn_conv_b=ffn_conv_b, ffn_w_down=ffn_w_down)
    Bp, Tpr, D = x_prompt.shape
    Bs, Ts, _ = x_sample.shape
    cache_k, cache_v = cache_fox_k, cache_fox_v
    cache_lft = jnp.swapaxes(cache_fox_logf, -1, -2)

    stacked = _stacked_weights(w_out, ffn_w_gate, ffn_w_val, ffn_w_down)
    zeros = lambda *s: jnp.zeros(s, f32)
    xp = x_prompt.reshape(Bp * Tpr, D)
    xs = x_sample.reshape(Bs * Ts, D)
    c_all = jnp.pad(jnp.concatenate([c_prompt, c_sample], axis=0), ((0, 16 - Bp - Bs), (0, 0)))
    new_p = {n: [] for n in STATE_NAMES}
    new_s = {n: [] for n in STATE_NAMES}
    for l in range(DEPTH):
        W = dict(_layer_weights(Pm, l, page_table.shape[1]), **stacked)
        mod = ada_mod(c_all, w_ada, b_ada[:, None, :], l)
        st_p = dict(rw_shift=zeros(Bp, RW_COLS), rw_wkv=zeros(Bp, RW_H, RW_DH, RW_DH),
                    ml_conv=zeros(Bp, ML_CONV - 1, ML_QK), ml_c=zeros(Bp, ML_H, ML_DH, ML_DH),
                    ml_n=zeros(Bp, ML_H, ML_DH), ml_m=zeros(Bp, ML_H), ffn_conv=zeros(Bp, FFN_CONV - 1, D_FF))
        st_s = dict(rw_shift=state_rwkv_shift[l], rw_wkv=state_rwkv_wkv[l], ml_conv=state_mlstm_conv[l],
                    ml_c=state_mlstm_c[l], ml_n=state_mlstm_n[l], ml_m=state_mlstm_m[l],
                    ffn_conv=state_ffn_conv[l])
        xp, lp = _layer(xp, mod[:Bp], st_p, None, W, None, l, Bp, Tpr)
        xs, ls = _layer(xs, mod[Bp:Bp + Bs], st_s, (cache_k, cache_v, cache_lft), W, page_table, l, Bs, Ts)
        for n in STATE_NAMES:
            new_p[n].append(lp[n])
            new_s[n].append(ls[n])
    sp = {n: jnp.stack(v) for n, v in new_p.items()}
    ss = {n: jnp.stack(v) for n, v in new_s.items()}
    return (xp.reshape(Bp, Tpr, D), xs.reshape(Bs, Ts, D),
            sp['fox_k'], ss['fox_k'], sp['fox_v'], ss['fox_v'], sp['fox_logf'], ss['fox_logf'],
            sp['rw_shift'], ss['rw_shift'], sp['rw_wkv'], ss['rw_wkv'],
            sp['ml_conv'], ss['ml_conv'], sp['ml_c'], ss['ml_c'], sp['ml_n'], ss['ml_n'], sp['ml_m'], ss['ml_m'],
            sp['ffn_conv'], ss['ffn_conv'])
```
